```python
import jax, jax.numpy as jnp
from jax import lax
import numpy as np


D_MODEL = 2048
BATCH = 4
SEQ = 2048
DEPTH = 2

HEAD_DIM = 128
ROPE_THETA = 500000.0
ROPE_DIM = HEAD_DIM // 4
NORM_EPS = 1e-6
NEG_INF = -1e30

NSA_HEADS = 8
NSA_KV_HEADS = 2
NSA_CMP_LEN = 32
NSA_CMP_STRIDE = 16
NSA_CMP_HIDDEN = 256
NSA_SEL_BLOCK = 64
NSA_SEL_TOPN = 16
NSA_WINDOW = 512
NSA_Q_CHUNK = 64
NSA_FORCE_BONUS = 1e4

DIL_GROUPS = ((128, 1), (512, 4), (2048, 16))
DIL_HEADS_PER_GROUP = 4
DIL_HEADS = DIL_HEADS_PER_GROUP * len(DIL_GROUPS)
DIL_Q_CHUNK = 128

MOBA_HEADS = 8
MOBA_BLOCK = 256
MOBA_TOPK = 3
MOBA_Q_CHUNK = 32

BAND_BLOCK = 128
D_FF = 4 * D_MODEL

A_Q = NSA_HEADS * HEAD_DIM
A_KV = NSA_KV_HEADS * HEAD_DIM
A_G = 3 * NSA_HEADS
B_QKV = DIL_HEADS * HEAD_DIM
C_QKV = MOBA_HEADS * HEAD_DIM
A_OUT = NSA_HEADS * HEAD_DIM
B_OUT = DIL_HEADS_PER_GROUP * HEAD_DIM
C_OUT = MOBA_HEADS * HEAD_DIM
IN_SPLIT_SIZES = (A_Q, A_KV, A_KV, A_KV, A_KV, A_KV, A_KV, A_G,
                  B_QKV, B_QKV, B_QKV, C_QKV, C_QKV, C_QKV,
                  D_MODEL, D_MODEL, D_MODEL)
IN_WIDTH = sum(IN_SPLIT_SIZES)

kernel_name = "nsa_dilated_moba_gated_hybrid"


def rms_norm(x, g):
    xf = x.astype(jnp.float32)
    y = xf * lax.rsqrt(jnp.mean(xf * xf, axis=-1, keepdims=True) + NORM_EPS)
    return (y * g.astype(jnp.float32)).astype(x.dtype)


def rope_tables(seq):
    inv = ROPE_THETA ** (-jnp.arange(0, ROPE_DIM, 2, dtype=jnp.float32) / ROPE_DIM)
    ang = jnp.arange(seq, dtype=jnp.float32)[:, None] * inv[None, :]
    return jnp.cos(ang), jnp.sin(ang)


def apply_partial_rope(x, cos, sin):
    half = ROPE_DIM // 2
    xf = x[..., :ROPE_DIM].astype(jnp.float32)
    x1, x2 = xf[..., :half], xf[..., half:]
    c = cos[None, :, None, :]
    s = sin[None, :, None, :]
    rot = jnp.concatenate([x1 * c - x2 * s, x2 * c + x1 * s], axis=-1).astype(x.dtype)
    return jnp.concatenate([rot, x[..., ROPE_DIM:]], axis=-1)


def masked_softmax(scores, mask):
    s = jnp.where(mask, scores, NEG_INF)
    m = jnp.max(s, axis=-1, keepdims=True)
    e = jnp.where(mask, jnp.exp(s - m), 0.0)
    den = jnp.maximum(jnp.sum(e, axis=-1, keepdims=True), 1e-30)
    return e / den, (m + jnp.log(den))[..., 0]


def banded_causal_attention(q, k, v, window):
    B, S, H, dh = q.shape
    G = k.shape[2]
    R = H // G
    nb = S // BAND_BLOCK
    span = window + BAND_BLOCK
    kp = jnp.pad(k, ((0, 0), (window, 0), (0, 0), (0, 0)))
    vp = jnp.pad(v, ((0, 0), (window, 0), (0, 0), (0, 0)))
    idx = np.arange(nb)[:, None] * BAND_BLOCK + np.arange(span)[None, :]
    kb = kp[:, idx]
    vb = vp[:, idx]
    qb = q.reshape(B, nb, BAND_BLOCK, G, R, dh)
    s = jnp.einsum('bitgrd,bisgd->bigrts', qb, kb, preferred_element_type=jnp.float32) * dh ** -0.5
    tpos = np.arange(nb)[:, None] * BAND_BLOCK + np.arange(BAND_BLOCK)[None, :]
    kpos = idx - window
    mask = ((kpos[:, None, :] <= tpos[:, :, None]) & (kpos[:, None, :] > tpos[:, :, None] - window)
            & (kpos[:, None, :] >= 0))
    p, _ = masked_softmax(s, mask[None, :, None, None])
    o = jnp.einsum('bigrts,bisgd->bitgrd', p.astype(vb.dtype), vb)
    return o.reshape(B, S, H, dh)


def nsa_compress(kv, pe, w1, w2):
    S = kv.shape[1]
    M = (S - NSA_CMP_LEN) // NSA_CMP_STRIDE + 1
    idx = np.arange(M)[:, None] * NSA_CMP_STRIDE + np.arange(NSA_CMP_LEN)[None, :]
    blocks = kv[:, idx] + pe[None, None, :, None, :]
    hid = jax.nn.gelu(jnp.einsum('bmlgd,ldf->bmgf', blocks, w1))
    return jnp.einsum('bmgf,fd->bmgd', hid, w2)


def cmp_to_sel_overlap(seq):
    M = (seq - NSA_CMP_LEN) // NSA_CMP_STRIDE + 1
    NB = seq // NSA_SEL_BLOCK
    cs = np.arange(M)[:, None] * NSA_CMP_STRIDE
    bs = np.arange(NB)[None, :] * NSA_SEL_BLOCK
    ov = np.clip(np.minimum(cs + NSA_CMP_LEN, bs + NSA_SEL_BLOCK) - np.maximum(cs, bs), 0, None) / NSA_CMP_LEN
    return jnp.asarray(ov, dtype=jnp.float32)


def nsa_mixer(q, k_cmp, v_cmp, k_sel, v_sel, k_win, v_win, gates,
              pe_k, w1_k, w2_k, pe_v, w1_v, w2_v, cos, sin):
    B, S, H, dh = q.shape
    G = NSA_KV_HEADS
    R = H // G
    scale = dh ** -0.5
    t_pos = jnp.arange(S)

    kc = nsa_compress(k_cmp, pe_k, w1_k, w2_k)
    vc = nsa_compress(v_cmp, pe_v, w1_v, w2_v)
    M = kc.shape[1]
    qg = q.reshape(B, S, G, R, dh)
    s_c = jnp.einsum('btgrd,bmgd->btgrm', qg, kc, preferred_element_type=jnp.float32) * scale
    cmp_end = jnp.arange(M) * NSA_CMP_STRIDE + NSA_CMP_LEN - 1
    mask_c = (cmp_end[None, :] <= t_pos[:, None])[None, :, None, None, :]
    p_c, _ = masked_softmax(s_c, mask_c)
    o_cmp = jnp.einsum('btgrm,bmgd->btgrd', p_c.astype(vc.dtype), vc).reshape(B, S, H, dh)

    NB = S // NSA_SEL_BLOCK
    imp = jnp.einsum('btgrm,mj->btgj', p_c, cmp_to_sel_overlap(S))
    blk = jnp.arange(NB)
    q_blk = t_pos // NSA_SEL_BLOCK
    causal_blk = blk[None, :] <= q_blk[:, None]
    forced = (blk[None, :] == 0) | (blk[None, :] == q_blk[:, None]) | (blk[None, :] == q_blk[:, None] - 1)
    bonus = jnp.where(forced, NSA_FORCE_BONUS, 0.0)
    score = jnp.where(causal_blk[None, :, None, :], imp + bonus[None, :, None, :], NEG_INF)
    n_sel = min(NSA_SEL_TOPN, NB)
    top_s, top_i = lax.top_k(score, n_sel)
    top_ok = top_s > NEG_INF * 0.5

    qr = apply_partial_rope(q, cos, sin)
    ks = apply_partial_rope(k_sel, cos, sin).reshape(B, NB, NSA_SEL_BLOCK, G, dh).transpose(0, 3, 1, 2, 4)
    vs = v_sel.reshape(B, NB, NSA_SEL_BLOCK, G, dh).transpose(0, 3, 1, 2, 4)
    bi = jnp.arange(B)[:, None, None, None]
    gi = jnp.arange(G)[None, None, :, None]
    Qc = NSA_Q_CHUNK
    nkey = n_sel * NSA_SEL_BLOCK

    def sel_chunk(c):
        t0 = c * Qc
        tc = t0 + jnp.arange(Qc)
        qc = lax.dynamic_slice_in_dim(qr, t0, Qc, axis=1).reshape(B, Qc, G, R, dh)
        ic = lax.dynamic_slice_in_dim(top_i, t0, Qc, axis=1)
        okc = lax.dynamic_slice_in_dim(top_ok, t0, Qc, axis=1)
        kg = ks[bi, gi, ic]
        vg = vs[bi, gi, ic]
        s = jnp.einsum('btgrd,btgnld->btgrnl', qc, kg, preferred_element_type=jnp.float32) * scale
        kpos = ic[..., None] * NSA_SEL_BLOCK + jnp.arange(NSA_SEL_BLOCK)
        mask = (kpos <= tc[None, :, None, None, None]) & okc[..., None]
        p, _ = masked_softmax(s.reshape(B, Qc, G, R, nkey), mask.reshape(B, Qc, G, 1, nkey))
        o = jnp.einsum('btgrk,btgkd->btgrd', p.astype(vg.dtype), vg.reshape(B, Qc, G, nkey, dh))
        return o.reshape(B, Qc, H, dh)

    o_sel = lax.map(sel_chunk, jnp.arange(S // Qc))
    o_sel = o_sel.transpose(1, 0, 2, 3, 4).reshape(B, S, H, dh)

    o_win = banded_causal_attention(qr, apply_partial_rope(k_win, cos, sin), v_win, NSA_WINDOW)

    return gates[..., 0:1] * o_cmp + gates[..., 1:2] * o_sel + gates[..., 2:3] * o_win


def dilated_group_attention(q, k, v, window, dilation):
    B, S, Hg, dh = q.shape
    nk = window // dilation + 1
    Qc = DIL_Q_CHUNK
    offs = dilation * jnp.arange(nk)
    scale = dh ** -0.5

    def chunk(c):
        t = c * Qc + jnp.arange(Qc)
        kidx = t[:, None] - offs[None, :]
        ok = kidx >= 0
        kidx = jnp.maximum(kidx, 0)
        qc = lax.dynamic_slice_in_dim(q, c * Qc, Qc, axis=1)
        kg = k[:, kidx]
        vg = v[:, kidx]
        s = jnp.einsum('bthd,btnhd->bthn', qc, kg, preferred_element_type=jnp.float32) * scale
        p, lse = masked_softmax(s, ok[None, :, None, :])
        o = jnp.einsum('bthn,btnhd->bthd', p.astype(vg.dtype), vg)
        return o, lse

    o, lse = lax.map(chunk, jnp.arange(S // Qc))
    o = o.transpose(1, 0, 2, 3, 4).reshape(B, S, Hg, dh)
    lse = lse.transpose(1, 0, 2, 3).reshape(B, S, Hg)
    return o, lse


def dilated_mixer(q, k, v):
    outs, lses = [], []
    for g, (w, r) in enumerate(DIL_GROUPS):
        sl = slice(g * DIL_HEADS_PER_GROUP, (g + 1) * DIL_HEADS_PER_GROUP)
        o, l = dilated_group_attention(q[:, :, sl], k[:, :, sl], v[:, :, sl], w, r)
        outs.append(o)
        lses.append(l)
    alpha = jax.nn.softmax(jnp.stack(lses, axis=0), axis=0)
    o_all = jnp.stack(outs, axis=0)
    return jnp.sum(alpha[..., None].astype(o_all.dtype) * o_all, axis=0)


def moba_mixer(q, k, v):
    B, S, H, dh = q.shape
    L = MOBA_BLOCK
    nbk = -(-S // L)
    pad = nbk * L - S
    kb = jnp.pad(k, ((0, 0), (0, pad), (0, 0), (0, 0))).reshape(B, nbk, L, H, dh)
    vb = jnp.pad(v, ((0, 0), (0, pad), (0, 0), (0, 0))).reshape(B, nbk, L, H, dh)
    kmean = jnp.mean(kb.astype(jnp.float32), axis=2)
    gate = jnp.einsum('bthd,bjhd->bthj', q.astype(jnp.float32), kmean)
    t_pos = jnp.arange(S)
    past = jnp.arange(nbk)[None, :] < (t_pos // L)[:, None]
    gate = jnp.where(past[None, :, None, :], gate, NEG_INF)
    ktop = min(MOBA_TOPK, nbk)
    top_s, top_i = lax.top_k(gate, ktop)
    top_ok = top_s > NEG_INF * 0.5
    kbt = kb.transpose(0, 3, 1, 2, 4)
    vbt = vb.transpose(0, 3, 1, 2, 4)
    bi = jnp.arange(B)[:, None, None, None]
    hi = jnp.arange(H)[None, None, :, None]
    Qc = MOBA_Q_CHUNK
    nsel = ktop * L
    scale = dh ** -0.5

    def chunk(c):
        t0 = c * Qc
        tc = t0 + jnp.arange(Qc)
        qc = lax.dynamic_slice_in_dim(q, t0, Qc, axis=1)
        ic = lax.dynamic_slice_in_dim(top_i, t0, Qc, axis=1)
        okc = lax.dynamic_slice_in_dim(top_ok, t0, Qc, axis=1)
        kg = kbt[bi, hi, ic].reshape(B, Qc, H, nsel, dh)
        vg = vbt[bi, hi, ic].reshape(B, Qc, H, nsel, dh)
        s_sel = jnp.einsum('bthd,bthkd->bthk', qc, kg, preferred_element_type=jnp.float32) * scale
        m_sel = jnp.broadcast_to(okc[..., None], (B, Qc, H, ktop, L)).reshape(B, Qc, H, nsel)
        own = t0 // L
        k_own = lax.dynamic_index_in_dim(kb, own, axis=1, keepdims=False)
        v_own = lax.dynamic_index_in_dim(vb, own, axis=1, keepdims=False)
        s_own = jnp.einsum('bthd,blhd->bthl', qc, k_own, preferred_element_type=jnp.float32) * scale
        m_own = (own * L + jnp.arange(L))[None, :] <= tc[:, None]
        m_own = jnp.broadcast_to(m_own[None, :, None, :], (B, Qc, H, L))
        p, _ = masked_softmax(jnp.concatenate([s_sel, s_own], axis=-1),
                              jnp.concatenate([m_sel, m_own], axis=-1))
        p = p.astype(v.dtype)
        return (jnp.einsum('bthk,bthkd->bthd', p[..., :nsel], vg)
                + jnp.einsum('bthl,blhd->bthd', p[..., nsel:], v_own))

    o = lax.map(chunk, jnp.arange(S // Qc))
    return o.transpose(1, 0, 2, 3, 4).reshape(B, S, H, dh)


def hybrid_layer(x, cos, sin, attn_g, w_in, pe_k, w1_k, w2_k, pe_v, w1_v, w2_v,
                 w_br_a, w_br_b, w_br_c, w_o, mlp_g, w_mlp_in, w_mlp_out):
    B, S, D = x.shape
    h = rms_norm(x, attn_g)
    proj = jnp.einsum('bsd,dn->bsn', h, w_in)
    split_points = np.cumsum(IN_SPLIT_SIZES)[:-1].tolist()
    (a_q, a_kc, a_vc, a_ks, a_vs, a_kw, a_vw, a_g,
     b_q, b_k, b_v, c_q, c_k, c_v, m_a, m_b, m_c) = jnp.split(proj, split_points, axis=-1)

    def heads(t, n):
        return t.reshape(B, S, n, HEAD_DIM)

    a_gates = jax.nn.sigmoid(a_g.astype(jnp.float32)).reshape(B, S, NSA_HEADS, 3).astype(x.dtype)
    o_a = nsa_mixer(heads(a_q, NSA_HEADS),
                    heads(a_kc, NSA_KV_HEADS), heads(a_vc, NSA_KV_HEADS),
                    heads(a_ks, NSA_KV_HEADS), heads(a_vs, NSA_KV_HEADS),
                    heads(a_kw, NSA_KV_HEADS), heads(a_vw, NSA_KV_HEADS),
                    a_gates, pe_k, w1_k, w2_k, pe_v, w1_v, w2_v, cos, sin)
    o_b = dilated_mixer(apply_partial_rope(heads(b_q, DIL_HEADS), cos, sin),
                        apply_partial_rope(heads(b_k, DIL_HEADS), cos, sin),
                        heads(b_v, DIL_HEADS))
    o_c = moba_mixer(apply_partial_rope(heads(c_q, MOBA_HEADS), cos, sin),
                     apply_partial_rope(heads(c_k, MOBA_HEADS), cos, sin),
                     heads(c_v, MOBA_HEADS))

    y_a = jnp.einsum('bsk,kd->bsd', o_a.reshape(B, S, A_OUT), w_br_a)
    y_b = jnp.einsum('bsk,kd->bsd', o_b.reshape(B, S, B_OUT), w_br_b)
    y_c = jnp.einsum('bsk,kd->bsd', o_c.reshape(B, S, C_OUT), w_br_c)
    merged = jax.nn.sigmoid(m_a) * y_a + jax.nn.sigmoid(m_b) * y_b + jax.nn.sigmoid(m_c) * y_c
    x = x + jnp.einsum('bsd,de->bse', merged, w_o)

    h2 = rms_norm(x, mlp_g)
    u = jnp.square(jax.nn.relu(jnp.einsum('bsd,df->bsf', h2, w_mlp_in)))
    return x + jnp.einsum('bsf,fd->bsd', u, w_mlp_out)


def setup_inputs(seed: int = 0) -> dict:
    key = jax.random.key(seed)
    ks = jax.random.split(key, 18)
    f32 = jnp.float32
    D = D_MODEL

    def nrm(k, shape, scale):
        return jax.random.normal(k, shape, f32) * scale

    return {
        "x": nrm(ks[0], (BATCH, SEQ, D), 1.0),
        "attn_norm_g": 1.0 + nrm(ks[1], (DEPTH, D), 0.02),
        "w_in": nrm(ks[2], (DEPTH, D, IN_WIDTH), D ** -0.5),
        "cmp_pe_k": nrm(ks[3], (DEPTH, NSA_CMP_LEN, HEAD_DIM), 0.02),
        "cmp_w1_k": nrm(ks[4], (DEPTH, NSA_CMP_LEN, HEAD_DIM, NSA_CMP_HIDDEN), (NSA_CMP_LEN * HEAD_DIM) ** -0.5),
        "cmp_w2_k": nrm(ks[5], (DEPTH, NSA_CMP_HIDDEN, HEAD_DIM), NSA_CMP_HIDDEN ** -0.5),
        "cmp_pe_v": nrm(ks[6], (DEPTH, NSA_CMP_LEN, HEAD_DIM), 0.02),
        "cmp_w1_v": nrm(ks[7], (DEPTH, NSA_CMP_LEN, HEAD_DIM, NSA_CMP_HIDDEN), (NSA_CMP_LEN * HEAD_DIM) ** -0.5),
        "cmp_w2_v": nrm(ks[8], (DEPTH, NSA_CMP_HIDDEN, HEAD_DIM), NSA_CMP_HIDDEN ** -0.5),
        "w_br_a": nrm(ks[9], (DEPTH, A_OUT, D), A_OUT ** -0.5),
        "w_br_b": nrm(ks[10], (DEPTH, B_OUT, D), B_OUT ** -0.5),
        "w_br_c": nrm(ks[11], (DEPTH, C_OUT, D), C_OUT ** -0.5),
        "w_o": nrm(ks[12], (DEPTH, D, D), D ** -0.5),
        "mlp_norm_g": 1.0 + nrm(ks[13], (DEPTH, D), 0.02),
        "w_mlp_in": nrm(ks[14], (DEPTH, D, D_FF), D ** -0.5),
        "w_mlp_out": nrm(ks[15], (DEPTH, D_FF, D), D_FF ** -0.5),
        "final_norm_g": 1.0 + nrm(ks[16], (D,), 0.02),
    }


def reference(x, attn_norm_g, w_in, cmp_pe_k, cmp_w1_k, cmp_w2_k, cmp_pe_v, cmp_w1_v, cmp_w2_v,
              w_br_a, w_br_b, w_br_c, w_o, mlp_norm_g, w_mlp_in, w_mlp_out, final_norm_g):
    cos, sin = rope_tables(x.shape[1])
    for l in range(DEPTH):
        x = hybrid_layer(x, cos, sin, attn_norm_g[l], w_in[l],
                         cmp_pe_k[l], cmp_w1_k[l], cmp_w2_k[l], cmp_pe_v[l], cmp_w1_v[l], cmp_w2_v[l],
                         w_br_a[l], w_br_b[l], w_br_c[l], w_o[l],
                         mlp_norm_g[l], w_mlp_in[l], w_mlp_out[l])
    return rms_norm(x, final_norm_g)
```

```python
import functools

import numpy as np
import jax
import jax.numpy as jnp
from jax import lax
from jax.experimental import pallas as pl
from jax.experimental.pallas import tpu as pltpu

D_MODEL = 2048
SEQ = 2048
HEAD_DIM = 128
ROPE_THETA = 500000.0
ROPE_DIM = HEAD_DIM // 4
ROPE_HALF = ROPE_DIM // 2
NORM_EPS = 1e-6
NEG_INF = -1e30

NSA_HEADS = 8
NSA_KV_HEADS = 2
NSA_REP = NSA_HEADS // NSA_KV_HEADS
NSA_CMP_LEN = 32
NSA_CMP_STRIDE = 16
NSA_CMP_HIDDEN = 256
NSA_SEL_BLOCK = 64
NSA_SEL_TOPN = 16
NSA_WINDOW = 512
NSA_FORCE_BONUS = 1e4
NSA_NB = SEQ // NSA_SEL_BLOCK
NSA_M_PAD = SEQ // NSA_CMP_STRIDE

DIL_GROUPS = ((128, 1), (512, 4), (2048, 16))
DIL_HEADS_PER_GROUP = 4
DIL_HEADS = DIL_HEADS_PER_GROUP * len(DIL_GROUPS)
DIL_TILE = 128

MOBA_HEADS = 8
MOBA_BLOCK = 256
MOBA_TOPK = 3
MOBA_NB = SEQ // MOBA_BLOCK

D_FF = 4 * D_MODEL
A_Q = NSA_HEADS * HEAD_DIM
A_KV = NSA_KV_HEADS * HEAD_DIM
A_G = 3 * NSA_HEADS
B_QKV = DIL_HEADS * HEAD_DIM
C_QKV = MOBA_HEADS * HEAD_DIM
IN_SPLIT_SIZES = (A_Q, A_KV, A_KV, A_KV, A_KV, A_KV, A_KV, A_G,
                  B_QKV, B_QKV, B_QKV, C_QKV, C_QKV, C_QKV,
                  D_MODEL, D_MODEL, D_MODEL)

LANES = 128
VMEM_LIMIT = 56 * 1024 * 1024

BF16 = jnp.bfloat16
F32 = jnp.float32


def _params(*sem):
    return pltpu.CompilerParams(dimension_semantics=sem, vmem_limit_bytes=VMEM_LIMIT)


def _dot(a, b):
    return jnp.dot(a, b, preferred_element_type=F32)


def _dot_nt(a, b):
    return lax.dot_general(a, b, (((1,), (1,)), ((), ())), preferred_element_type=F32)


def _rmsnorm_rows(x, g):
    y = x * lax.rsqrt(jnp.mean(x * x, axis=-1, keepdims=True) + NORM_EPS)
    return y * g


def _rmsnorm_kernel(x_ref, g_ref, h_ref):
    h_ref[...] = _rmsnorm_rows(x_ref[...], g_ref[...]).astype(h_ref.dtype)


def rmsnorm(x2, g, out_dtype, tm=512):
    m, d = x2.shape
    return pl.pallas_call(
        _rmsnorm_kernel,
        grid=(m // tm,),
        in_specs=[pl.BlockSpec((tm, d), lambda i: (i, 0)), pl.BlockSpec((1, d), lambda i: (0, 0))],
        out_specs=pl.BlockSpec((tm, d), lambda i: (i, 0)),
        out_shape=jax.ShapeDtypeStruct((m, d), out_dtype),
        compiler_params=_params("parallel"),
        name="rmsnorm",
    )(x2, g.reshape(1, d))


def _rope_lanes(acc, c, s_up, s_dn):
    tn = acc.shape[1]
    reps = tn // HEAD_DIM
    if reps > 1:
        c = jnp.concatenate([c] * reps, axis=1)
        s_up = jnp.concatenate([s_up] * reps, axis=1)
        s_dn = jnp.concatenate([s_dn] * reps, axis=1)
    up = pltpu.roll(acc, tn - ROPE_HALF, axis=1)
    dn = pltpu.roll(acc, ROPE_HALF, axis=1)
    return acc * c + up * s_up + dn * s_dn


def _proj_kernel(*refs, mode):
    if mode in ("rope", "both"):
        h_ref, w_ref, c_ref, su_ref, sd_ref = refs[:5]
        outs = refs[5:]
    else:
        h_ref, w_ref = refs[:2]
        outs = refs[2:]
    acc = _dot(h_ref[...], w_ref[...])
    if mode == "plain":
        outs[0][...] = acc.astype(outs[0].dtype)
    elif mode == "sigmoid":
        outs[0][...] = jax.nn.sigmoid(acc).astype(outs[0].dtype)
    else:
        roped = _rope_lanes(acc, c_ref[...], su_ref[...], sd_ref[...])
        if mode == "both":
            outs[0][...] = acc.astype(outs[0].dtype)
            outs[1][...] = roped.astype(outs[1].dtype)
        else:
            outs[0][...] = roped.astype(outs[0].dtype)


def project(h, w, mode, out_dtype, rope_tabs=None, tm=1024, tn=512):
    m, k = h.shape
    n = w.shape[1]
    tn = min(tn, n)
    assert m % tm == 0 and n % tn == 0 and SEQ % tm == 0
    in_specs = [pl.BlockSpec((tm, k), lambda j, i: (i, 0)),
                pl.BlockSpec((k, tn), lambda j, i: (0, j))]
    args = [h, w]
    if mode in ("rope", "both"):
        pos_blocks = SEQ // tm
        for t in rope_tabs:
            in_specs.append(pl.BlockSpec((tm, HEAD_DIM), lambda j, i: (i % pos_blocks, 0)))
            args.append(t)
    n_out = 2 if mode == "both" else 1
    out_spec = pl.BlockSpec((tm, tn), lambda j, i: (i, j))
    out_shape = jax.ShapeDtypeStruct((m, n), out_dtype)
    res = pl.pallas_call(
        functools.partial(_proj_kernel, mode=mode),
        grid=(n // tn, m // tm),
        in_specs=in_specs,
        out_specs=[out_spec] * n_out,
        out_shape=[out_shape] * n_out,
        compiler_params=_params("parallel", "parallel"),
        name="proj_" + mode,
    )(*args)
    return res if n_out == 2 else res[0]


def _compress_kernel(k0_ref, k1_ref, v0_ref, v1_ref, pek_ref, w1k_ref, w2k_ref, pev_ref, w1v_ref, w2v_ref,
                     out_ref):
    half = NSA_CMP_STRIDE * HEAD_DIM
    for idx, src_ref in enumerate((k0_ref, k1_ref, v0_ref, v1_ref)):
        is_k = idx < NSA_KV_HEADS
        pe_ref, w1_ref, w2_ref = (pek_ref, w1k_ref, w2k_ref) if is_k else (pev_ref, w1v_ref, w2v_ref)
        x = jnp.concatenate(
            [src_ref[pl.ds(l, NSA_M_PAD, stride=NSA_CMP_STRIDE), :]
             for l in range(NSA_CMP_STRIDE)], axis=1)
        pe = pe_ref[...]
        first = _dot((x + pe[:, :half]).astype(BF16), w1_ref[:half, :])
        second = _dot((x + pe[:, half:]).astype(BF16), w1_ref[half:, :])
        hid = jax.nn.gelu(first + pltpu.roll(second, NSA_M_PAD - 1, axis=0))
        out_ref[idx] = _dot(hid.astype(BF16), w2_ref[...]).astype(out_ref.dtype)


def nsa_compress(pf, pe_k, w1_k, w2_k, pe_v, w1_v, w2_v, batch):
    flat = NSA_CMP_LEN * HEAD_DIM
    const = lambda shape: pl.BlockSpec(shape, lambda b: (0,) * len(shape))
    return pl.pallas_call(
        _compress_kernel,
        grid=(batch,),
        in_specs=[pl.BlockSpec((SEQ, HEAD_DIM), lambda b, c=c: (b, c)) for c in range(4)] + [
                  const((1, flat)), const((flat, NSA_CMP_HIDDEN)), const((NSA_CMP_HIDDEN, HEAD_DIM)),
                  const((1, flat)), const((flat, NSA_CMP_HIDDEN)), const((NSA_CMP_HIDDEN, HEAD_DIM))],
        out_specs=pl.BlockSpec((None, 4, NSA_M_PAD, HEAD_DIM), lambda b: (b, 0, 0, 0)),
        out_shape=jax.ShapeDtypeStruct((batch, 4, NSA_M_PAD, HEAD_DIM), BF16),
        compiler_params=_params("parallel"),
        name="nsa_compress",
    )(pf, pf, pf, pf, pe_k, w1_k, w2_k, pe_v, w1_v, w2_v)


NSA_TQ = 256
NSA_KC = 256


def _softmax_update(s, v, m, l, acc):
    m_new = jnp.maximum(m, jnp.max(s, axis=-1, keepdims=True))
    alpha = jnp.exp(m - m_new)
    p = jnp.exp(s - m_new)
    l = alpha * l + jnp.sum(p, axis=-1, keepdims=True)
    acc = alpha * acc + _dot(p.astype(BF16), v)
    return m_new, l, acc


def _nsa_kernel(qraw_ref, qrot_ref, kc_ref, vc_ref, ksel_ref, vsel_ref, kwin_ref, vwin_ref,
                gate_ref, ov_ref, expand_ref, o_ref, bias_scr):
    i = pl.program_id(2)
    tq, kc_w, rep = NSA_TQ, NSA_KC, NSA_REP
    t0 = i * tq
    scale = HEAD_DIM ** -0.5
    lane = lax.broadcasted_iota(jnp.int32, (tq, LANES), 1)
    t_pos = t0 + lax.broadcasted_iota(jnp.int32, (tq, LANES), 0)

    kc = kc_ref[...]
    vc = vc_ref[...]
    ov = ov_ref[...]
    mask_c = (lane * NSA_CMP_STRIDE + (NSA_CMP_LEN - 1)) <= t_pos
    o_cmp = []
    imp = jnp.zeros((tq, LANES), F32)
    for r in range(rep):
        q = qraw_ref[:, r * HEAD_DIM:(r + 1) * HEAD_DIM]
        s = jnp.where(mask_c, _dot_nt(q, kc) * scale, NEG_INF)
        m = jnp.max(s, axis=-1, keepdims=True)
        e = jnp.where(mask_c, jnp.exp(s - m), 0.0)
        den = jnp.maximum(jnp.sum(e, axis=-1, keepdims=True), 1e-30)
        p = (e / den).astype(BF16)
        o_cmp.append(_dot(p, vc))
        imp = imp + _dot(p, ov)

    q_blk = t_pos // NSA_SEL_BLOCK
    causal_blk = lane <= q_blk
    forced = (lane == 0) | (lane == q_blk) | (lane == q_blk - 1)
    score = jnp.where(causal_blk, imp + jnp.where(forced, NSA_FORCE_BONUS, 0.0), NEG_INF)
    rank = jnp.zeros((tq, LANES), F32)
    for j in range(NSA_NB):
        sj = score[:, j:j + 1]
        ahead = (sj > score) | ((sj == score) & (lane > j))
        rank = rank + jnp.where(ahead, 1.0, 0.0)
    chosen = (rank < float(NSA_SEL_TOPN)) & (score > NEG_INF * 0.5)
    chosen_keys = _dot(jnp.where(chosen, 1.0, 0.0).astype(BF16), expand_ref[...])
    key_pos = lax.broadcasted_iota(jnp.int32, (tq, SEQ), 1)
    row_pos = t0 + lax.broadcasted_iota(jnp.int32, (tq, SEQ), 0)
    bias_scr[...] = jnp.where((chosen_keys > 0.5) & (key_pos <= row_pos), 0.0, NEG_INF)

    q_stack = jnp.concatenate([qrot_ref[:, r * HEAD_DIM:(r + 1) * HEAD_DIM] for r in range(rep)], axis=0)
    init = (jnp.full((rep * tq, 1), NEG_INF, F32), jnp.zeros((rep * tq, 1), F32),
            jnp.zeros((rep * tq, HEAD_DIM), F32))

    def add_bias(s, bias):
        return (s.reshape(rep, tq, kc_w) + bias[None]).reshape(rep * tq, kc_w)

    def sel_body(kj, carry):
        off = pl.multiple_of(kj * kc_w, kc_w)
        s = _dot_nt(q_stack, ksel_ref[pl.ds(off, kc_w), :]) * scale
        s = add_bias(s, bias_scr[:, pl.ds(off, kc_w)])
        return _softmax_update(s, vsel_ref[pl.ds(off, kc_w), :], *carry)

    _, l_sel, acc_sel = lax.fori_loop(0, i + 1, sel_body, init)

    row_minus_col = (lax.broadcasted_iota(jnp.int32, (tq, kc_w), 0)
                     - lax.broadcasted_iota(jnp.int32, (tq, kc_w), 1))

    def win_body(kj, carry):
        off = pl.multiple_of(kj * kc_w, kc_w)
        s = _dot_nt(q_stack, kwin_ref[pl.ds(off, kc_w), :]) * scale
        dist = row_minus_col + (t0 - off)
        bias = jnp.where((dist >= 0) & (dist < NSA_WINDOW), 0.0, NEG_INF)
        return _softmax_update(add_bias(s, bias), vwin_ref[pl.ds(off, kc_w), :], *carry)

    first = jnp.maximum(i - NSA_WINDOW // kc_w, 0)
    _, l_win, acc_win = lax.fori_loop(first, i + 1, win_body, init)

    o_sel = acc_sel / l_sel
    o_win = acc_win / l_win
    gates = gate_ref[...]
    outs = []
    for r in range(rep):
        rows = slice(r * tq, (r + 1) * tq)
        g_cmp = gates[:, 3 * r:3 * r + 1]
        g_sel = gates[:, 3 * r + 1:3 * r + 2]
        g_win = gates[:, 3 * r + 2:3 * r + 3]
        outs.append(g_cmp * o_cmp[r] + g_sel * o_sel[rows] + g_win * o_win[rows])
    o_ref[...] = jnp.concatenate(outs, axis=1).astype(o_ref.dtype)


def nsa_attention(q_raw, q_rot, cmp_kv, kr, pv, gates_a, overlap, expand, batch):
    tq = NSA_TQ
    nq = SEQ // tq
    g_n = NSA_KV_HEADS
    row = lambda b, g, i: (b * nq + i, g)
    seq_col = lambda c0: (lambda b, g, i: (b, c0 + g))
    const2 = lambda b, g, i: (0, 0)
    return pl.pallas_call(
        _nsa_kernel,
        grid=(batch, g_n, nq),
        in_specs=[pl.BlockSpec((tq, NSA_REP * HEAD_DIM), row),
                  pl.BlockSpec((tq, NSA_REP * HEAD_DIM), row),
                  pl.BlockSpec((None, None, NSA_M_PAD, HEAD_DIM), lambda b, g, i: (b, g, 0, 0)),
                  pl.BlockSpec((None, None, NSA_M_PAD, HEAD_DIM), lambda b, g, i: (b, g_n + g, 0, 0)),
                  pl.BlockSpec((SEQ, HEAD_DIM), seq_col(0)),
                  pl.BlockSpec((SEQ, HEAD_DIM), seq_col(0)),
                  pl.BlockSpec((SEQ, HEAD_DIM), seq_col(g_n)),
                  pl.BlockSpec((SEQ, HEAD_DIM), seq_col(g_n)),
                  pl.BlockSpec((tq, LANES), row),
                  pl.BlockSpec((NSA_M_PAD, LANES), const2),
                  pl.BlockSpec((LANES, SEQ), const2)],
        out_specs=pl.BlockSpec((tq, NSA_REP * HEAD_DIM), row),
        out_shape=jax.ShapeDtypeStruct((batch * SEQ, A_Q), BF16),
        scratch_shapes=[pltpu.VMEM((tq, SEQ), F32)],
        compiler_params=_params("parallel", "parallel", "parallel"),
        name="nsa_attention",
    )(q_raw, q_rot, cmp_kv, cmp_kv, kr, pv, kr, pv, gates_a, overlap, expand)


def _dilated_kernel(q0, k0, v0, q1, k1, v1, q2, k2, v2, o_ref,
                    o0_scr, o1_scr, o2_scr, l0_scr, l1_scr, l2_scr):
    tile = DIL_TILE
    scale = HEAD_DIM ** -0.5
    groups = ((q0, k0, v0, o0_scr, l0_scr), (q1, k1, v1, o1_scr, l1_scr), (q2, k2, v2, o2_scr, l2_scr))
    for (window, dil), (q_ref, k_ref, v_ref, og_scr, lg_scr) in zip(DIL_GROUPS, groups):
        assert window // dil == tile
        per_class = SEQ // dil
        tiles_per_class = per_class // tile
        nk = tile if tiles_per_class == 1 else 2 * tile
        a_minus_a = (lax.broadcasted_iota(jnp.int32, (tile, nk), 0)
                     - lax.broadcasted_iota(jnp.int32, (tile, nk), 1))

        def tile_body(t, carry, q_ref=q_ref, k_ref=k_ref, v_ref=v_ref, og_scr=og_scr, lg_scr=lg_scr,
                      dil=dil, tiles_per_class=tiles_per_class, nk=nk, a_minus_a=a_minus_a):
            cls = t // tiles_per_class
            n0 = (t % tiles_per_class) * tile
            kbase = jnp.maximum(n0 - (nk - tile), 0)
            q_rows = pl.ds(cls + dil * n0, tile, stride=dil)
            k_rows = pl.ds(cls + dil * kbase, nk, stride=dil)
            q = q_ref[q_rows, :].astype(BF16)
            k = k_ref[k_rows, :].astype(BF16)
            v = v_ref[k_rows, :].astype(BF16)
            s = _dot_nt(q, k) * scale
            dist = a_minus_a + (n0 - kbase)
            mask = (dist >= 0) & (dist <= tile)
            s = jnp.where(mask, s, NEG_INF)
            m = jnp.max(s, axis=-1, keepdims=True)
            e = jnp.where(mask, jnp.exp(s - m), 0.0)
            den = jnp.maximum(jnp.sum(e, axis=-1, keepdims=True), 1e-30)
            o = _dot((e / den).astype(BF16), v)
            og_scr[q_rows, :] = o
            lg_scr[q_rows, :] = jnp.broadcast_to(m + jnp.log(den), (tile, HEAD_DIM))
            return carry

        lax.fori_loop(0, SEQ // tile, tile_body, 0)

    rows = 256

    def merge_body(c, carry):
        sl = pl.ds(pl.multiple_of(c * rows, rows), rows)
        la, lb, lc = l0_scr[sl, :], l1_scr[sl, :], l2_scr[sl, :]
        mx = jnp.maximum(jnp.maximum(la, lb), lc)
        ea, eb, ec = jnp.exp(la - mx), jnp.exp(lb - mx), jnp.exp(lc - mx)
        tot = ea + eb + ec
        out = (ea / tot) * o0_scr[sl, :] + (eb / tot) * o1_scr[sl, :] + (ec / tot) * o2_scr[sl, :]
        o_ref[sl, :] = out.astype(o_ref.dtype)
        return carry

    lax.fori_loop(0, SEQ // rows, merge_body, 0)


def dilated_attention(rb, pf, batch):
    hp = DIL_HEADS_PER_GROUP
    in_specs, args = [], []
    for g in range(len(DIL_GROUPS)):
        for arr, c0 in ((rb, 0), (rb, DIL_HEADS), (pf, 4)):
            in_specs.append(pl.BlockSpec((SEQ, HEAD_DIM), lambda b, j, c0=c0, g=g: (b, c0 + g * hp + j)))
            args.append(arr)
    return pl.pallas_call(
        _dilated_kernel,
        grid=(batch, hp),
        in_specs=in_specs,
        out_specs=pl.BlockSpec((SEQ, HEAD_DIM), lambda b, j: (b, j)),
        out_shape=jax.ShapeDtypeStruct((batch * SEQ, hp * HEAD_DIM), BF16),
        scratch_shapes=[pltpu.VMEM((SEQ, HEAD_DIM), F32)] * 6,
        compiler_params=_params("parallel", "parallel"),
        name="dilated_attention",
    )(*args)


def _moba_kernel(q_ref, k_ref, v_ref, o_ref, kmean_scr):
    i = pl.program_id(2)
    tq = MOBA_BLOCK
    scale = HEAD_DIM ** -0.5

    @pl.when(i == 0)
    def _():
        blk = lax.broadcasted_iota(jnp.int32, (LANES, SEQ), 0)
        pos = lax.broadcasted_iota(jnp.int32, (LANES, SEQ), 1)
        avg = jnp.where(pos // MOBA_BLOCK == blk, 1.0 / MOBA_BLOCK, 0.0).astype(BF16)
        kmean_scr[...] = _dot(avg, k_ref[...]).astype(kmean_scr.dtype)

    q = q_ref[...]
    lane = lax.broadcasted_iota(jnp.int32, (tq, LANES), 1)
    gate = jnp.where(lane < i, _dot_nt(q, kmean_scr[...]), NEG_INF)
    rank = jnp.zeros((tq, LANES), F32)
    for j in range(MOBA_NB):
        gj = gate[:, j:j + 1]
        ahead = (gj > gate) | ((gj == gate) & (lane > j))
        rank = rank + jnp.where(ahead, 1.0, 0.0)
    chosen = jnp.where((rank < float(MOBA_TOPK)) & (gate > NEG_INF * 0.5), 1.0, 0.0)

    own = pl.multiple_of(i * tq, tq)
    s = _dot_nt(q, k_ref[pl.ds(own, tq), :]) * scale
    causal = (lax.broadcasted_iota(jnp.int32, (tq, tq), 1) <= lax.broadcasted_iota(jnp.int32, (tq, tq), 0))
    s = jnp.where(causal, s, NEG_INF)
    init = _softmax_update(s, v_ref[pl.ds(own, tq), :],
                           jnp.full((tq, 1), NEG_INF, F32), jnp.zeros((tq, 1), F32),
                           jnp.zeros((tq, HEAD_DIM), F32))

    def body(kj, carry):
        off = pl.multiple_of(kj * tq, tq)
        picked = jnp.sum(jnp.where(lane == kj, chosen, 0.0), axis=-1, keepdims=True)
        s = _dot_nt(q, k_ref[pl.ds(off, tq), :]) * scale + jnp.where(picked > 0.5, 0.0, NEG_INF)
        return _softmax_update(s, v_ref[pl.ds(off, tq), :], *carry)

    _, l, acc = lax.fori_loop(0, i, body, init)
    o_ref[...] = (acc / l).astype(o_ref.dtype)


def moba_attention(kr, pv, batch):
    tq = MOBA_BLOCK
    nq = SEQ // tq
    q_c0 = 2 * NSA_KV_HEADS
    k_c0 = q_c0 + MOBA_HEADS
    v_c0 = 2 * NSA_KV_HEADS
    return pl.pallas_call(
        _moba_kernel,
        grid=(batch, MOBA_HEADS, nq),
        in_specs=[pl.BlockSpec((tq, HEAD_DIM), lambda b, h, i: (b * nq + i, q_c0 + h)),
                  pl.BlockSpec((SEQ, HEAD_DIM), lambda b, h, i: (b, k_c0 + h)),
                  pl.BlockSpec((SEQ, HEAD_DIM), lambda b, h, i: (b, v_c0 + h))],
        out_specs=pl.BlockSpec((tq, HEAD_DIM), lambda b, h, i: (b * nq + i, h)),
        out_shape=jax.ShapeDtypeStruct((batch * SEQ, C_QKV), BF16),
        scratch_shapes=[pltpu.VMEM((LANES, HEAD_DIM), BF16)],
        compiler_params=_params("parallel", "parallel", "arbitrary"),
        name="moba_attention",
    )(kr, kr, pv)


def _merge_kernel(oa_ref, ob_ref, oc_ref, ga_ref, gb_ref, gc_ref, wa_ref, wb_ref, wc_ref, out_ref):
    y = ga_ref[...] * _dot(oa_ref[...], wa_ref[...])
    y = y + gb_ref[...] * _dot(ob_ref[...], wb_ref[...])
    y = y + gc_ref[...] * _dot(oc_ref[...], wc_ref[...])
    out_ref[...] = y.astype(out_ref.dtype)


def gated_merge(o_a, o_b, o_c, gates_m, w_a, w_b, w_c, tm=1024, tn=512):
    m = o_a.shape[0]
    nb = D_MODEL // tn
    act = lambda width: pl.BlockSpec((tm, width), lambda j, i: (i, 0))
    gate = lambda g: pl.BlockSpec((tm, tn), lambda j, i, g=g: (i, g * nb + j))
    wgt = lambda width: pl.BlockSpec((width, tn), lambda j, i: (0, j))
    return pl.pallas_call(
        _merge_kernel,
        grid=(nb, m // tm),
        in_specs=[act(o_a.shape[1]), act(o_b.shape[1]), act(o_c.shape[1]),
                  gate(0), gate(1), gate(2),
                  wgt(w_a.shape[0]), wgt(w_b.shape[0]), wgt(w_c.shape[0])],
        out_specs=pl.BlockSpec((tm, tn), lambda j, i: (i, j)),
        out_shape=jax.ShapeDtypeStruct((m, D_MODEL), BF16),
        compiler_params=_params("parallel", "parallel"),
        name="gated_merge",
    )(o_a, o_b, o_c, gates_m, gates_m, gates_m, w_a, w_b, w_c)


def _out_proj_kernel(y_ref, w_ref, x_ref, g_ref, xo_ref, h_ref):
    x_new = x_ref[...] + _dot(y_ref[...], w_ref[...])
    xo_ref[...] = x_new
    h_ref[...] = _rmsnorm_rows(x_new, g_ref[...]).astype(h_ref.dtype)


def out_proj_residual_norm(y, w_o, x2, g, tm=512):
    m, d = x2.shape
    row = pl.BlockSpec((tm, d), lambda i: (i, 0))
    return pl.pallas_call(
        _out_proj_kernel,
        grid=(m // tm,),
        in_specs=[row, pl.BlockSpec((d, d), lambda i: (0, 0)), row, pl.BlockSpec((1, d), lambda i: (0, 0))],
        out_specs=[row, row],
        out_shape=[jax.ShapeDtypeStruct((m, d), F32), jax.ShapeDtypeStruct((m, d), BF16)],
        compiler_params=_params("parallel"),
        name="out_proj",
    )(y, w_o, x2, g.reshape(1, d))


def _mlp_kernel(h_ref, w1_ref, w2_ref, x_ref, g_ref, xo_ref, hn_ref, acc_scr):
    f = pl.program_id(1)

    @pl.when(f == 0)
    def _():
        acc_scr[...] = jnp.zeros_like(acc_scr)

    u = jnp.square(jnp.maximum(_dot(h_ref[...], w1_ref[...]), 0.0))
    acc_scr[...] += _dot(u.astype(BF16), w2_ref[...])

    @pl.when(f == pl.num_programs(1) - 1)
    def _():
        x_new = x_ref[...] + acc_scr[...]
        xo_ref[...] = x_new
        hn_ref[...] = _rmsnorm_rows(x_new, g_ref[...]).astype(hn_ref.dtype)


def mlp_residual_norm(h2, w1, w2, x2, g_next, next_dtype, tm=512, tf=512):
    m, d = x2.shape
    row = pl.BlockSpec((tm, d), lambda i, f: (i, 0))
    return pl.pallas_call(
        _mlp_kernel,
        grid=(m // tm, D_FF // tf),
        in_specs=[row, pl.BlockSpec((d, tf), lambda i, f: (0, f)), pl.BlockSpec((tf, d), lambda i, f: (f, 0)),
                  row, pl.BlockSpec((1, d), lambda i, f: (0, 0))],
        out_specs=[row, row],
        out_shape=[jax.ShapeDtypeStruct((m, d), F32), jax.ShapeDtypeStruct((m, d), next_dtype)],
        scratch_shapes=[pltpu.VMEM((tm, d), F32)],
        compiler_params=_params("parallel", "arbitrary"),
        name="mlp",
    )(h2, w1, w2, x2, g_next.reshape(1, d))


def _rope_tables():
    inv = ROPE_THETA ** (-jnp.arange(0, ROPE_DIM, 2, dtype=F32) / ROPE_DIM)
    ang = jnp.arange(SEQ, dtype=F32)[:, None] * inv[None, :]
    cos, sin = jnp.cos(ang), jnp.sin(ang)
    zeros = jnp.zeros((SEQ, HEAD_DIM - ROPE_DIM), F32)
    zero_h = jnp.zeros((SEQ, ROPE_HALF), F32)
    c = jnp.concatenate([cos, cos, jnp.ones_like(zeros)], axis=1)
    s_up = jnp.concatenate([-sin, zero_h, zeros], axis=1)
    s_dn = jnp.concatenate([zero_h, sin, zeros], axis=1)
    return c, s_up, s_dn


def _overlap_table():
    cs = np.arange(NSA_M_PAD)[:, None] * NSA_CMP_STRIDE
    bs = np.arange(LANES)[None, :] * NSA_SEL_BLOCK
    ov = np.clip(np.minimum(cs + NSA_CMP_LEN, bs + NSA_SEL_BLOCK) - np.maximum(cs, bs), 0, None) / NSA_CMP_LEN
    ov[:, NSA_NB:] = 0.0
    ov[NSA_M_PAD - 1, :] = 0.0
    return jnp.asarray(ov, dtype=BF16)


def _expand_table():
    e = (np.arange(SEQ)[None, :] // NSA_SEL_BLOCK) == np.arange(LANES)[:, None]
    return jnp.asarray(e, dtype=BF16)


def _split_w_in(w):
    pts = np.cumsum((0,) + IN_SPLIT_SIZES)
    names = ("a_q", "a_kc", "a_vc", "a_ks", "a_vs", "a_kw", "a_vw", "a_g",
             "b_q", "b_k", "b_v", "c_q", "c_k", "c_v", "m_a", "m_b", "m_c")
    return {n: w[:, pts[i]:pts[i + 1]] for i, n in enumerate(names)}


def _layer(x2, h, batch, tabs, overlap, expand, w_in, pe_k, w1_k, w2_k, pe_v, w1_v, w2_v,
           w_br_a, w_br_b, w_br_c, w_o, mlp_g, w_mlp_in, w_mlp_out, g_next, next_dtype):
    w = _split_w_in(w_in)
    cat = lambda names: jnp.concatenate([w[n] for n in names], axis=1).astype(BF16)
    per_group = 3 * NSA_REP
    a_g = jnp.concatenate(
        [jnp.pad(w["a_g"][:, g * per_group:(g + 1) * per_group], ((0, 0), (0, LANES - per_group)))
         for g in range(NSA_KV_HEADS)], axis=1).astype(BF16)

    q_raw, q_rot = project(h, w["a_q"].astype(BF16), "both", BF16, tabs)
    kr = project(h, cat(("a_ks", "a_kw", "c_q", "c_k")), "rope", BF16, tabs)
    rb = project(h, cat(("b_q", "b_k")), "rope", F32, tabs)
    pv = project(h, cat(("a_vs", "a_vw", "c_v")), "plain", BF16)
    pf = project(h, cat(("a_kc", "a_vc", "b_v")), "plain", F32)
    gates_m = project(h, cat(("m_a", "m_b", "m_c")), "sigmoid", F32)
    gates_a = project(h, a_g, "sigmoid", F32)

    flat = NSA_CMP_LEN * HEAD_DIM
    cmp_kv = nsa_compress(pf, pe_k.reshape(1, flat), w1_k.reshape(flat, NSA_CMP_HIDDEN).astype(BF16),
                          w2_k.astype(BF16), pe_v.reshape(1, flat),
                          w1_v.reshape(flat, NSA_CMP_HIDDEN).astype(BF16), w2_v.astype(BF16), batch)
    o_a = nsa_attention(q_raw, q_rot, cmp_kv, kr, pv, gates_a, overlap, expand, batch)
    o_b = dilated_attention(rb, pf, batch)
    o_c = moba_attention(kr, pv, batch)

    merged = gated_merge(o_a, o_b, o_c, gates_m, w_br_a.astype(BF16), w_br_b.astype(BF16), w_br_c.astype(BF16))
    x2, h2 = out_proj_residual_norm(merged, w_o.astype(BF16), x2, mlp_g)
    return mlp_residual_norm(h2, w_mlp_in.astype(BF16), w_mlp_out.astype(BF16), x2, g_next, next_dtype)


def kernel(x, attn_norm_g, w_in, cmp_pe_k, cmp_w1_k, cmp_w2_k, cmp_pe_v, cmp_w1_v, cmp_w2_v,
           w_br_a, w_br_b, w_br_c, w_o, mlp_norm_g, w_mlp_in, w_mlp_out, final_norm_g):
    batch, seq, d = x.shape
    assert seq == SEQ and d == D_MODEL
    depth = w_in.shape[0]
    tabs = _rope_tables()
    overlap = _overlap_table()
    expand = _expand_table()
    x2 = x.reshape(batch * seq, d)
    h = rmsnorm(x2, attn_norm_g[0], BF16)
    for l in range(depth):
        last = l == depth - 1
        g_next = final_norm_g if last else attn_norm_g[l + 1]
        x2, h = _layer(x2, h, batch, tabs, overlap, expand, w_in[l],
                       cmp_pe_k[l], cmp_w1_k[l], cmp_w2_k[l], cmp_pe_v[l], cmp_w1_v[l], cmp_w2_v[l],
                       w_br_a[l], w_br_b[l], w_br_c[l], w_o[l], mlp_norm_g[l], w_mlp_in[l], w_mlp_out[l],
                       g_next, F32 if last else BF16)
    return h.reshape(batch, seq, d)
```

```python
import functools
import math

import numpy as np
import jax
import jax.numpy as jnp
from jax import lax
from jax.experimental import pallas as pl
from jax.experimental.pallas import tpu as pltpu

D_MODEL = 2048
SEQ = 2048
HEAD_DIM = 128
ROPE_THETA = 500000.0
ROPE_DIM = HEAD_DIM // 4
ROPE_HALF = ROPE_DIM // 2
NORM_EPS = 1e-6
NEG_INF = -1e30
Q_SCALE = HEAD_DIM ** -0.5 * math.log2(math.e)

NSA_HEADS = 8
NSA_KV_HEADS = 2
NSA_REP = NSA_HEADS // NSA_KV_HEADS
NSA_CMP_LEN = 32
NSA_CMP_STRIDE = 16
NSA_CMP_HIDDEN = 256
NSA_SEL_BLOCK = 64
NSA_SEL_TOPN = 16
NSA_WINDOW = 512
NSA_FORCE_BONUS = 1e4
NSA_NB = SEQ // NSA_SEL_BLOCK
NSA_M_PAD = SEQ // NSA_CMP_STRIDE

DIL_GROUPS = ((128, 1), (512, 4), (2048, 16))
DIL_HEADS_PER_GROUP = 4
DIL_HEADS = DIL_HEADS_PER_GROUP * len(DIL_GROUPS)
DIL_TILE = 128
DIL_UNROLL = 4

MOBA_HEADS = 8
MOBA_BLOCK = 256
MOBA_TOPK = 3
MOBA_NB = SEQ // MOBA_BLOCK

D_FF = 4 * D_MODEL
A_Q = NSA_HEADS * HEAD_DIM
A_KV = NSA_KV_HEADS * HEAD_DIM
A_G = 3 * NSA_HEADS
B_QKV = DIL_HEADS * HEAD_DIM
C_QKV = MOBA_HEADS * HEAD_DIM
IN_SPLIT_SIZES = (A_Q, A_KV, A_KV, A_KV, A_KV, A_KV, A_KV, A_G,
                  B_QKV, B_QKV, B_QKV, C_QKV, C_QKV, C_QKV,
                  D_MODEL, D_MODEL, D_MODEL)

LANES = 128
VMEM_LIMIT = 56 * 1024 * 1024

BF16 = jnp.bfloat16
F32 = jnp.float32


def _params(*sem):
    return pltpu.CompilerParams(dimension_semantics=sem, vmem_limit_bytes=VMEM_LIMIT)


def _dot(a, b):
    return jnp.dot(a, b, preferred_element_type=F32)


def _dot_nt(a, b):
    return lax.dot_general(a, b, (((1,), (1,)), ((), ())), preferred_element_type=F32)


def _iota(shape, axis):
    return lax.broadcasted_iota(jnp.int32, shape, axis)


def _rmsnorm_rows(x, g):
    y = x * lax.rsqrt(jnp.mean(x * x, axis=-1, keepdims=True) + NORM_EPS)
    return y * g


def _rmsnorm_kernel(x_ref, g_ref, h_ref):
    h_ref[...] = _rmsnorm_rows(x_ref[...], g_ref[...]).astype(h_ref.dtype)


def rmsnorm(x2, g, out_dtype, tm=512):
    m, d = x2.shape
    return pl.pallas_call(
        _rmsnorm_kernel,
        grid=(m // tm,),
        in_specs=[pl.BlockSpec((tm, d), lambda i: (i, 0)), pl.BlockSpec((1, d), lambda i: (0, 0))],
        out_specs=pl.BlockSpec((tm, d), lambda i: (i, 0)),
        out_shape=jax.ShapeDtypeStruct((m, d), out_dtype),
        compiler_params=_params("parallel"),
        name="rmsnorm",
    )(x2, g.reshape(1, d))


def _rope_lanes(acc, c, s_up, s_dn):
    tn = acc.shape[1]
    reps = tn // HEAD_DIM
    if reps > 1:
        c = jnp.concatenate([c] * reps, axis=1)
        s_up = jnp.concatenate([s_up] * reps, axis=1)
        s_dn = jnp.concatenate([s_dn] * reps, axis=1)
    up = pltpu.roll(acc, tn - ROPE_HALF, axis=1)
    dn = pltpu.roll(acc, ROPE_HALF, axis=1)
    return acc * c + up * s_up + dn * s_dn


def _proj_kernel(*refs, mode):
    if mode in ("rope", "both"):
        h_ref, w_ref, c_ref, su_ref, sd_ref, cs_ref = refs[:6]
        outs = refs[6:]
    else:
        h_ref, w_ref = refs[:2]
        outs = refs[2:]
    acc = _dot(h_ref[...], w_ref[...])
    if mode == "plain":
        outs[0][...] = acc.astype(outs[0].dtype)
    elif mode == "sigmoid":
        outs[0][...] = jax.nn.sigmoid(acc).astype(outs[0].dtype)
    else:
        col_scale = cs_ref[...]
        roped = _rope_lanes(acc, c_ref[...], su_ref[...], sd_ref[...]) * col_scale
        if mode == "both":
            outs[0][...] = (acc * col_scale).astype(outs[0].dtype)
            outs[1][...] = roped.astype(outs[1].dtype)
        else:
            outs[0][...] = roped.astype(outs[0].dtype)


def project(h, w, mode, out_dtype, rope_tabs=None, col_scale=None, tm=1024, tn=512):
    m, k = h.shape
    n = w.shape[1]
    tn = min(tn, n)
    assert m % tm == 0 and n % tn == 0 and SEQ % tm == 0
    in_specs = [pl.BlockSpec((tm, k), lambda j, i: (i, 0)),
                pl.BlockSpec((k, tn), lambda j, i: (0, j))]
    args = [h, w]
    if mode in ("rope", "both"):
        pos_blocks = SEQ // tm
        for t in rope_tabs:
            in_specs.append(pl.BlockSpec((tm, HEAD_DIM), lambda j, i: (i % pos_blocks, 0)))
            args.append(t)
        in_specs.append(pl.BlockSpec((1, tn), lambda j, i: (0, j)))
        args.append(col_scale)
    n_out = 2 if mode == "both" else 1
    out_spec = pl.BlockSpec((tm, tn), lambda j, i: (i, j))
    out_shape = jax.ShapeDtypeStruct((m, n), out_dtype)
    res = pl.pallas_call(
        functools.partial(_proj_kernel, mode=mode),
        grid=(n // tn, m // tm),
        in_specs=in_specs,
        out_specs=[out_spec] * n_out,
        out_shape=[out_shape] * n_out,
        compiler_params=_params("parallel", "parallel"),
        name="proj_" + mode,
    )(*args)
    return res if n_out == 2 else res[0]


def _proj_t_kernel(wt_ref, h_ref, out_ref):
    out_ref[...] = _dot_nt(wt_ref[...], h_ref[...]).astype(out_ref.dtype)


def project_transposed(h, wt, out_dtype, tm=1024, tn=512):
    m, k = h.shape
    n = wt.shape[0]
    return pl.pallas_call(
        _proj_t_kernel,
        grid=(n // tn, m // tm),
        in_specs=[pl.BlockSpec((tn, k), lambda j, i: (j, 0)), pl.BlockSpec((tm, k), lambda j, i: (i, 0))],
        out_specs=pl.BlockSpec((tn, tm), lambda j, i: (j, i)),
        out_shape=jax.ShapeDtypeStruct((n, m), out_dtype),
        compiler_params=_params("parallel", "parallel"),
        name="proj_transposed",
    )(wt, h)


def _compress_kernel(k0_ref, k1_ref, v0_ref, v1_ref, pek_ref, w1k_ref, w2k_ref, pev_ref, w1v_ref, w2v_ref,
                     out_ref):
    half = NSA_CMP_STRIDE * HEAD_DIM
    for idx, src_ref in enumerate((k0_ref, k1_ref, v0_ref, v1_ref)):
        is_k = idx < NSA_KV_HEADS
        pe_ref, w1_ref, w2_ref = (pek_ref, w1k_ref, w2k_ref) if is_k else (pev_ref, w1v_ref, w2v_ref)
        x = jnp.concatenate(
            [src_ref[pl.ds(l, NSA_M_PAD, stride=NSA_CMP_STRIDE), :]
             for l in range(NSA_CMP_STRIDE)], axis=1)
        pe = pe_ref[...]
        first = _dot((x + pe[:, :half]).astype(BF16), w1_ref[:half, :])
        second = _dot((x + pe[:, half:]).astype(BF16), w1_ref[half:, :])
        hid = jax.nn.gelu(first + pltpu.roll(second, NSA_M_PAD - 1, axis=0))
        out = _dot(hid.astype(BF16), w2_ref[...])
        out_ref[idx] = (out if is_k else jnp.transpose(out)).astype(out_ref.dtype)


def nsa_compress(pf, pe_k, w1_k, w2_k, pe_v, w1_v, w2_v, batch):
    flat = NSA_CMP_LEN * HEAD_DIM
    const = lambda shape: pl.BlockSpec(shape, lambda b: (0,) * len(shape))
    return pl.pallas_call(
        _compress_kernel,
        grid=(batch,),
        in_specs=[pl.BlockSpec((SEQ, HEAD_DIM), lambda b, c=c: (b, c)) for c in range(4)] + [
                  const((1, flat)), const((flat, NSA_CMP_HIDDEN)), const((NSA_CMP_HIDDEN, HEAD_DIM)),
                  const((1, flat)), const((flat, NSA_CMP_HIDDEN)), const((NSA_CMP_HIDDEN, HEAD_DIM))],
        out_specs=pl.BlockSpec((None, 4, NSA_M_PAD, HEAD_DIM), lambda b: (b, 0, 0, 0)),
        out_shape=jax.ShapeDtypeStruct((batch, 4, NSA_M_PAD, HEAD_DIM), BF16),
        compiler_params=_params("parallel"),
        name="nsa_compress",
    )(pf, pf, pf, pf, pe_k, w1_k, w2_k, pe_v, w1_v, w2_v)


NSA_TQ = 256
NSA_KC = 256
NSA_BLK_PER_CHUNK = NSA_KC // NSA_SEL_BLOCK


def _nsa_kernel(qraw_ref, qrot_ref, kc_ref, vct_ref, ksel_ref, vselt_ref, kwin_ref, vwint_ref,
                gate_ref, ovt_ref, o_ref, bias_scr):
    i = pl.program_id(2)
    tq, kc_w, rep, d = NSA_TQ, NSA_KC, NSA_REP, HEAD_DIM
    width = rep * tq
    bpc = NSA_BLK_PER_CHUNK
    t0 = i * tq
    stack = lambda ref: jnp.concatenate([ref[:, r * d:(r + 1) * d] for r in range(rep)], axis=0)
    q_raw = stack(qraw_ref)
    q_rot = stack(qrot_ref)

    def q_pos(rows):
        return t0 + (_iota((rows, width), 1) & (tq - 1))

    s = _dot_nt(kc_ref[...], q_raw)
    vis = (_iota((NSA_M_PAD, width), 0) * NSA_CMP_STRIDE + (NSA_CMP_LEN - 1)) <= q_pos(NSA_M_PAD)
    s = jnp.where(vis, s, NEG_INF)
    e = jnp.where(vis, jnp.exp2(s - jnp.max(s, axis=0, keepdims=True)), 0.0)
    den = jnp.maximum(jnp.sum(e, axis=0, keepdims=True), 1e-30)
    p = (e / den).astype(BF16)
    o_cmp = _dot(vct_ref[...], p)
    imp_heads = _dot(ovt_ref[...], p)
    imp = imp_heads[:, :tq]
    for r in range(1, rep):
        imp = imp + imp_heads[:, r * tq:(r + 1) * tq]

    blk = _iota((NSA_NB, tq), 0)
    q_blk = (t0 + _iota((NSA_NB, tq), 1)) // NSA_SEL_BLOCK
    forced = (blk == 0) | (blk == q_blk) | (blk == q_blk - 1)
    score = jnp.where(blk <= q_blk, imp + jnp.where(forced, NSA_FORCE_BONUS, 0.0), NEG_INF)
    rank = jnp.zeros((NSA_NB, tq), F32)
    for j in range(NSA_NB):
        sj = score[j:j + 1, :]
        ahead = (sj > score) | ((sj == score) & (blk > j))
        rank = rank + jnp.where(ahead, 1.0, 0.0)
    bias = jnp.where((rank < float(NSA_SEL_TOPN)) & (score > NEG_INF * 0.5), 0.0, NEG_INF)
    bias = jnp.concatenate([bias] * rep, axis=1)
    for c in range(SEQ // kc_w):
        bias_scr[c] = bias[c * bpc:(c + 1) * bpc, :]

    def chunk_scores(kref, kj, mask):
        off = pl.multiple_of(kj * kc_w, kc_w)
        s = _dot_nt(kref[pl.ds(off, kc_w), :], q_rot)
        return s if mask is None else jnp.where(mask, s, NEG_INF)

    def sel_step(kj, s, carry):
        m, l, acc = carry
        off = pl.multiple_of(kj * kc_w, kc_w)
        b = bias_scr[kj]
        s3 = s.reshape(bpc, NSA_SEL_BLOCK, width)
        m_new = jnp.maximum(m, jnp.max(jnp.max(s3, axis=1) + b, axis=0, keepdims=True))
        alpha = jnp.exp2(m - m_new)
        p = jnp.exp2(s3 - (m_new - b)[:, None, :]).reshape(kc_w, width)
        l = alpha * l + jnp.sum(p, axis=0, keepdims=True)
        acc = alpha * acc + _dot(vselt_ref[:, pl.ds(off, kc_w)], p.astype(BF16))
        return m_new, l, acc

    init = (jnp.full((1, width), NEG_INF, F32), jnp.zeros((1, width), F32), jnp.zeros((d, width), F32))
    carry = lax.fori_loop(0, i, lambda kj, c: sel_step(kj, chunk_scores(ksel_ref, kj, None), c), init)

    key_row = _iota((kc_w, width), 0)
    q_col = _iota((kc_w, width), 1) & (tq - 1)
    causal = key_row <= q_col
    _, l_sel, acc_sel = sel_step(i, chunk_scores(ksel_ref, i, causal), carry)

    far = i - 2
    near = i - 1
    far_ok = jnp.where(far >= 0, 0.0, NEG_INF)
    near_ok = jnp.where(near >= 0, 0.0, NEG_INF)
    far_c = jnp.maximum(far, 0)
    near_c = jnp.maximum(near, 0)
    s_own = chunk_scores(kwin_ref, i, causal)
    s_near = chunk_scores(kwin_ref, near_c, None) + near_ok
    s_far = chunk_scores(kwin_ref, far_c, key_row > q_col) + far_ok
    m_w = jnp.maximum(jnp.maximum(jnp.max(s_own, axis=0, keepdims=True), jnp.max(s_near, axis=0, keepdims=True)),
                      jnp.max(s_far, axis=0, keepdims=True))
    p_own = jnp.exp2(s_own - m_w)
    p_near = jnp.exp2(s_near - m_w)
    p_far = jnp.exp2(s_far - m_w)
    l_win = (jnp.sum(p_own, axis=0, keepdims=True) + jnp.sum(p_near, axis=0, keepdims=True)
             + jnp.sum(p_far, axis=0, keepdims=True))
    v_at = lambda kj: vwint_ref[:, pl.ds(pl.multiple_of(kj * kc_w, kc_w), kc_w)]
    acc_win = (_dot(v_at(i), p_own.astype(BF16)) + _dot(v_at(near_c), p_near.astype(BF16))
               + _dot(v_at(far_c), p_far.astype(BF16)))

    o_sel = acc_sel / l_sel
    o_win = acc_win / l_win
    gates = jnp.transpose(gate_ref[...])
    for r in range(rep):
        lanes = slice(r * tq, (r + 1) * tq)
        o = (gates[3 * r:3 * r + 1, :] * o_cmp[:, lanes] + gates[3 * r + 1:3 * r + 2, :] * o_sel[:, lanes]
             + gates[3 * r + 2:3 * r + 3, :] * o_win[:, lanes])
        o_ref[:, r * d:(r + 1) * d] = jnp.transpose(o).astype(o_ref.dtype)


def nsa_attention(q_raw, q_rot, cmp_kv, kr, vt, gates_a, overlap_t, batch):
    tq = NSA_TQ
    nq = SEQ // tq
    g_n = NSA_KV_HEADS
    row = lambda b, g, i: (b * nq + i, g)
    ks_c0 = 2 * MOBA_HEADS
    vt_r0 = MOBA_HEADS
    return pl.pallas_call(
        _nsa_kernel,
        grid=(batch, g_n, nq),
        in_specs=[pl.BlockSpec((tq, NSA_REP * HEAD_DIM), row),
                  pl.BlockSpec((tq, NSA_REP * HEAD_DIM), row),
                  pl.BlockSpec((None, None, NSA_M_PAD, HEAD_DIM), lambda b, g, i: (b, g, 0, 0)),
                  pl.BlockSpec((None, None, HEAD_DIM, NSA_M_PAD), lambda b, g, i: (b, g_n + g, 0, 0)),
                  pl.BlockSpec((SEQ, HEAD_DIM), lambda b, g, i: (b, ks_c0 + g)),
                  pl.BlockSpec((HEAD_DIM, SEQ), lambda b, g, i: (vt_r0 + g, b)),
                  pl.BlockSpec((SEQ, HEAD_DIM), lambda b, g, i: (b, ks_c0 + g_n + g)),
                  pl.BlockSpec((HEAD_DIM, SEQ), lambda b, g, i: (vt_r0 + g_n + g, b)),
                  pl.BlockSpec((tq, LANES), row),
                  pl.BlockSpec((NSA_NB, NSA_M_PAD), lambda b, g, i: (0, 0))],
        out_specs=pl.BlockSpec((tq, NSA_REP * HEAD_DIM), row),
        out_shape=jax.ShapeDtypeStruct((batch * SEQ, A_Q), BF16),
        scratch_shapes=[pltpu.VMEM((SEQ // NSA_KC, NSA_BLK_PER_CHUNK, NSA_REP * tq), F32)],
        compiler_params=_params("parallel", "parallel", "parallel"),
        name="nsa_attention",
    )(q_raw, q_rot, cmp_kv, cmp_kv, kr, vt, kr, vt, gates_a, overlap_t)


def _dilated_kernel(q0, k0, v0, q1, k1, v1, q2, k2, v2, o_ref,
                    o0_scr, o1_scr, o2_scr, l0_scr, l1_scr, l2_scr):
    tile = DIL_TILE
    groups = ((q0, k0, v0, o0_scr, l0_scr), (q1, k1, v1, o1_scr, l1_scr), (q2, k2, v2, o2_scr, l2_scr))
    for (window, dil), (q_ref, k_ref, v_ref, og_scr, lg_scr) in zip(DIL_GROUPS, groups):
        assert window // dil == tile
        per_class = SEQ // dil
        tiles_per_class = per_class // tile
        nk = tile if tiles_per_class == 1 else 2 * tile
        a_minus_a = _iota((tile, nk), 0) - _iota((tile, nk), 1)

        def step(u, carry, q_ref=q_ref, k_ref=k_ref, v_ref=v_ref, og_scr=og_scr, lg_scr=lg_scr,
                 dil=dil, tiles_per_class=tiles_per_class, nk=nk, a_minus_a=a_minus_a):
            ts = [u * DIL_UNROLL + a for a in range(DIL_UNROLL)]
            cls = [t // tiles_per_class for t in ts]
            n0 = [(t % tiles_per_class) * tile for t in ts]
            kbase = [jnp.maximum(n - (nk - tile), 0) for n in n0]
            q_rows = [pl.ds(c + dil * n, tile, stride=dil) for c, n in zip(cls, n0)]
            k_rows = [pl.ds(c + dil * kb, nk, stride=dil) for c, kb in zip(cls, kbase)]
            qs = [q_ref[r, :].astype(BF16) for r in q_rows]
            ks = [k_ref[r, :].astype(BF16) for r in k_rows]
            vs = [v_ref[r, :].astype(BF16) for r in k_rows]
            ss = [_dot_nt(q, k) for q, k in zip(qs, ks)]
            masks = []
            for n, kb in zip(n0, kbase):
                dist = a_minus_a + (n - kb)
                masks.append((dist >= 0) & (dist <= tile))
            ss = [jnp.where(mk, s, NEG_INF) for mk, s in zip(masks, ss)]
            ms = [jnp.max(s, axis=-1, keepdims=True) for s in ss]
            es = [jnp.where(mk, jnp.exp2(s - m), 0.0) for mk, s, m in zip(masks, ss, ms)]
            dens = [jnp.maximum(jnp.sum(e, axis=-1, keepdims=True), 1e-30) for e in es]
            os_ = [_dot((e / den).astype(BF16), v) for e, den, v in zip(es, dens, vs)]
            for r, o, m, den in zip(q_rows, os_, ms, dens):
                og_scr[r, :] = o
                lg_scr[r, :] = jnp.broadcast_to(m + jnp.log2(den), (tile, HEAD_DIM))
            return carry

        lax.fori_loop(0, SEQ // tile // DIL_UNROLL, step, 0)

    rows = 256

    def merge_body(c, carry):
        sl = pl.ds(pl.multiple_of(c * rows, rows), rows)
        la, lb, lc = l0_scr[sl, :], l1_scr[sl, :], l2_scr[sl, :]
        mx = jnp.maximum(jnp.maximum(la, lb), lc)
        ea, eb, ec = jnp.exp2(la - mx), jnp.exp2(lb - mx), jnp.exp2(lc - mx)
        tot = ea + eb + ec
        out = (ea / tot) * o0_scr[sl, :] + (eb / tot) * o1_scr[sl, :] + (ec / tot) * o2_scr[sl, :]
        o_ref[sl, :] = out.astype(o_ref.dtype)
        return carry

    lax.fori_loop(0, SEQ // rows, merge_body, 0)


def dilated_attention(rb, pf, batch):
    hp = DIL_HEADS_PER_GROUP
    in_specs, args = [], []
    for g in range(len(DIL_GROUPS)):
        for arr, c0 in ((rb, 0), (rb, DIL_HEADS), (pf, 4)):
            in_specs.append(pl.BlockSpec((SEQ, HEAD_DIM), lambda b, j, c0=c0, g=g: (b, c0 + g * hp + j)))
            args.append(arr)
    return pl.pallas_call(
        _dilated_kernel,
        grid=(batch, hp),
        in_specs=in_specs,
        out_specs=pl.BlockSpec((SEQ, HEAD_DIM), lambda b, j: (b, j)),
        out_shape=jax.ShapeDtypeStruct((batch * SEQ, hp * HEAD_DIM), BF16),
        scratch_shapes=[pltpu.VMEM((SEQ, HEAD_DIM), F32)] * 6,
        compiler_params=_params("parallel", "parallel"),
        name="dilated_attention",
    )(*args)


def _moba_kernel(q_ref, k_ref, vt_ref, o_ref, kmean_scr, bias_scr):
    i = pl.program_id(1)
    tq, nb, d = MOBA_BLOCK, MOBA_NB, HEAD_DIM
    heads = range(MOBA_HEADS)
    col = lambda h: slice(h * d, (h + 1) * d)

    @pl.when(i == 0)
    def _():
        avg = jnp.where(_iota((nb, SEQ), 1) // MOBA_BLOCK == _iota((nb, SEQ), 0), 1.0 / MOBA_BLOCK, 0.0)
        kmean_scr[...] = _dot(avg.astype(BF16), k_ref[...]).astype(kmean_scr.dtype)

    blk = _iota((nb, tq), 0)
    own = pl.multiple_of(i * tq, tq)
    causal = _iota((tq, tq), 0) <= _iota((tq, tq), 1)
    qs = [q_ref[:, col(h)] for h in heads]
    gates = [jnp.where(blk < i, _dot_nt(kmean_scr[:, col(h)], qs[h]), NEG_INF) for h in heads]
    ss = [_dot_nt(k_ref[pl.ds(own, tq), col(h)], qs[h]) for h in heads]
    for h in heads:
        gate = gates[h]
        rank = jnp.zeros((nb, tq), F32)
        for j in range(nb):
            gj = gate[j:j + 1, :]
            ahead = (gj > gate) | ((gj == gate) & (blk > j))
            rank = rank + jnp.where(ahead, 1.0, 0.0)
        bias_scr[h] = jnp.where((rank < float(MOBA_TOPK)) & (gate > NEG_INF * 0.5), 0.0, NEG_INF)
    ss = [jnp.where(causal, s, NEG_INF) for s in ss]
    ms = [jnp.max(s, axis=0, keepdims=True) for s in ss]
    ps = [jnp.exp2(s - m) for s, m in zip(ss, ms)]
    ls = [jnp.sum(p, axis=0, keepdims=True) for p in ps]
    accs = [_dot(vt_ref[col(h), pl.ds(own, tq)], ps[h].astype(BF16)) for h in heads]

    def body(kj, carry):
        off = pl.multiple_of(kj * tq, tq)
        bs = [bias_scr[h, pl.ds(kj, 1), :] for h in heads]
        ss = [_dot_nt(k_ref[pl.ds(off, tq), col(h)], qs[h]) for h in heads]
        m_new = [jnp.maximum(carry[h][0], jnp.max(ss[h], axis=0, keepdims=True) + bs[h]) for h in heads]
        alpha = [jnp.exp2(carry[h][0] - m_new[h]) for h in heads]
        ps = [jnp.exp2(ss[h] - (m_new[h] - bs[h])) for h in heads]
        ls = [alpha[h] * carry[h][1] + jnp.sum(ps[h], axis=0, keepdims=True) for h in heads]
        pvs = [_dot(vt_ref[col(h), pl.ds(off, tq)], ps[h].astype(BF16)) for h in heads]
        return tuple((m_new[h], ls[h], alpha[h] * carry[h][2] + pvs[h]) for h in heads)

    fin = lax.fori_loop(0, i, body, tuple((ms[h], ls[h], accs[h]) for h in heads))
    for h in heads:
        _, l, acc = fin[h]
        o_ref[:, col(h)] = jnp.transpose(acc / l).astype(o_ref.dtype)


def moba_attention(kr, vt, batch):
    tq = MOBA_BLOCK
    nq = SEQ // tq
    return pl.pallas_call(
        _moba_kernel,
        grid=(batch, nq),
        in_specs=[pl.BlockSpec((tq, C_QKV), lambda b, i: (b * nq + i, 0)),
                  pl.BlockSpec((SEQ, C_QKV), lambda b, i: (b, 1)),
                  pl.BlockSpec((C_QKV, SEQ), lambda b, i: (0, b))],
        out_specs=pl.BlockSpec((tq, C_QKV), lambda b, i: (b * nq + i, 0)),
        out_shape=jax.ShapeDtypeStruct((batch * SEQ, C_QKV), BF16),
        scratch_shapes=[pltpu.VMEM((MOBA_NB, C_QKV), BF16), pltpu.VMEM((MOBA_HEADS, MOBA_NB, tq), F32)],
        compiler_params=_params("parallel", "arbitrary"),
        name="moba_attention",
    )(kr, kr, vt)


def _merge_kernel(oa_ref, ob_ref, oc_ref, ga_ref, gb_ref, gc_ref, wa_ref, wb_ref, wc_ref, out_ref):
    y = ga_ref[...] * _dot(oa_ref[...], wa_ref[...])
    y = y + gb_ref[...] * _dot(ob_ref[...], wb_ref[...])
    y = y + gc_ref[...] * _dot(oc_ref[...], wc_ref[...])
    out_ref[...] = y.astype(out_ref.dtype)


def gated_merge(o_a, o_b, o_c, gates_m, w_a, w_b, w_c, tm=1024, tn=512):
    m = o_a.shape[0]
    nb = D_MODEL // tn
    act = lambda width: pl.BlockSpec((tm, width), lambda j, i: (i, 0))
    gate = lambda g: pl.BlockSpec((tm, tn), lambda j, i, g=g: (i, g * nb + j))
    wgt = lambda width: pl.BlockSpec((width, tn), lambda j, i: (0, j))
    return pl.pallas_call(
        _merge_kernel,
        grid=(nb, m // tm),
        in_specs=[act(o_a.shape[1]), act(o_b.shape[1]), act(o_c.shape[1]),
                  gate(0), gate(1), gate(2),
                  wgt(w_a.shape[0]), wgt(w_b.shape[0]), wgt(w_c.shape[0])],
        out_specs=pl.BlockSpec((tm, tn), lambda j, i: (i, j)),
        out_shape=jax.ShapeDtypeStruct((m, D_MODEL), BF16),
        compiler_params=_params("parallel", "parallel"),
        name="gated_merge",
    )(o_a, o_b, o_c, gates_m, gates_m, gates_m, w_a, w_b, w_c)


def _out_proj_kernel(y_ref, w_ref, x_ref, g_ref, xo_ref, h_ref):
    x_new = x_ref[...] + _dot(y_ref[...], w_ref[...])
    xo_ref[...] = x_new
    h_ref[...] = _rmsnorm_rows(x_new, g_ref[...]).astype(h_ref.dtype)


def out_proj_residual_norm(y, w_o, x2, g, tm=512):
    m, d = x2.shape
    row = pl.BlockSpec((tm, d), lambda i: (i, 0))
    return pl.pallas_call(
        _out_proj_kernel,
        grid=(m // tm,),
        in_specs=[row, pl.BlockSpec((d, d), lambda i: (0, 0)), row, pl.BlockSpec((1, d), lambda i: (0, 0))],
        out_specs=[row, row],
        out_shape=[jax.ShapeDtypeStruct((m, d), F32), jax.ShapeDtypeStruct((m, d), BF16)],
        compiler_params=_params("parallel"),
        name="out_proj",
    )(y, w_o, x2, g.reshape(1, d))


def _mlp_kernel(h_ref, w1_ref, w2_ref, x_ref, g_ref, xo_ref, hn_ref, acc_scr):
    f = pl.program_id(1)

    @pl.when(f == 0)
    def _():
        acc_scr[...] = jnp.zeros_like(acc_scr)

    u = jnp.square(jnp.maximum(_dot(h_ref[...], w1_ref[...]), 0.0))
    acc_scr[...] += _dot(u.astype(BF16), w2_ref[...])

    @pl.when(f == pl.num_programs(1) - 1)
    def _():
        x_new = x_ref[...] + acc_scr[...]
        xo_ref[...] = x_new
        hn_ref[...] = _rmsnorm_rows(x_new, g_ref[...]).astype(hn_ref.dtype)


def mlp_residual_norm(h2, w1, w2, x2, g_next, next_dtype, tm=512, tf=512):
    m, d = x2.shape
    row = pl.BlockSpec((tm, d), lambda i, f: (i, 0))
    return pl.pallas_call(
        _mlp_kernel,
        grid=(m // tm, D_FF // tf),
        in_specs=[row, pl.BlockSpec((d, tf), lambda i, f: (0, f)), pl.BlockSpec((tf, d), lambda i, f: (f, 0)),
                  row, pl.BlockSpec((1, d), lambda i, f: (0, 0))],
        out_specs=[row, row],
        out_shape=[jax.ShapeDtypeStruct((m, d), F32), jax.ShapeDtypeStruct((m, d), next_dtype)],
        scratch_shapes=[pltpu.VMEM((tm, d), F32)],
        compiler_params=_params("parallel", "arbitrary"),
        name="mlp",
    )(h2, w1, w2, x2, g_next.reshape(1, d))


def _rope_tables():
    inv = ROPE_THETA ** (-jnp.arange(0, ROPE_DIM, 2, dtype=F32) / ROPE_DIM)
    ang = jnp.arange(SEQ, dtype=F32)[:, None] * inv[None, :]
    cos, sin = jnp.cos(ang), jnp.sin(ang)
    zeros = jnp.zeros((SEQ, HEAD_DIM - ROPE_DIM), F32)
    zero_h = jnp.zeros((SEQ, ROPE_HALF), F32)
    c = jnp.concatenate([cos, cos, jnp.ones_like(zeros)], axis=1)
    s_up = jnp.concatenate([-sin, zero_h, zeros], axis=1)
    s_dn = jnp.concatenate([zero_h, sin, zeros], axis=1)
    return c, s_up, s_dn


def _overlap_table_t():
    cs = np.arange(NSA_M_PAD)[None, :] * NSA_CMP_STRIDE
    bs = np.arange(NSA_NB)[:, None] * NSA_SEL_BLOCK
    ov = np.clip(np.minimum(cs + NSA_CMP_LEN, bs + NSA_SEL_BLOCK) - np.maximum(cs, bs), 0, None) / NSA_CMP_LEN
    ov[:, NSA_M_PAD - 1] = 0.0
    return jnp.asarray(ov, dtype=BF16)


def _split_w_in(w):
    pts = np.cumsum((0,) + IN_SPLIT_SIZES)
    names = ("a_q", "a_kc", "a_vc", "a_ks", "a_vs", "a_kw", "a_vw", "a_g",
             "b_q", "b_k", "b_v", "c_q", "c_k", "c_v", "m_a", "m_b", "m_c")
    return {n: w[:, pts[i]:pts[i + 1]] for i, n in enumerate(names)}


def _scales(*widths_and_values):
    return jnp.concatenate([jnp.full((1, w), v, F32) for w, v in widths_and_values], axis=1)


def _layer(x2, h, batch, tabs, overlap_t, w_in, pe_k, w1_k, w2_k, pe_v, w1_v, w2_v,
           w_br_a, w_br_b, w_br_c, w_o, mlp_g, w_mlp_in, w_mlp_out, g_next, next_dtype):
    w = _split_w_in(w_in)
    cat = lambda names: jnp.concatenate([w[n] for n in names], axis=1).astype(BF16)
    per_group = 3 * NSA_REP
    a_g = jnp.concatenate(
        [jnp.pad(w["a_g"][:, g * per_group:(g + 1) * per_group], ((0, 0), (0, LANES - per_group)))
         for g in range(NSA_KV_HEADS)], axis=1).astype(BF16)

    q_raw, q_rot = project(h, w["a_q"].astype(BF16), "both", BF16, tabs, _scales((A_Q, Q_SCALE)))
    kr = project(h, cat(("c_q", "c_k", "a_ks", "a_kw")), "rope", BF16, tabs,
                 _scales((C_QKV, Q_SCALE), (C_QKV + 2 * A_KV, 1.0)))
    rb = project(h, cat(("b_q", "b_k")), "rope", F32, tabs, _scales((B_QKV, Q_SCALE), (B_QKV, 1.0)))
    pf = project(h, cat(("a_kc", "a_vc", "b_v")), "plain", F32)
    vt = project_transposed(h, jnp.transpose(cat(("c_v", "a_vs", "a_vw"))), BF16)
    gates_m = project(h, cat(("m_a", "m_b", "m_c")), "sigmoid", F32)
    gates_a = project(h, a_g, "sigmoid", F32)

    flat = NSA_CMP_LEN * HEAD_DIM
    cmp_kv = nsa_compress(pf, pe_k.reshape(1, flat), w1_k.reshape(flat, NSA_CMP_HIDDEN).astype(BF16),
                          w2_k.astype(BF16), pe_v.reshape(1, flat),
                          w1_v.reshape(flat, NSA_CMP_HIDDEN).astype(BF16), w2_v.astype(BF16), batch)
    o_a = nsa_attention(q_raw, q_rot, cmp_kv, kr, vt, gates_a, overlap_t, batch)
    o_b = dilated_attention(rb, pf, batch)
    o_c = moba_attention(kr, vt, batch)

    merged = gated_merge(o_a, o_b, o_c, gates_m, w_br_a.astype(BF16), w_br_b.astype(BF16), w_br_c.astype(BF16))
    x2, h2 = out_proj_residual_norm(merged, w_o.astype(BF16), x2, mlp_g)
    return mlp_residual_norm(h2, w_mlp_in.astype(BF16), w_mlp_out.astype(BF16), x2, g_next, next_dtype)


def kernel(x, attn_norm_g, w_in, cmp_pe_k, cmp_w1_k, cmp_w2_k, cmp_pe_v, cmp_w1_v, cmp_w2_v,
           w_br_a, w_br_b, w_br_c, w_o, mlp_norm_g, w_mlp_in, w_mlp_out, final_norm_g):
    batch, seq, d = x.shape
    assert seq == SEQ and d == D_MODEL
    depth = w_in.shape[0]
    tabs = _rope_tables()
    overlap_t = _overlap_table_t()
    x2 = x.reshape(batch * seq, d)
    h = rmsnorm(x2, attn_norm_g[0], BF16)
    for l in range(depth):
        last = l == depth - 1
        g_next = final_norm_g if last else attn_norm_g[l + 1]
        x2, h = _layer(x2, h, batch, tabs, overlap_t, w_in[l],
                       cmp_pe_k[l], cmp_w1_k[l], cmp_w2_k[l], cmp_pe_v[l], cmp_w1_v[l], cmp_w2_v[l],
                       w_br_a[l], w_br_b[l], w_br_c[l], w_o[l], mlp_norm_g[l], w_mlp_in[l], w_mlp_out[l],
                       g_next, F32 if last else BF16)
    return h.reshape(batch, seq, d)
```

```python
import functools
import math

import numpy as np
import jax
import jax.numpy as jnp
from jax import lax
from jax.experimental import pallas as pl
from jax.experimental.pallas import tpu as pltpu

D_MODEL = 2048
SEQ = 2048
HEAD_DIM = 128
ROPE_THETA = 500000.0
ROPE_DIM = HEAD_DIM // 4
ROPE_HALF = ROPE_DIM // 2
NORM_EPS = 1e-6
NEG_INF = -1e30
Q_SCALE = HEAD_DIM ** -0.5 * math.log2(math.e)

NSA_HEADS = 8
NSA_KV_HEADS = 2
NSA_REP = NSA_HEADS // NSA_KV_HEADS
NSA_CMP_LEN = 32
NSA_CMP_STRIDE = 16
NSA_CMP_HIDDEN = 256
NSA_SEL_BLOCK = 64
NSA_SEL_TOPN = 16
NSA_WINDOW = 512
NSA_FORCE_BONUS = 1e4
NSA_NB = SEQ // NSA_SEL_BLOCK
NSA_M_PAD = SEQ // NSA_CMP_STRIDE

DIL_GROUPS = ((128, 1), (512, 4), (2048, 16))
DIL_HEADS_PER_GROUP = 4
DIL_HEADS = DIL_HEADS_PER_GROUP * len(DIL_GROUPS)
DIL_TILE = 128
DIL_UNROLL = 4

MOBA_HEADS = 8
MOBA_BLOCK = 256
MOBA_TOPK = 3
MOBA_NB = SEQ // MOBA_BLOCK

D_FF = 4 * D_MODEL
A_Q = NSA_HEADS * HEAD_DIM
A_KV = NSA_KV_HEADS * HEAD_DIM
A_G = 3 * NSA_HEADS
B_QKV = DIL_HEADS * HEAD_DIM
C_QKV = MOBA_HEADS * HEAD_DIM
IN_SPLIT_SIZES = (A_Q, A_KV, A_KV, A_KV, A_KV, A_KV, A_KV, A_G,
                  B_QKV, B_QKV, B_QKV, C_QKV, C_QKV, C_QKV,
                  D_MODEL, D_MODEL, D_MODEL)

LANES = 128
VMEM_LIMIT = 56 * 1024 * 1024

BF16 = jnp.bfloat16
F32 = jnp.float32


def _params(*sem):
    return pltpu.CompilerParams(dimension_semantics=sem, vmem_limit_bytes=VMEM_LIMIT)


def _dot(a, b):
    return jnp.dot(a, b, preferred_element_type=F32)


def _dot_nt(a, b):
    return lax.dot_general(a, b, (((1,), (1,)), ((), ())), preferred_element_type=F32)


def _iota(shape, axis):
    return lax.broadcasted_iota(jnp.int32, shape, axis)


def _rmsnorm_rows(x, g):
    y = x * lax.rsqrt(jnp.mean(x * x, axis=-1, keepdims=True) + NORM_EPS)
    return y * g


def _rmsnorm_kernel(x_ref, g_ref, h_ref):
    h_ref[...] = _rmsnorm_rows(x_ref[...], g_ref[...]).astype(h_ref.dtype)


def rmsnorm(x2, g, out_dtype, tm=512):
    m, d = x2.shape
    return pl.pallas_call(
        _rmsnorm_kernel,
        grid=(m // tm,),
        in_specs=[pl.BlockSpec((tm, d), lambda i: (i, 0)), pl.BlockSpec((1, d), lambda i: (0, 0))],
        out_specs=pl.BlockSpec((tm, d), lambda i: (i, 0)),
        out_shape=jax.ShapeDtypeStruct((m, d), out_dtype),
        compiler_params=_params("parallel"),
        name="rmsnorm",
    )(x2, g.reshape(1, d))


def _rope_lanes(acc, c, s_up, s_dn):
    tn = acc.shape[1]
    reps = tn // HEAD_DIM
    if reps > 1:
        c = jnp.concatenate([c] * reps, axis=1)
        s_up = jnp.concatenate([s_up] * reps, axis=1)
        s_dn = jnp.concatenate([s_dn] * reps, axis=1)
    up = pltpu.roll(acc, tn - ROPE_HALF, axis=1)
    dn = pltpu.roll(acc, ROPE_HALF, axis=1)
    return acc * c + up * s_up + dn * s_dn


def _proj_kernel(*refs, mode):
    if mode in ("rope", "both"):
        h_ref, w_ref, c_ref, su_ref, sd_ref, cs_ref = refs[:6]
        outs = refs[6:]
    else:
        h_ref, w_ref = refs[:2]
        outs = refs[2:]
    acc = _dot(h_ref[...], w_ref[...])
    if mode == "plain":
        outs[0][...] = acc.astype(outs[0].dtype)
    elif mode == "sigmoid":
        outs[0][...] = jax.nn.sigmoid(acc).astype(outs[0].dtype)
    else:
        col_scale = cs_ref[...]
        roped = _rope_lanes(acc, c_ref[...], su_ref[...], sd_ref[...]) * col_scale
        if mode == "both":
            outs[0][...] = (acc * col_scale).astype(outs[0].dtype)
            outs[1][...] = roped.astype(outs[1].dtype)
        else:
            outs[0][...] = roped.astype(outs[0].dtype)


def project(h, w_pad, col0, n, mode, out_dtype, rope_tabs=None, col_scale=None, tm=2048, tn=512):
    m, k = h.shape
    tn = min(tn, n)
    assert m % tm == 0 and n % tn == 0 and col0 % tn == 0 and tm % SEQ == 0
    j0 = col0 // tn
    in_specs = [pl.BlockSpec((tm, k), lambda i, j: (i, 0)),
                pl.BlockSpec((k, tn), lambda i, j: (0, j0 + j))]
    args = [h, w_pad]
    if mode in ("rope", "both"):
        reps = tm // SEQ
        for t in rope_tabs:
            in_specs.append(pl.BlockSpec((tm, HEAD_DIM), lambda i, j: (0, 0)))
            args.append(t if reps == 1 else jnp.concatenate([t] * reps, axis=0))
        in_specs.append(pl.BlockSpec((1, tn), lambda i, j: (0, j)))
        args.append(col_scale)
    n_out = 2 if mode == "both" else 1
    out_spec = pl.BlockSpec((tm, tn), lambda i, j: (i, j))
    out_shape = jax.ShapeDtypeStruct((m, n), out_dtype)
    res = pl.pallas_call(
        functools.partial(_proj_kernel, mode=mode),
        grid=(m // tm, n // tn),
        in_specs=in_specs,
        out_specs=[out_spec] * n_out,
        out_shape=[out_shape] * n_out,
        compiler_params=_params("parallel", "parallel"),
        name="proj_" + mode,
    )(*args)
    return res if n_out == 2 else res[0]


def _proj_t_kernel(wt_ref, h_ref, out_ref):
    out_ref[...] = _dot_nt(wt_ref[...], h_ref[...]).astype(out_ref.dtype)


def project_transposed(h, wt, out_dtype, tm=1024, tn=512):
    m, k = h.shape
    n = wt.shape[0]
    return pl.pallas_call(
        _proj_t_kernel,
        grid=(n // tn, m // tm),
        in_specs=[pl.BlockSpec((tn, k), lambda j, i: (j, 0)), pl.BlockSpec((tm, k), lambda j, i: (i, 0))],
        out_specs=pl.BlockSpec((tn, tm), lambda j, i: (j, i)),
        out_shape=jax.ShapeDtypeStruct((n, m), out_dtype),
        compiler_params=_params("parallel", "parallel"),
        name="proj_transposed",
    )(wt, h)


def _compress_kernel(k0_ref, k1_ref, v0_ref, v1_ref, pek_ref, w1k_ref, w2k_ref, pev_ref, w1v_ref, w2v_ref,
                     out_ref):
    half = NSA_CMP_STRIDE * HEAD_DIM
    for idx, src_ref in enumerate((k0_ref, k1_ref, v0_ref, v1_ref)):
        is_k = idx < NSA_KV_HEADS
        pe_ref, w1_ref, w2_ref = (pek_ref, w1k_ref, w2k_ref) if is_k else (pev_ref, w1v_ref, w2v_ref)
        x = jnp.concatenate(
            [src_ref[pl.ds(l, NSA_M_PAD, stride=NSA_CMP_STRIDE), :]
             for l in range(NSA_CMP_STRIDE)], axis=1)
        pe = pe_ref[...]
        first = _dot((x + pe[:, :half]).astype(BF16), w1_ref[:half, :])
        second = _dot((x + pe[:, half:]).astype(BF16), w1_ref[half:, :])
        hid = jax.nn.gelu(first + pltpu.roll(second, NSA_M_PAD - 1, axis=0))
        out = _dot(hid.astype(BF16), w2_ref[...])
        out_ref[idx] = (out if is_k else jnp.transpose(out)).astype(out_ref.dtype)


def nsa_compress(pf, pe_k, w1_k, w2_k, pe_v, w1_v, w2_v, batch):
    flat = NSA_CMP_LEN * HEAD_DIM
    const = lambda shape: pl.BlockSpec(shape, lambda b: (0,) * len(shape))
    return pl.pallas_call(
        _compress_kernel,
        grid=(batch,),
        in_specs=[pl.BlockSpec((SEQ, HEAD_DIM), lambda b, c=c: (b, c)) for c in range(4)] + [
                  const((1, flat)), const((flat, NSA_CMP_HIDDEN)), const((NSA_CMP_HIDDEN, HEAD_DIM)),
                  const((1, flat)), const((flat, NSA_CMP_HIDDEN)), const((NSA_CMP_HIDDEN, HEAD_DIM))],
        out_specs=pl.BlockSpec((None, 4, NSA_M_PAD, HEAD_DIM), lambda b: (b, 0, 0, 0)),
        out_shape=jax.ShapeDtypeStruct((batch, 4, NSA_M_PAD, HEAD_DIM), BF16),
        compiler_params=_params("parallel"),
        name="nsa_compress",
    )(pf, pf, pf, pf, pe_k, w1_k, w2_k, pe_v, w1_v, w2_v)


NSA_TQ = 256
NSA_KC = 256
NSA_BLK_PER_CHUNK = NSA_KC // NSA_SEL_BLOCK


def _nsa_kernel(qraw_ref, qrot_ref, kc_ref, vct_ref, ksel_ref, vselt_ref, kwin_ref, vwint_ref,
                gate_ref, ovt_ref, o_ref, bias_scr):
    i = pl.program_id(2)
    tq, kc_w, rep, d = NSA_TQ, NSA_KC, NSA_REP, HEAD_DIM
    width = rep * tq
    bpc = NSA_BLK_PER_CHUNK
    t0 = i * tq
    stack = lambda ref: jnp.concatenate([ref[:, r * d:(r + 1) * d] for r in range(rep)], axis=0)
    q_raw = stack(qraw_ref)
    q_rot = stack(qrot_ref)

    def q_pos(rows):
        return t0 + (_iota((rows, width), 1) & (tq - 1))

    s = _dot_nt(kc_ref[...], q_raw)
    vis = (_iota((NSA_M_PAD, width), 0) * NSA_CMP_STRIDE + (NSA_CMP_LEN - 1)) <= q_pos(NSA_M_PAD)
    s = jnp.where(vis, s, NEG_INF)
    e = jnp.where(vis, jnp.exp2(s - jnp.max(s, axis=0, keepdims=True)), 0.0)
    den = jnp.maximum(jnp.sum(e, axis=0, keepdims=True), 1e-30)
    p = (e / den).astype(BF16)
    o_cmp = _dot(vct_ref[...], p)
    imp_heads = _dot(ovt_ref[...], p)
    imp = imp_heads[:, :tq]
    for r in range(1, rep):
        imp = imp + imp_heads[:, r * tq:(r + 1) * tq]

    blk = _iota((NSA_NB, tq), 0)
    q_blk = (t0 + _iota((NSA_NB, tq), 1)) // NSA_SEL_BLOCK
    forced = (blk == 0) | (blk == q_blk) | (blk == q_blk - 1)
    score = jnp.where(blk <= q_blk, imp + jnp.where(forced, NSA_FORCE_BONUS, 0.0), NEG_INF)
    rank = jnp.zeros((NSA_NB, tq), F32)
    for j in range(NSA_NB):
        sj = score[j:j + 1, :]
        ahead = (sj > score) | ((sj == score) & (blk > j))
        rank = rank + jnp.where(ahead, 1.0, 0.0)
    bias = jnp.where((rank < float(NSA_SEL_TOPN)) & (score > NEG_INF * 0.5), 0.0, NEG_INF)
    bias = jnp.concatenate([bias] * rep, axis=1)
    for c in range(SEQ // kc_w):
        bias_scr[c] = bias[c * bpc:(c + 1) * bpc, :]

    def chunk_scores(kref, kj, mask):
        off = pl.multiple_of(kj * kc_w, kc_w)
        s = _dot_nt(kref[pl.ds(off, kc_w), :], q_rot)
        return s if mask is None else jnp.where(mask, s, NEG_INF)

    def sel_step(kj, s, carry):
        m, l, acc = carry
        off = pl.multiple_of(kj * kc_w, kc_w)
        b = bias_scr[kj]
        s3 = s.reshape(bpc, NSA_SEL_BLOCK, width)
        m_new = jnp.maximum(m, jnp.max(jnp.max(s3, axis=1) + b, axis=0, keepdims=True))
        alpha = jnp.exp2(m - m_new)
        p = jnp.exp2(s3 - (m_new - b)[:, None, :]).reshape(kc_w, width)
        l = alpha * l + jnp.sum(p, axis=0, keepdims=True)
        acc = alpha * acc + _dot(vselt_ref[:, pl.ds(off, kc_w)], p.astype(BF16))
        return m_new, l, acc

    init = (jnp.full((1, width), NEG_INF, F32), jnp.zeros((1, width), F32), jnp.zeros((d, width), F32))
    carry = lax.fori_loop(0, i, lambda kj, c: sel_step(kj, chunk_scores(ksel_ref, kj, None), c), init)

    key_row = _iota((kc_w, width), 0)
    q_col = _iota((kc_w, width), 1) & (tq - 1)
    causal = key_row <= q_col
    _, l_sel, acc_sel = sel_step(i, chunk_scores(ksel_ref, i, causal), carry)

    far = i - 2
    near = i - 1
    far_ok = jnp.where(far >= 0, 0.0, NEG_INF)
    near_ok = jnp.where(near >= 0, 0.0, NEG_INF)
    far_c = jnp.maximum(far, 0)
    near_c = jnp.maximum(near, 0)
    s_own = chunk_scores(kwin_ref, i, causal)
    s_near = chunk_scores(kwin_ref, near_c, None) + near_ok
    s_far = chunk_scores(kwin_ref, far_c, key_row > q_col) + far_ok
    m_w = jnp.maximum(jnp.maximum(jnp.max(s_own, axis=0, keepdims=True), jnp.max(s_near, axis=0, keepdims=True)),
                      jnp.max(s_far, axis=0, keepdims=True))
    p_own = jnp.exp2(s_own - m_w)
    p_near = jnp.exp2(s_near - m_w)
    p_far = jnp.exp2(s_far - m_w)
    l_win = (jnp.sum(p_own, axis=0, keepdims=True) + jnp.sum(p_near, axis=0, keepdims=True)
             + jnp.sum(p_far, axis=0, keepdims=True))
    v_at = lambda kj: vwint_ref[:, pl.ds(pl.multiple_of(kj * kc_w, kc_w), kc_w)]
    acc_win = (_dot(v_at(i), p_own.astype(BF16)) + _dot(v_at(near_c), p_near.astype(BF16))
               + _dot(v_at(far_c), p_far.astype(BF16)))

    o_sel = acc_sel / l_sel
    o_win = acc_win / l_win
    gates = jnp.transpose(gate_ref[...])
    for r in range(rep):
        lanes = slice(r * tq, (r + 1) * tq)
        o = (gates[3 * r:3 * r + 1, :] * o_cmp[:, lanes] + gates[3 * r + 1:3 * r + 2, :] * o_sel[:, lanes]
             + gates[3 * r + 2:3 * r + 3, :] * o_win[:, lanes])
        o_ref[:, r * d:(r + 1) * d] = jnp.transpose(o).astype(o_ref.dtype)


def nsa_attention(q_raw, q_rot, cmp_kv, k_sel, k_win, vt, gates_a, overlap_t, batch):
    tq = NSA_TQ
    nq = SEQ // tq
    g_n = NSA_KV_HEADS
    row = lambda b, g, i: (b * nq + i, g)
    vt_r0 = MOBA_HEADS
    return pl.pallas_call(
        _nsa_kernel,
        grid=(batch, g_n, nq),
        in_specs=[pl.BlockSpec((tq, NSA_REP * HEAD_DIM), row),
                  pl.BlockSpec((tq, NSA_REP * HEAD_DIM), row),
                  pl.BlockSpec((None, None, NSA_M_PAD, HEAD_DIM), lambda b, g, i: (b, g, 0, 0)),
                  pl.BlockSpec((None, None, HEAD_DIM, NSA_M_PAD), lambda b, g, i: (b, g_n + g, 0, 0)),
                  pl.BlockSpec((SEQ, HEAD_DIM), lambda b, g, i: (b, g)),
                  pl.BlockSpec((HEAD_DIM, SEQ), lambda b, g, i: (vt_r0 + g, b)),
                  pl.BlockSpec((SEQ, HEAD_DIM), lambda b, g, i: (b, g)),
                  pl.BlockSpec((HEAD_DIM, SEQ), lambda b, g, i: (vt_r0 + g_n + g, b)),
                  pl.BlockSpec((tq, LANES), row),
                  pl.BlockSpec((NSA_NB, NSA_M_PAD), lambda b, g, i: (0, 0))],
        out_specs=pl.BlockSpec((tq, NSA_REP * HEAD_DIM), row),
        out_shape=jax.ShapeDtypeStruct((batch * SEQ, A_Q), BF16),
        scratch_shapes=[pltpu.VMEM((SEQ // NSA_KC, NSA_BLK_PER_CHUNK, NSA_REP * tq), F32)],
        compiler_params=_params("parallel", "parallel", "parallel"),
        name="nsa_attention",
    )(q_raw, q_rot, cmp_kv, cmp_kv, k_sel, vt, k_win, vt, gates_a, overlap_t)


def _dilated_kernel(q0, k0, v0, q1, k1, v1, q2, k2, v2, o_ref,
                    o0_scr, o1_scr, o2_scr, l0_scr, l1_scr, l2_scr):
    tile = DIL_TILE
    groups = ((q0, k0, v0, o0_scr, l0_scr), (q1, k1, v1, o1_scr, l1_scr), (q2, k2, v2, o2_scr, l2_scr))
    for (window, dil), (q_ref, k_ref, v_ref, og_scr, lg_scr) in zip(DIL_GROUPS, groups):
        assert window // dil == tile
        per_class = SEQ // dil
        tiles_per_class = per_class // tile
        nk = tile if tiles_per_class == 1 else 2 * tile
        a_minus_a = _iota((tile, nk), 0) - _iota((tile, nk), 1)

        def step(u, carry, q_ref=q_ref, k_ref=k_ref, v_ref=v_ref, og_scr=og_scr, lg_scr=lg_scr,
                 dil=dil, tiles_per_class=tiles_per_class, nk=nk, a_minus_a=a_minus_a):
            ts = [u * DIL_UNROLL + a for a in range(DIL_UNROLL)]
            cls = [t // tiles_per_class for t in ts]
            n0 = [(t % tiles_per_class) * tile for t in ts]
            kbase = [jnp.maximum(n - (nk - tile), 0) for n in n0]
            q_rows = [pl.ds(c + dil * n, tile, stride=dil) for c, n in zip(cls, n0)]
            k_rows = [pl.ds(c + dil * kb, nk, stride=dil) for c, kb in zip(cls, kbase)]
            qs = [q_ref[r, :].astype(BF16) for r in q_rows]
            ks = [k_ref[r, :].astype(BF16) for r in k_rows]
            vs = [v_ref[r, :].astype(BF16) for r in k_rows]
            ss = [_dot_nt(q, k) for q, k in zip(qs, ks)]
            masks = []
            for n, kb in zip(n0, kbase):
                dist = a_minus_a + (n - kb)
                masks.append((dist >= 0) & (dist <= tile))
            ss = [jnp.where(mk, s, NEG_INF) for mk, s in zip(masks, ss)]
            ms = [jnp.max(s, axis=-1, keepdims=True) for s in ss]
            es = [jnp.where(mk, jnp.exp2(s - m), 0.0) for mk, s, m in zip(masks, ss, ms)]
            dens = [jnp.maximum(jnp.sum(e, axis=-1, keepdims=True), 1e-30) for e in es]
            os_ = [_dot((e / den).astype(BF16), v) for e, den, v in zip(es, dens, vs)]
            for r, o, m, den in zip(q_rows, os_, ms, dens):
                og_scr[r, :] = o
                lg_scr[r, :] = jnp.broadcast_to(m + jnp.log2(den), (tile, HEAD_DIM))
            return carry

        lax.fori_loop(0, SEQ // tile // DIL_UNROLL, step, 0)

    rows = 256

    def merge_body(c, carry):
        sl = pl.ds(pl.multiple_of(c * rows, rows), rows)
        la, lb, lc = l0_scr[sl, :], l1_scr[sl, :], l2_scr[sl, :]
        mx = jnp.maximum(jnp.maximum(la, lb), lc)
        ea, eb, ec = jnp.exp2(la - mx), jnp.exp2(lb - mx), jnp.exp2(lc - mx)
        tot = ea + eb + ec
        out = (ea / tot) * o0_scr[sl, :] + (eb / tot) * o1_scr[sl, :] + (ec / tot) * o2_scr[sl, :]
        o_ref[sl, :] = out.astype(o_ref.dtype)
        return carry

    lax.fori_loop(0, SEQ // rows, merge_body, 0)


def dilated_attention(b_qk, b_v, batch):
    hp = DIL_HEADS_PER_GROUP
    in_specs, args = [], []
    for g in range(len(DIL_GROUPS)):
        for arr, c0 in ((b_qk, 0), (b_qk, DIL_HEADS), (b_v, 0)):
            in_specs.append(pl.BlockSpec((SEQ, HEAD_DIM), lambda b, j, c0=c0, g=g: (b, c0 + g * hp + j)))
            args.append(arr)
    return pl.pallas_call(
        _dilated_kernel,
        grid=(batch, hp),
        in_specs=in_specs,
        out_specs=pl.BlockSpec((SEQ, HEAD_DIM), lambda b, j: (b, j)),
        out_shape=jax.ShapeDtypeStruct((batch * SEQ, hp * HEAD_DIM), BF16),
        scratch_shapes=[pltpu.VMEM((SEQ, HEAD_DIM), F32)] * 6,
        compiler_params=_params("parallel", "parallel"),
        name="dilated_attention",
    )(*args)


def _moba_kernel(q_ref, k_ref, vt_ref, o_ref, kmean_scr, bias_scr):
    i = pl.program_id(1)
    tq, nb, d = MOBA_BLOCK, MOBA_NB, HEAD_DIM
    heads = range(MOBA_HEADS)
    col = lambda h: slice(h * d, (h + 1) * d)

    @pl.when(i == 0)
    def _():
        avg = jnp.where(_iota((nb, SEQ), 1) // MOBA_BLOCK == _iota((nb, SEQ), 0), 1.0 / MOBA_BLOCK, 0.0)
        kmean_scr[...] = _dot(avg.astype(BF16), k_ref[...]).astype(kmean_scr.dtype)

    blk = _iota((nb, tq), 0)
    own = pl.multiple_of(i * tq, tq)
    causal = _iota((tq, tq), 0) <= _iota((tq, tq), 1)
    qs = [q_ref[:, col(h)] for h in heads]
    gates = [jnp.where(blk < i, _dot_nt(kmean_scr[:, col(h)], qs[h]), NEG_INF) for h in heads]
    ss = [_dot_nt(k_ref[pl.ds(own, tq), col(h)], qs[h]) for h in heads]
    for h in heads:
        gate = gates[h]
        rank = jnp.zeros((nb, tq), F32)
        for j in range(nb):
            gj = gate[j:j + 1, :]
            ahead = (gj > gate) | ((gj == gate) & (blk > j))
            rank = rank + jnp.where(ahead, 1.0, 0.0)
        bias_scr[h] = jnp.where((rank < float(MOBA_TOPK)) & (gate > NEG_INF * 0.5), 0.0, NEG_INF)
    ss = [jnp.where(causal, s, NEG_INF) for s in ss]
    ms = [jnp.max(s, axis=0, keepdims=True) for s in ss]
    ps = [jnp.exp2(s - m) for s, m in zip(ss, ms)]
    ls = [jnp.sum(p, axis=0, keepdims=True) for p in ps]
    accs = [_dot(vt_ref[col(h), pl.ds(own, tq)], ps[h].astype(BF16)) for h in heads]

    def body(kj, carry):
        off = pl.multiple_of(kj * tq, tq)
        bs = [bias_scr[h, pl.ds(kj, 1), :] for h in heads]
        ss = [_dot_nt(k_ref[pl.ds(off, tq), col(h)], qs[h]) for h in heads]
        m_new = [jnp.maximum(carry[h][0], jnp.max(ss[h], axis=0, keepdims=True) + bs[h]) for h in heads]
        alpha = [jnp.exp2(carry[h][0] - m_new[h]) for h in heads]
        ps = [jnp.exp2(ss[h] - (m_new[h] - bs[h])) for h in heads]
        ls = [alpha[h] * carry[h][1] + jnp.sum(ps[h], axis=0, keepdims=True) for h in heads]
        pvs = [_dot(vt_ref[col(h), pl.ds(off, tq)], ps[h].astype(BF16)) for h in heads]
        return tuple((m_new[h], ls[h], alpha[h] * carry[h][2] + pvs[h]) for h in heads)

    fin = lax.fori_loop(0, i, body, tuple((ms[h], ls[h], accs[h]) for h in heads))
    for h in heads:
        _, l, acc = fin[h]
        o_ref[:, col(h)] = jnp.transpose(acc / l).astype(o_ref.dtype)


def moba_attention(c_qk, vt, batch):
    tq = MOBA_BLOCK
    nq = SEQ // tq
    return pl.pallas_call(
        _moba_kernel,
        grid=(batch, nq),
        in_specs=[pl.BlockSpec((tq, C_QKV), lambda b, i: (b * nq + i, 0)),
                  pl.BlockSpec((SEQ, C_QKV), lambda b, i: (b, 1)),
                  pl.BlockSpec((C_QKV, SEQ), lambda b, i: (0, b))],
        out_specs=pl.BlockSpec((tq, C_QKV), lambda b, i: (b * nq + i, 0)),
        out_shape=jax.ShapeDtypeStruct((batch * SEQ, C_QKV), BF16),
        scratch_shapes=[pltpu.VMEM((MOBA_NB, C_QKV), BF16), pltpu.VMEM((MOBA_HEADS, MOBA_NB, tq), F32)],
        compiler_params=_params("parallel", "arbitrary"),
        name="moba_attention",
    )(c_qk, c_qk, vt)


def _merge_kernel(oa_ref, ob_ref, oc_ref, ga_ref, gb_ref, gc_ref, wa_ref, wb_ref, wc_ref, out_ref):
    y = ga_ref[...] * _dot(oa_ref[...], wa_ref[...])
    y = y + gb_ref[...] * _dot(ob_ref[...], wb_ref[...])
    y = y + gc_ref[...] * _dot(oc_ref[...], wc_ref[...])
    out_ref[...] = y.astype(out_ref.dtype)


def gated_merge(o_a, o_b, o_c, gates_m, w_a, w_b, w_c, tm=1024, tn=512):
    m = o_a.shape[0]
    nb = D_MODEL // tn
    act = lambda width: pl.BlockSpec((tm, width), lambda j, i: (i, 0))
    gate = lambda g: pl.BlockSpec((tm, tn), lambda j, i, g=g: (i, g * nb + j))
    wgt = lambda width: pl.BlockSpec((width, tn), lambda j, i: (0, j))
    return pl.pallas_call(
        _merge_kernel,
        grid=(nb, m // tm),
        in_specs=[act(o_a.shape[1]), act(o_b.shape[1]), act(o_c.shape[1]),
                  gate(0), gate(1), gate(2),
                  wgt(w_a.shape[0]), wgt(w_b.shape[0]), wgt(w_c.shape[0])],
        out_specs=pl.BlockSpec((tm, tn), lambda j, i: (i, j)),
        out_shape=jax.ShapeDtypeStruct((m, D_MODEL), BF16),
        compiler_params=_params("parallel", "parallel"),
        name="gated_merge",
    )(o_a, o_b, o_c, gates_m, gates_m, gates_m, w_a, w_b, w_c)


def _out_proj_kernel(y_ref, w_ref, x_ref, g_ref, xo_ref, h_ref):
    x_new = x_ref[...] + _dot(y_ref[...], w_ref[...])
    xo_ref[...] = x_new
    h_ref[...] = _rmsnorm_rows(x_new, g_ref[...]).astype(h_ref.dtype)


def out_proj_residual_norm(y, w_o, x2, g, tm=512):
    m, d = x2.shape
    row = pl.BlockSpec((tm, d), lambda i: (i, 0))
    return pl.pallas_call(
        _out_proj_kernel,
        grid=(m // tm,),
        in_specs=[row, pl.BlockSpec((d, d), lambda i: (0, 0)), row, pl.BlockSpec((1, d), lambda i: (0, 0))],
        out_specs=[row, row],
        out_shape=[jax.ShapeDtypeStruct((m, d), F32), jax.ShapeDtypeStruct((m, d), BF16)],
        compiler_params=_params("parallel"),
        name="out_proj",
    )(y, w_o, x2, g.reshape(1, d))


def _mlp_kernel(h_ref, w1_ref, w2_ref, x_ref, g_ref, *rest):
    xo_ref = rest[0] if len(rest) == 3 else None
    hn_ref, acc_scr = rest[-2:]
    f = pl.program_id(1)

    @pl.when(f == 0)
    def _():
        acc_scr[...] = jnp.zeros_like(acc_scr)

    u = jnp.square(jnp.maximum(_dot(h_ref[...], w1_ref[...]), 0.0))
    acc_scr[...] += _dot(u.astype(BF16), w2_ref[...])

    @pl.when(f == pl.num_programs(1) - 1)
    def _():
        x_new = x_ref[...] + acc_scr[...]
        if xo_ref is not None:
            xo_ref[...] = x_new
        hn_ref[...] = _rmsnorm_rows(x_new, g_ref[...]).astype(hn_ref.dtype)


def mlp_residual_norm(h2, w1, w2, x2, g_next, next_dtype, emit_x, tm=1024, tf=512):
    m, d = x2.shape
    row = pl.BlockSpec((tm, d), lambda i, f: (i, 0))
    row_once = pl.BlockSpec((tm, d), lambda i, f: (i, 0), pipeline_mode=pl.Buffered(1))
    out_shape = [jax.ShapeDtypeStruct((m, d), F32)] * emit_x + [jax.ShapeDtypeStruct((m, d), next_dtype)]
    res = pl.pallas_call(
        _mlp_kernel,
        grid=(m // tm, D_FF // tf),
        in_specs=[row, pl.BlockSpec((d, tf), lambda i, f: (0, f)), pl.BlockSpec((tf, d), lambda i, f: (f, 0)),
                  row_once, pl.BlockSpec((1, d), lambda i, f: (0, 0))],
        out_specs=[row_once] * len(out_shape),
        out_shape=out_shape,
        scratch_shapes=[pltpu.VMEM((tm, d), F32)],
        compiler_params=_params("parallel", "arbitrary"),
        name="mlp",
    )(h2, w1, w2, x2, g_next.reshape(1, d))
    return (res[0], res[1]) if emit_x else (None, res[0])


def _rope_tables():
    inv = ROPE_THETA ** (-jnp.arange(0, ROPE_DIM, 2, dtype=F32) / ROPE_DIM)
    ang = jnp.arange(SEQ, dtype=F32)[:, None] * inv[None, :]
    cos, sin = jnp.cos(ang), jnp.sin(ang)
    zeros = jnp.zeros((SEQ, HEAD_DIM - ROPE_DIM), F32)
    zero_h = jnp.zeros((SEQ, ROPE_HALF), F32)
    c = jnp.concatenate([cos, cos, jnp.ones_like(zeros)], axis=1)
    s_up = jnp.concatenate([-sin, zero_h, zeros], axis=1)
    s_dn = jnp.concatenate([zero_h, sin, zeros], axis=1)
    return c, s_up, s_dn


def _overlap_table_t():
    cs = np.arange(NSA_M_PAD)[None, :] * NSA_CMP_STRIDE
    bs = np.arange(NSA_NB)[:, None] * NSA_SEL_BLOCK
    ov = np.clip(np.minimum(cs + NSA_CMP_LEN, bs + NSA_SEL_BLOCK) - np.maximum(cs, bs), 0, None) / NSA_CMP_LEN
    ov[:, NSA_M_PAD - 1] = 0.0
    return jnp.asarray(ov, dtype=BF16)


def _split_w_in(w):
    pts = np.cumsum((0,) + IN_SPLIT_SIZES)
    names = ("a_q", "a_kc", "a_vc", "a_ks", "a_vs", "a_kw", "a_vw", "a_g",
             "b_q", "b_k", "b_v", "c_q", "c_k", "c_v", "m_a", "m_b", "m_c")
    return {n: w[:, pts[i]:pts[i + 1]] for i, n in enumerate(names)}


def _scales(*widths_and_values):
    return jnp.concatenate([jnp.full((1, w), v, F32) for w, v in widths_and_values], axis=1)


def _layer(x2, h, batch, tabs, overlap_t, w_in, pe_k, w1_k, w2_k, pe_v, w1_v, w2_v,
           w_br_a, w_br_b, w_br_c, w_o, mlp_g, w_mlp_in, w_mlp_out, g_next, last):
    w = _split_w_in(w_in)
    per_group = 3 * NSA_REP
    gate_end = int(np.sum(IN_SPLIT_SIZES[:8]))
    a_g = [jnp.pad(w["a_g"][:, g * per_group:(g + 1) * per_group], ((0, 0), (0, LANES - per_group)))
           for g in range(NSA_KV_HEADS)]
    w_pad = jnp.concatenate([w_in[:, :gate_end - A_G], w_in[:, gate_end:]] + a_g, axis=1).astype(BF16)
    names = ("a_q", "a_kc", "a_vc", "a_ks", "a_vs", "a_kw", "a_vw",
             "b_q", "b_k", "b_v", "c_q", "c_k", "c_v", "m_a", "m_b", "m_c")
    col, at = {}, 0
    for n in names:
        col[n] = at
        at += w[n].shape[1]
    col["a_g"] = at
    proj = functools.partial(project, h, w_pad)

    q_raw, q_rot = proj(col["a_q"], A_Q, "both", BF16, tabs, _scales((A_Q, Q_SCALE)))
    cmp_src = proj(col["a_kc"], 2 * A_KV, "plain", F32)
    k_sel = proj(col["a_ks"], A_KV, "rope", BF16, tabs, _scales((A_KV, 1.0)))
    k_win = proj(col["a_kw"], A_KV, "rope", BF16, tabs, _scales((A_KV, 1.0)))
    b_qk = proj(col["b_q"], 2 * B_QKV, "rope", F32, tabs, _scales((B_QKV, Q_SCALE), (B_QKV, 1.0)))
    b_v = proj(col["b_v"], B_QKV, "plain", F32)
    c_qk = proj(col["c_q"], 2 * C_QKV, "rope", BF16, tabs, _scales((C_QKV, Q_SCALE), (C_QKV, 1.0)))
    gates_m = proj(col["m_a"], 3 * D_MODEL, "sigmoid", BF16)
    gates_a = proj(col["a_g"], NSA_KV_HEADS * LANES, "sigmoid", F32)
    wt_v = jnp.transpose(jnp.concatenate([w["c_v"], w["a_vs"], w["a_vw"]], axis=1)).astype(BF16)
    vt = project_transposed(h, wt_v, BF16)

    flat = NSA_CMP_LEN * HEAD_DIM
    cmp_kv = nsa_compress(cmp_src, pe_k.reshape(1, flat), w1_k.reshape(flat, NSA_CMP_HIDDEN).astype(BF16),
                          w2_k.astype(BF16), pe_v.reshape(1, flat),
                          w1_v.reshape(flat, NSA_CMP_HIDDEN).astype(BF16), w2_v.astype(BF16), batch)
    o_a = nsa_attention(q_raw, q_rot, cmp_kv, k_sel, k_win, vt, gates_a, overlap_t, batch)
    o_b = dilated_attention(b_qk, b_v, batch)
    o_c = moba_attention(c_qk, vt, batch)

    merged = gated_merge(o_a, o_b, o_c, gates_m, w_br_a.astype(BF16), w_br_b.astype(BF16), w_br_c.astype(BF16))
    x2, h2 = out_proj_residual_norm(merged, w_o.astype(BF16), x2, mlp_g)
    return mlp_residual_norm(h2, w_mlp_in.astype(BF16), w_mlp_out.astype(BF16), x2, g_next,
                             F32 if last else BF16, emit_x=not last)


def kernel(x, attn_norm_g, w_in, cmp_pe_k, cmp_w1_k, cmp_w2_k, cmp_pe_v, cmp_w1_v, cmp_w2_v,
           w_br_a, w_br_b, w_br_c, w_o, mlp_norm_g, w_mlp_in, w_mlp_out, final_norm_g):
    batch, seq, d = x.shape
    assert seq == SEQ and d == D_MODEL
    depth = w_in.shape[0]
    tabs = _rope_tables()
    overlap_t = _overlap_table_t()
    x2 = x.reshape(batch * seq, d)
    h = rmsnorm(x2, attn_norm_g[0], BF16)
    for l in range(depth):
        last = l == depth - 1
        g_next = final_norm_g if last else attn_norm_g[l + 1]
        x2, h = _layer(x2, h, batch, tabs, overlap_t, w_in[l],
                       cmp_pe_k[l], cmp_w1_k[l], cmp_w2_k[l], cmp_pe_v[l], cmp_w1_v[l], cmp_w2_v[l],
                       w_br_a[l], w_br_b[l], w_br_c[l], w_o[l], mlp_norm_g[l], w_mlp_in[l], w_mlp_out[l],
                       g_next, last)
    return h.reshape(batch, seq, d)
```

```python
import functools
import math

import numpy as np
import jax
import jax.numpy as jnp
from jax import lax
from jax.experimental import pallas as pl
from jax.experimental.pallas import tpu as pltpu

D_MODEL = 2048
SEQ = 2048
HEAD_DIM = 128
ROPE_THETA = 500000.0
ROPE_DIM = HEAD_DIM // 4
ROPE_HALF = ROPE_DIM // 2
NORM_EPS = 1e-6
NEG_INF = -1e30
Q_SCALE = HEAD_DIM ** -0.5 * math.log2(math.e)

NSA_HEADS = 8
NSA_KV_HEADS = 2
NSA_REP = NSA_HEADS // NSA_KV_HEADS
NSA_CMP_LEN = 32
NSA_CMP_STRIDE = 16
NSA_CMP_HIDDEN = 256
NSA_SEL_BLOCK = 64
NSA_SEL_TOPN = 16
NSA_WINDOW = 512
NSA_FORCE_BONUS = 1e4
NSA_NB = SEQ // NSA_SEL_BLOCK
NSA_M_PAD = SEQ // NSA_CMP_STRIDE

DIL_GROUPS = ((128, 1), (512, 4), (2048, 16))
DIL_HEADS_PER_GROUP = 4
DIL_HEADS = DIL_HEADS_PER_GROUP * len(DIL_GROUPS)
DIL_TILE = 128
DIL_UNROLL = 8

MOBA_HEADS = 8
MOBA_BLOCK = 256
MOBA_TOPK = 3
MOBA_NB = SEQ // MOBA_BLOCK

D_FF = 4 * D_MODEL
A_Q = NSA_HEADS * HEAD_DIM
A_KV = NSA_KV_HEADS * HEAD_DIM
A_G = 3 * NSA_HEADS
B_QKV = DIL_HEADS * HEAD_DIM
C_QKV = MOBA_HEADS * HEAD_DIM
IN_SPLIT_SIZES = (A_Q, A_KV, A_KV, A_KV, A_KV, A_KV, A_KV, A_G,
                  B_QKV, B_QKV, B_QKV, C_QKV, C_QKV, C_QKV,
                  D_MODEL, D_MODEL, D_MODEL)

LANES = 128
PROJ_TN_MAX = 512
PROJ_VMEM_BUDGET = 46 * 1024 * 1024
VMEM_LIMIT = 56 * 1024 * 1024

BF16 = jnp.bfloat16
F32 = jnp.float32


def _params(*sem):
    return pltpu.CompilerParams(dimension_semantics=sem, vmem_limit_bytes=VMEM_LIMIT)


def _dot(a, b):
    return jnp.dot(a, b, preferred_element_type=F32)


def _dot_nt(a, b):
    return lax.dot_general(a, b, (((1,), (1,)), ((), ())), preferred_element_type=F32)


def _iota(shape, axis):
    return lax.broadcasted_iota(jnp.int32, shape, axis)


def _rmsnorm_rows(x, g):
    y = x * lax.rsqrt(jnp.mean(x * x, axis=-1, keepdims=True) + NORM_EPS)
    return y * g


def _rmsnorm_kernel(x_ref, g_ref, h_ref):
    h_ref[...] = _rmsnorm_rows(x_ref[...], g_ref[...]).astype(h_ref.dtype)


def rmsnorm(x2, g, out_dtype, tm=512):
    m, d = x2.shape
    return pl.pallas_call(
        _rmsnorm_kernel,
        grid=(m // tm,),
        in_specs=[pl.BlockSpec((tm, d), lambda i: (i, 0)), pl.BlockSpec((1, d), lambda i: (0, 0))],
        out_specs=pl.BlockSpec((tm, d), lambda i: (i, 0)),
        out_shape=jax.ShapeDtypeStruct((m, d), out_dtype),
        compiler_params=_params("parallel"),
        name="rmsnorm",
    )(x2, g.reshape(1, d))


def _rope_lanes(acc, c, s_up, s_dn):
    tn = acc.shape[1]
    reps = tn // HEAD_DIM
    if reps > 1:
        c = jnp.concatenate([c] * reps, axis=1)
        s_up = jnp.concatenate([s_up] * reps, axis=1)
        s_dn = jnp.concatenate([s_dn] * reps, axis=1)
    up = pltpu.roll(acc, tn - ROPE_HALF, axis=1)
    dn = pltpu.roll(acc, ROPE_HALF, axis=1)
    return acc * c + up * s_up + dn * s_dn


def _proj_kernel(*refs, mode):
    if mode in ("rope", "both"):
        h_ref, w_ref, c_ref, su_ref, sd_ref, cs_ref = refs[:6]
        outs = refs[6:]
    else:
        h_ref, w_ref = refs[:2]
        outs = refs[2:]
    acc = _dot(h_ref[...], w_ref[...])
    if mode == "plain":
        outs[0][...] = acc.astype(outs[0].dtype)
    elif mode == "sigmoid":
        outs[0][...] = jax.nn.sigmoid(acc).astype(outs[0].dtype)
    else:
        col_scale = cs_ref[...]
        roped = _rope_lanes(acc, c_ref[...], su_ref[...], sd_ref[...]) * col_scale
        if mode == "both":
            outs[0][...] = (acc * col_scale).astype(outs[0].dtype)
            outs[1][...] = roped.astype(outs[1].dtype)
        else:
            outs[0][...] = roped.astype(outs[0].dtype)


def _proj_tiles(k, n, col0, out_bytes, n_out, rope):
    tn = math.gcd(math.gcd(PROJ_TN_MAX, n), col0) if col0 else math.gcd(PROJ_TN_MAX, n)
    for tm in (SEQ, SEQ // 2):
        blocks = tm * k * 2 + k * tn * 2 + n_out * tm * tn * out_bytes + (3 * tm * HEAD_DIM * 4 if rope else 0)
        if 2 * blocks <= PROJ_VMEM_BUDGET:
            break
    return tm, tn


def project(h, w_pad, layer, col0, n, mode, out_dtype, rope_tabs=None, col_scale=None):
    m, k = h.shape
    n_out = 2 if mode == "both" else 1
    rope = mode in ("rope", "both")
    tm, tn = _proj_tiles(k, n, col0, jnp.dtype(out_dtype).itemsize, n_out, rope)
    assert m % tm == 0 and n % tn == 0 and col0 % tn == 0 and SEQ % tm == 0
    j0 = col0 // tn
    in_specs = [pl.BlockSpec((tm, k), lambda i, j: (i, 0)),
                pl.BlockSpec((None, k, tn), lambda i, j: (layer, 0, j0 + j))]
    args = [h, w_pad]
    if rope:
        pos_blocks = SEQ // tm
        for t in rope_tabs:
            in_specs.append(pl.BlockSpec((tm, HEAD_DIM), lambda i, j: (i % pos_blocks, 0)))
            args.append(t)
        in_specs.append(pl.BlockSpec((1, tn), lambda i, j: (0, j)))
        args.append(col_scale)
    out_spec = pl.BlockSpec((tm, tn), lambda i, j: (i, j))
    out_shape = jax.ShapeDtypeStruct((m, n), out_dtype)
    res = pl.pallas_call(
        functools.partial(_proj_kernel, mode=mode),
        grid=(m // tm, n // tn),
        in_specs=in_specs,
        out_specs=[out_spec] * n_out,
        out_shape=[out_shape] * n_out,
        compiler_params=_params("parallel", "parallel"),
        name="proj_" + mode,
    )(*args)
    return res if n_out == 2 else res[0]


def _proj_t_kernel(wt_ref, h_ref, out_ref):
    out_ref[...] = _dot_nt(wt_ref[...], h_ref[...]).astype(out_ref.dtype)


def project_transposed(h, wt, layer, out_dtype, tm=1024, tn=512):
    m, k = h.shape
    n = wt.shape[1]
    return pl.pallas_call(
        _proj_t_kernel,
        grid=(n // tn, m // tm),
        in_specs=[pl.BlockSpec((None, tn, k), lambda j, i: (layer, j, 0)),
                  pl.BlockSpec((tm, k), lambda j, i: (i, 0))],
        out_specs=pl.BlockSpec((tn, tm), lambda j, i: (j, i)),
        out_shape=jax.ShapeDtypeStruct((n, m), out_dtype),
        compiler_params=_params("parallel", "parallel"),
        name="proj_transposed",
    )(wt, h)


def _compress_kernel(k0_ref, k1_ref, v0_ref, v1_ref, pek_ref, w1k_ref, w2k_ref, pev_ref, w1v_ref, w2v_ref,
                     out_ref):
    half = NSA_CMP_STRIDE * HEAD_DIM
    for idx, src_ref in enumerate((k0_ref, k1_ref, v0_ref, v1_ref)):
        is_k = idx < NSA_KV_HEADS
        pe_ref, w1_ref, w2_ref = (pek_ref, w1k_ref, w2k_ref) if is_k else (pev_ref, w1v_ref, w2v_ref)
        x = jnp.concatenate(
            [src_ref[pl.ds(l, NSA_M_PAD, stride=NSA_CMP_STRIDE), :]
             for l in range(NSA_CMP_STRIDE)], axis=1)
        pe = pe_ref[...]
        first = _dot((x + pe[:, :half]).astype(BF16), w1_ref[:half, :])
        second = _dot((x + pe[:, half:]).astype(BF16), w1_ref[half:, :])
        hid = jax.nn.gelu(first + pltpu.roll(second, NSA_M_PAD - 1, axis=0))
        out = _dot(hid.astype(BF16), w2_ref[...])
        out_ref[idx] = (out if is_k else jnp.transpose(out)).astype(out_ref.dtype)


def nsa_compress(pf, pe_k, w1_k, w2_k, pe_v, w1_v, w2_v, batch):
    flat = NSA_CMP_LEN * HEAD_DIM
    const = lambda shape: pl.BlockSpec(shape, lambda b: (0,) * len(shape))
    return pl.pallas_call(
        _compress_kernel,
        grid=(batch,),
        in_specs=[pl.BlockSpec((SEQ, HEAD_DIM), lambda b, c=c: (b, c)) for c in range(4)] + [
                  const((1, flat)), const((flat, NSA_CMP_HIDDEN)), const((NSA_CMP_HIDDEN, HEAD_DIM)),
                  const((1, flat)), const((flat, NSA_CMP_HIDDEN)), const((NSA_CMP_HIDDEN, HEAD_DIM))],
        out_specs=pl.BlockSpec((None, 4, NSA_M_PAD, HEAD_DIM), lambda b: (b, 0, 0, 0)),
        out_shape=jax.ShapeDtypeStruct((batch, 4, NSA_M_PAD, HEAD_DIM), BF16),
        compiler_params=_params("parallel"),
        name="nsa_compress",
    )(pf, pf, pf, pf, pe_k, w1_k, w2_k, pe_v, w1_v, w2_v)


NSA_TQ = 256
NSA_KC = 256
NSA_BLK_PER_CHUNK = NSA_KC // NSA_SEL_BLOCK


def _nsa_kernel(qraw_ref, qrot_ref, cmp_ref, ksel_ref, vselt_ref, kwin_ref, vwint_ref,
                gate_ref, ovt_ref, o_ref, bias_scr):
    i = pl.program_id(1)
    tq, kc_w, rep, d = NSA_TQ, NSA_KC, NSA_REP, HEAD_DIM
    groups = range(NSA_KV_HEADS)
    width = rep * tq
    bpc = NSA_BLK_PER_CHUNK
    t0 = i * tq
    head_cols = lambda g, r: slice((g * rep + r) * d, (g * rep + r + 1) * d)
    stack = lambda ref, g: jnp.concatenate([ref[:, head_cols(g, r)] for r in range(rep)], axis=0)
    q_raw = [stack(qraw_ref, g) for g in groups]
    q_rot = [stack(qrot_ref, g) for g in groups]
    k_of = lambda ref, g, kj: ref[pl.ds(pl.multiple_of(kj * kc_w, kc_w), kc_w), g * d:(g + 1) * d]
    vt_of = lambda ref, g, kj: ref[g * d:(g + 1) * d, pl.ds(pl.multiple_of(kj * kc_w, kc_w), kc_w)]
    col_max = lambda s: jnp.max(s, axis=0, keepdims=True)
    col_sum = lambda p: jnp.sum(p, axis=0, keepdims=True)

    q_pos = t0 + (_iota((NSA_M_PAD, width), 1) & (tq - 1))
    vis = (_iota((NSA_M_PAD, width), 0) * NSA_CMP_STRIDE + (NSA_CMP_LEN - 1)) <= q_pos
    sc = [jnp.where(vis, _dot_nt(cmp_ref[g], q_raw[g]), NEG_INF) for g in groups]
    ec = [jnp.where(vis, jnp.exp2(s - col_max(s)), 0.0) for s in sc]
    pc = [(e / jnp.maximum(col_sum(e), 1e-30)).astype(BF16) for e in ec]
    o_cmp = [_dot(cmp_ref[NSA_KV_HEADS + g], pc[g]) for g in groups]
    imp_heads = [_dot(ovt_ref[...], p) for p in pc]

    blk = _iota((NSA_NB, tq), 0)
    q_blk = (t0 + _iota((NSA_NB, tq), 1)) // NSA_SEL_BLOCK
    forced = (blk == 0) | (blk == q_blk) | (blk == q_blk - 1)
    for g in groups:
        imp = imp_heads[g][:, :tq]
        for r in range(1, rep):
            imp = imp + imp_heads[g][:, r * tq:(r + 1) * tq]
        score = jnp.where(blk <= q_blk, imp + jnp.where(forced, NSA_FORCE_BONUS, 0.0), NEG_INF)
        rank = jnp.zeros((NSA_NB, tq), F32)
        for j in range(NSA_NB):
            sj = score[j:j + 1, :]
            ahead = (sj > score) | ((sj == score) & (blk > j))
            rank = rank + jnp.where(ahead, 1.0, 0.0)
        bias = jnp.where((rank < float(NSA_SEL_TOPN)) & (score > NEG_INF * 0.5), 0.0, NEG_INF)
        bias = jnp.concatenate([bias] * rep, axis=1)
        for c in range(SEQ // kc_w):
            bias_scr[g, c] = bias[c * bpc:(c + 1) * bpc, :]

    def sel_step(kj, ss, carry):
        bs = [bias_scr[g, kj] for g in groups]
        s3 = [s.reshape(bpc, NSA_SEL_BLOCK, width) for s in ss]
        m_new = [jnp.maximum(carry[g][0], col_max(jnp.max(s3[g], axis=1) + bs[g])) for g in groups]
        alpha = [jnp.exp2(carry[g][0] - m_new[g]) for g in groups]
        ps = [jnp.exp2(s3[g] - (m_new[g] - bs[g])[:, None, :]).reshape(kc_w, width) for g in groups]
        ls = [alpha[g] * carry[g][1] + col_sum(ps[g]) for g in groups]
        pvs = [_dot(vt_of(vselt_ref, g, kj), ps[g].astype(BF16)) for g in groups]
        return tuple((m_new[g], ls[g], alpha[g] * carry[g][2] + pvs[g]) for g in groups)

    init = tuple((jnp.full((1, width), NEG_INF, F32), jnp.zeros((1, width), F32), jnp.zeros((d, width), F32))
                 for g in groups)
    carry = lax.fori_loop(
        0, i, lambda kj, c: sel_step(kj, [_dot_nt(k_of(ksel_ref, g, kj), q_rot[g]) for g in groups], c), init)

    key_row = _iota((kc_w, width), 0)
    q_col = _iota((kc_w, width), 1) & (tq - 1)
    causal = key_row <= q_col
    diag = [jnp.where(causal, _dot_nt(k_of(ksel_ref, g, i), q_rot[g]), NEG_INF) for g in groups]
    sel = sel_step(i, diag, carry)

    far = i - 2
    near = i - 1
    far_ok = jnp.where(far >= 0, 0.0, NEG_INF)
    near_ok = jnp.where(near >= 0, 0.0, NEG_INF)
    far_c = jnp.maximum(far, 0)
    near_c = jnp.maximum(near, 0)
    s_own = [jnp.where(causal, _dot_nt(k_of(kwin_ref, g, i), q_rot[g]), NEG_INF) for g in groups]
    s_near = [_dot_nt(k_of(kwin_ref, g, near_c), q_rot[g]) + near_ok for g in groups]
    s_far = [jnp.where(key_row > q_col, _dot_nt(k_of(kwin_ref, g, far_c), q_rot[g]), NEG_INF) + far_ok
             for g in groups]
    m_w = [jnp.maximum(jnp.maximum(col_max(s_own[g]), col_max(s_near[g])), col_max(s_far[g])) for g in groups]
    p_own = [jnp.exp2(s_own[g] - m_w[g]) for g in groups]
    p_near = [jnp.exp2(s_near[g] - m_w[g]) for g in groups]
    p_far = [jnp.exp2(s_far[g] - m_w[g]) for g in groups]
    l_win = [col_sum(p_own[g]) + col_sum(p_near[g]) + col_sum(p_far[g]) for g in groups]
    acc_win = [_dot(vt_of(vwint_ref, g, i), p_own[g].astype(BF16))
               + _dot(vt_of(vwint_ref, g, near_c), p_near[g].astype(BF16))
               + _dot(vt_of(vwint_ref, g, far_c), p_far[g].astype(BF16)) for g in groups]

    for g in groups:
        o_sel = sel[g][2] / sel[g][1]
        o_win = acc_win[g] / l_win[g]
        gates = jnp.transpose(gate_ref[:, g * LANES:(g + 1) * LANES])
        for r in range(rep):
            lanes = slice(r * tq, (r + 1) * tq)
            o = (gates[3 * r:3 * r + 1, :] * o_cmp[g][:, lanes] + gates[3 * r + 1:3 * r + 2, :] * o_sel[:, lanes]
                 + gates[3 * r + 2:3 * r + 3, :] * o_win[:, lanes])
            o_ref[:, head_cols(g, r)] = jnp.transpose(o).astype(o_ref.dtype)


def nsa_attention(q_raw, q_rot, cmp_kv, k_sel, k_win, vt, gates_a, overlap_t, batch):
    tq = NSA_TQ
    nq = SEQ // tq
    g_n = NSA_KV_HEADS
    row = lambda b, i: (b * nq + i, 0)
    vt_r0 = C_QKV // A_KV
    return pl.pallas_call(
        _nsa_kernel,
        grid=(batch, nq),
        in_specs=[pl.BlockSpec((tq, A_Q), row),
                  pl.BlockSpec((tq, A_Q), row),
                  pl.BlockSpec((None, 2 * g_n, NSA_M_PAD, HEAD_DIM), lambda b, i: (b, 0, 0, 0)),
                  pl.BlockSpec((SEQ, A_KV), lambda b, i: (b, 0)),
                  pl.BlockSpec((A_KV, SEQ), lambda b, i: (vt_r0, b)),
                  pl.BlockSpec((SEQ, A_KV), lambda b, i: (b, 0)),
                  pl.BlockSpec((A_KV, SEQ), lambda b, i: (vt_r0 + 1, b)),
                  pl.BlockSpec((tq, g_n * LANES), row),
                  pl.BlockSpec((NSA_NB, NSA_M_PAD), lambda b, i: (0, 0))],
        out_specs=pl.BlockSpec((tq, A_Q), row),
        out_shape=jax.ShapeDtypeStruct((batch * SEQ, A_Q), BF16),
        scratch_shapes=[pltpu.VMEM((g_n, SEQ // NSA_KC, NSA_BLK_PER_CHUNK, NSA_REP * tq), F32)],
        compiler_params=_params("parallel", "parallel"),
        name="nsa_attention",
    )(q_raw, q_rot, cmp_kv, k_sel, vt, k_win, vt, gates_a, overlap_t)


def _dilated_kernel(q0, k0, v0, q1, k1, v1, q2, k2, v2, o_ref,
                    o0_scr, o1_scr, o2_scr, l0_scr, l1_scr, l2_scr):
    tile = DIL_TILE
    groups = ((q0, k0, v0, o0_scr, l0_scr), (q1, k1, v1, o1_scr, l1_scr), (q2, k2, v2, o2_scr, l2_scr))
    for (window, dil), (q_ref, k_ref, v_ref, og_scr, lg_scr) in zip(DIL_GROUPS, groups):
        assert window // dil == tile
        per_class = SEQ // dil
        tiles_per_class = per_class // tile
        nk = tile if tiles_per_class == 1 else 2 * tile
        a_minus_a = _iota((tile, nk), 0) - _iota((tile, nk), 1)

        def step(u, carry, q_ref=q_ref, k_ref=k_ref, v_ref=v_ref, og_scr=og_scr, lg_scr=lg_scr,
                 dil=dil, tiles_per_class=tiles_per_class, nk=nk, a_minus_a=a_minus_a):
            ts = [u * DIL_UNROLL + a for a in range(DIL_UNROLL)]
            cls = [t // tiles_per_class for t in ts]
            n0 = [(t % tiles_per_class) * tile for t in ts]
            kbase = [jnp.maximum(n - (nk - tile), 0) for n in n0]
            q_rows = [pl.ds(c + dil * n, tile, stride=dil) for c, n in zip(cls, n0)]
            k_rows = [pl.ds(c + dil * kb, nk, stride=dil) for c, kb in zip(cls, kbase)]
            qs = [q_ref[r, :].astype(BF16) for r in q_rows]
            ks = [k_ref[r, :].astype(BF16) for r in k_rows]
            vs = [v_ref[r, :].astype(BF16) for r in k_rows]
            ss = [_dot_nt(q, k) for q, k in zip(qs, ks)]
            masks = []
            for n, kb in zip(n0, kbase):
                dist = a_minus_a + (n - kb)
                masks.append((dist >= 0) & (dist <= tile))
            ss = [jnp.where(mk, s, NEG_INF) for mk, s in zip(masks, ss)]
            ms = [jnp.max(s, axis=-1, keepdims=True) for s in ss]
            es = [jnp.where(mk, jnp.exp2(s - m), 0.0) for mk, s, m in zip(masks, ss, ms)]
            dens = [jnp.maximum(jnp.sum(e, axis=-1, keepdims=True), 1e-30) for e in es]
            os_ = [_dot((e / den).astype(BF16), v) for e, den, v in zip(es, dens, vs)]
            for r, o, m, den in zip(q_rows, os_, ms, dens):
                og_scr[r, :] = o
                lg_scr[r, :] = jnp.broadcast_to(m + jnp.log2(den), (tile, HEAD_DIM))
            return carry

        lax.fori_loop(0, SEQ // tile // DIL_UNROLL, step, 0)

    rows = 256

    def merge_body(c, carry):
        sl = pl.ds(pl.multiple_of(c * rows, rows), rows)
        la, lb, lc = l0_scr[sl, :], l1_scr[sl, :], l2_scr[sl, :]
        mx = jnp.maximum(jnp.maximum(la, lb), lc)
        ea, eb, ec = jnp.exp2(la - mx), jnp.exp2(lb - mx), jnp.exp2(lc - mx)
        tot = ea + eb + ec
        out = (ea / tot) * o0_scr[sl, :] + (eb / tot) * o1_scr[sl, :] + (ec / tot) * o2_scr[sl, :]
        o_ref[sl, :] = out.astype(o_ref.dtype)
        return carry

    lax.fori_loop(0, SEQ // rows, merge_body, 0)


def dilated_attention(b_qk, b_v, batch):
    hp = DIL_HEADS_PER_GROUP
    in_specs, args = [], []
    for g in range(len(DIL_GROUPS)):
        for arr, c0 in ((b_qk, 0), (b_qk, DIL_HEADS), (b_v, 0)):
            in_specs.append(pl.BlockSpec((SEQ, HEAD_DIM), lambda b, j, c0=c0, g=g: (b, c0 + g * hp + j)))
            args.append(arr)
    return pl.pallas_call(
        _dilated_kernel,
        grid=(batch, hp),
        in_specs=in_specs,
        out_specs=pl.BlockSpec((SEQ, HEAD_DIM), lambda b, j: (b, j)),
        out_shape=jax.ShapeDtypeStruct((batch * SEQ, hp * HEAD_DIM), BF16),
        scratch_shapes=[pltpu.VMEM((SEQ, HEAD_DIM), F32)] * 6,
        compiler_params=_params("parallel", "parallel"),
        name="dilated_attention",
    )(*args)


def _moba_kernel(q_ref, k_ref, vt_ref, o_ref, kmean_scr, bias_scr):
    i = pl.program_id(1)
    tq, nb, d = MOBA_BLOCK, MOBA_NB, HEAD_DIM
    heads = range(MOBA_HEADS)
    col = lambda h: slice(h * d, (h + 1) * d)

    @pl.when(i == 0)
    def _():
        avg = jnp.where(_iota((nb, SEQ), 1) // MOBA_BLOCK == _iota((nb, SEQ), 0), 1.0 / MOBA_BLOCK, 0.0)
        kmean_scr[...] = _dot(avg.astype(BF16), k_ref[...]).astype(kmean_scr.dtype)

    blk = _iota((nb, tq), 0)
    own = pl.multiple_of(i * tq, tq)
    causal = _iota((tq, tq), 0) <= _iota((tq, tq), 1)
    qs = [q_ref[:, col(h)] for h in heads]
    gates = [jnp.where(blk < i, _dot_nt(kmean_scr[:, col(h)], qs[h]), NEG_INF) for h in heads]
    ss = [_dot_nt(k_ref[pl.ds(own, tq), col(h)], qs[h]) for h in heads]
    for h in heads:
        gate = gates[h]
        rank = jnp.zeros((nb, tq), F32)
        for j in range(nb):
            gj = gate[j:j + 1, :]
            ahead = (gj > gate) | ((gj == gate) & (blk > j))
            rank = rank + jnp.where(ahead, 1.0, 0.0)
        bias_scr[h] = jnp.where((rank < float(MOBA_TOPK)) & (gate > NEG_INF * 0.5), 0.0, NEG_INF)
    ss = [jnp.where(causal, s, NEG_INF) for s in ss]
    ms = [jnp.max(s, axis=0, keepdims=True) for s in ss]
    ps = [jnp.exp2(s - m) for s, m in zip(ss, ms)]
    ls = [jnp.sum(p, axis=0, keepdims=True) for p in ps]
    accs = [_dot(vt_ref[col(h), pl.ds(own, tq)], ps[h].astype(BF16)) for h in heads]

    def body(kj, carry):
        off = pl.multiple_of(kj * tq, tq)
        bs = [bias_scr[h, pl.ds(kj, 1), :] for h in heads]
        ss = [_dot_nt(k_ref[pl.ds(off, tq), col(h)], qs[h]) for h in heads]
        m_new = [jnp.maximum(carry[h][0], jnp.max(ss[h], axis=0, keepdims=True) + bs[h]) for h in heads]
        alpha = [jnp.exp2(carry[h][0] - m_new[h]) for h in heads]
        ps = [jnp.exp2(ss[h] - (m_new[h] - bs[h])) for h in heads]
        ls = [alpha[h] * carry[h][1] + jnp.sum(ps[h], axis=0, keepdims=True) for h in heads]
        pvs = [_dot(vt_ref[col(h), pl.ds(off, tq)], ps[h].astype(BF16)) for h in heads]
        return tuple((m_new[h], ls[h], alpha[h] * carry[h][2] + pvs[h]) for h in heads)

    fin = lax.fori_loop(0, i, body, tuple((ms[h], ls[h], accs[h]) for h in heads))
    for h in heads:
        _, l, acc = fin[h]
        o_ref[:, col(h)] = jnp.transpose(acc / l).astype(o_ref.dtype)


def moba_attention(c_qk, vt, batch):
    tq = MOBA_BLOCK
    nq = SEQ // tq
    return pl.pallas_call(
        _moba_kernel,
        grid=(batch, nq),
        in_specs=[pl.BlockSpec((tq, C_QKV), lambda b, i: (b * nq + i, 0)),
                  pl.BlockSpec((SEQ, C_QKV), lambda b, i: (b, 1)),
                  pl.BlockSpec((C_QKV, SEQ), lambda b, i: (0, b))],
        out_specs=pl.BlockSpec((tq, C_QKV), lambda b, i: (b * nq + i, 0)),
        out_shape=jax.ShapeDtypeStruct((batch * SEQ, C_QKV), BF16),
        scratch_shapes=[pltpu.VMEM((MOBA_NB, C_QKV), BF16), pltpu.VMEM((MOBA_HEADS, MOBA_NB, tq), F32)],
        compiler_params=_params("parallel", "arbitrary"),
        name="moba_attention",
    )(c_qk, c_qk, vt)


def _merge_kernel(oa_ref, ob_ref, oc_ref, ga_ref, gb_ref, gc_ref, wa_ref, wb_ref, wc_ref, out_ref):
    y = ga_ref[...] * _dot(oa_ref[...], wa_ref[...])
    y = y + gb_ref[...] * _dot(ob_ref[...], wb_ref[...])
    y = y + gc_ref[...] * _dot(oc_ref[...], wc_ref[...])
    out_ref[...] = y.astype(out_ref.dtype)


def gated_merge(o_a, o_b, o_c, gates_m, w_a, w_b, w_c, layer, tm=1024, tn=512):
    m = o_a.shape[0]
    nb = D_MODEL // tn
    act = lambda width: pl.BlockSpec((tm, width), lambda j, i: (i, 0))
    gate = lambda g: pl.BlockSpec((tm, tn), lambda j, i, g=g: (i, g * nb + j))
    wgt = lambda width: pl.BlockSpec((None, width, tn), lambda j, i: (layer, 0, j))
    return pl.pallas_call(
        _merge_kernel,
        grid=(nb, m // tm),
        in_specs=[act(o_a.shape[1]), act(o_b.shape[1]), act(o_c.shape[1]),
                  gate(0), gate(1), gate(2),
                  wgt(w_a.shape[1]), wgt(w_b.shape[1]), wgt(w_c.shape[1])],
        out_specs=pl.BlockSpec((tm, tn), lambda j, i: (i, j)),
        out_shape=jax.ShapeDtypeStruct((m, D_MODEL), BF16),
        compiler_params=_params("parallel", "parallel"),
        name="gated_merge",
    )(o_a, o_b, o_c, gates_m, gates_m, gates_m, w_a, w_b, w_c)


def _out_proj_kernel(y_ref, w_ref, x_ref, g_ref, xo_ref, h_ref):
    x_new = x_ref[...] + _dot(y_ref[...], w_ref[...])
    xo_ref[...] = x_new
    h_ref[...] = _rmsnorm_rows(x_new, g_ref[...]).astype(h_ref.dtype)


def out_proj_residual_norm(y, w_o, layer, x2, g, tm=512):
    m, d = x2.shape
    row = pl.BlockSpec((tm, d), lambda i: (i, 0))
    return pl.pallas_call(
        _out_proj_kernel,
        grid=(m // tm,),
        in_specs=[row, pl.BlockSpec((None, d, d), lambda i: (layer, 0, 0)), row,
                  pl.BlockSpec((1, d), lambda i: (0, 0))],
        out_specs=[row, row],
        out_shape=[jax.ShapeDtypeStruct((m, d), F32), jax.ShapeDtypeStruct((m, d), BF16)],
        compiler_params=_params("parallel"),
        name="out_proj",
    )(y, w_o, x2, g.reshape(1, d))


def _mlp_kernel(h_ref, w1_ref, w2_ref, x_ref, g_ref, *rest):
    xo_ref = rest[0] if len(rest) == 3 else None
    hn_ref, acc_scr = rest[-2:]
    f = pl.program_id(1)

    @pl.when(f == 0)
    def _():
        acc_scr[...] = jnp.zeros_like(acc_scr)

    u = jnp.square(jnp.maximum(_dot(h_ref[...], w1_ref[...]), 0.0))
    acc_scr[...] += _dot(u.astype(BF16), w2_ref[...])

    @pl.when(f == pl.num_programs(1) - 1)
    def _():
        x_new = x_ref[...] + acc_scr[...]
        if xo_ref is not None:
            xo_ref[...] = x_new
        hn_ref[...] = _rmsnorm_rows(x_new, g_ref[...]).astype(hn_ref.dtype)


def mlp_residual_norm(h2, w1, w2, layer, x2, g_next, next_dtype, emit_x, tm=1024, tf=512):
    m, d = x2.shape
    row = pl.BlockSpec((tm, d), lambda i, f: (i, 0))
    row_once = pl.BlockSpec((tm, d), lambda i, f: (i, 0), pipeline_mode=pl.Buffered(1))
    out_shape = [jax.ShapeDtypeStruct((m, d), F32)] * emit_x + [jax.ShapeDtypeStruct((m, d), next_dtype)]
    res = pl.pallas_call(
        _mlp_kernel,
        grid=(m // tm, D_FF // tf),
        in_specs=[row, pl.BlockSpec((None, d, tf), lambda i, f: (layer, 0, f)),
                  pl.BlockSpec((None, tf, d), lambda i, f: (layer, f, 0)),
                  row_once, pl.BlockSpec((1, d), lambda i, f: (0, 0))],
        out_specs=[row_once] * len(out_shape),
        out_shape=out_shape,
        scratch_shapes=[pltpu.VMEM((tm, d), F32)],
        compiler_params=_params("parallel", "arbitrary"),
        name="mlp",
    )(h2, w1, w2, x2, g_next.reshape(1, d))
    return (res[0], res[1]) if emit_x else (None, res[0])


def _rope_tables():
    inv = ROPE_THETA ** (-jnp.arange(0, ROPE_DIM, 2, dtype=F32) / ROPE_DIM)
    ang = jnp.arange(SEQ, dtype=F32)[:, None] * inv[None, :]
    cos, sin = jnp.cos(ang), jnp.sin(ang)
    zeros = jnp.zeros((SEQ, HEAD_DIM - ROPE_DIM), F32)
    zero_h = jnp.zeros((SEQ, ROPE_HALF), F32)
    c = jnp.concatenate([cos, cos, jnp.ones_like(zeros)], axis=1)
    s_up = jnp.concatenate([-sin, zero_h, zeros], axis=1)
    s_dn = jnp.concatenate([zero_h, sin, zeros], axis=1)
    return c, s_up, s_dn


def _overlap_table_t():
    cs = np.arange(NSA_M_PAD)[None, :] * NSA_CMP_STRIDE
    bs = np.arange(NSA_NB)[:, None] * NSA_SEL_BLOCK
    ov = np.clip(np.minimum(cs + NSA_CMP_LEN, bs + NSA_SEL_BLOCK) - np.maximum(cs, bs), 0, None) / NSA_CMP_LEN
    ov[:, NSA_M_PAD - 1] = 0.0
    return jnp.asarray(ov, dtype=BF16)


W_IN_NAMES = ("a_q", "a_kc", "a_vc", "a_ks", "a_vs", "a_kw", "a_vw", "a_g",
              "b_q", "b_k", "b_v", "c_q", "c_k", "c_v", "m_a", "m_b", "m_c")
W_IN_START = dict(zip(W_IN_NAMES, np.cumsum((0,) + IN_SPLIT_SIZES[:-1]).tolist()))
W_IN_SIZE = dict(zip(W_IN_NAMES, IN_SPLIT_SIZES))


def _w_in_cols(w_in, name):
    return w_in[:, :, W_IN_START[name]:W_IN_START[name] + W_IN_SIZE[name]]


def _bf16_w_in(w_in):
    gate_at = W_IN_START["a_g"]
    lo = w_in[:, :, :gate_at].astype(BF16)
    hi = w_in[:, :, gate_at + A_G:].astype(BF16)
    per_group = 3 * NSA_REP
    a_g = _w_in_cols(w_in, "a_g")
    a_g = jnp.concatenate(
        [jnp.pad(a_g[:, :, g * per_group:(g + 1) * per_group], ((0, 0), (0, 0), (0, LANES - per_group)))
         for g in range(NSA_KV_HEADS)], axis=2).astype(BF16)
    where = {"a_g": (a_g, 0)}
    for n in W_IN_NAMES:
        if W_IN_START[n] < gate_at:
            where[n] = (lo, W_IN_START[n])
        elif W_IN_START[n] > gate_at:
            where[n] = (hi, W_IN_START[n] - gate_at - A_G)
    return where


def _scales(*widths_and_values):
    return jnp.concatenate([jnp.full((1, w), v, F32) for w, v in widths_and_values], axis=1)


def _layer(x2, h, batch, layer, tabs, overlap_t, w_at, wt_v, pe_k, w1_k, w2_k, pe_v, w1_v, w2_v,
           w_br_a, w_br_b, w_br_c, w_o, mlp_g, w_mlp_in, w_mlp_out, g_next, last):
    def proj(first, n, *args):
        piece, col0 = w_at[first]
        return project(h, piece, layer, col0, n, *args)

    q_raw, q_rot = proj("a_q", A_Q, "both", BF16, tabs, _scales((A_Q, Q_SCALE)))
    cmp_src = proj("a_kc", 2 * A_KV, "plain", F32)
    k_sel = proj("a_ks", A_KV, "rope", BF16, tabs, _scales((A_KV, 1.0)))
    k_win = proj("a_kw", A_KV, "rope", BF16, tabs, _scales((A_KV, 1.0)))
    b_qk = proj("b_q", 2 * B_QKV, "rope", F32, tabs, _scales((B_QKV, Q_SCALE), (B_QKV, 1.0)))
    b_v = proj("b_v", B_QKV, "plain", F32)
    c_qk = proj("c_q", 2 * C_QKV, "rope", BF16, tabs, _scales((C_QKV, Q_SCALE), (C_QKV, 1.0)))
    gates_m = proj("m_a", 3 * D_MODEL, "sigmoid", BF16)
    gates_a = proj("a_g", NSA_KV_HEADS * LANES, "sigmoid", F32)
    vt = project_transposed(h, wt_v, layer, BF16)

    flat = NSA_CMP_LEN * HEAD_DIM
    cmp_kv = nsa_compress(cmp_src, pe_k.reshape(1, flat), w1_k.reshape(flat, NSA_CMP_HIDDEN).astype(BF16),
                          w2_k.astype(BF16), pe_v.reshape(1, flat),
                          w1_v.reshape(flat, NSA_CMP_HIDDEN).astype(BF16), w2_v.astype(BF16), batch)
    o_a = nsa_attention(q_raw, q_rot, cmp_kv, k_sel, k_win, vt, gates_a, overlap_t, batch)
    o_b = dilated_attention(b_qk, b_v, batch)
    o_c = moba_attention(c_qk, vt, batch)

    merged = gated_merge(o_a, o_b, o_c, gates_m, w_br_a, w_br_b, w_br_c, layer)
    x2, h2 = out_proj_residual_norm(merged, w_o, layer, x2, mlp_g)
    return mlp_residual_norm(h2, w_mlp_in, w_mlp_out, layer, x2, g_next, F32 if last else BF16, emit_x=not last)


def kernel(x, attn_norm_g, w_in, cmp_pe_k, cmp_w1_k, cmp_w2_k, cmp_pe_v, cmp_w1_v, cmp_w2_v,
           w_br_a, w_br_b, w_br_c, w_o, mlp_norm_g, w_mlp_in, w_mlp_out, final_norm_g):
    batch, seq, d = x.shape
    assert seq == SEQ and d == D_MODEL
    depth = w_in.shape[0]
    tabs = _rope_tables()
    overlap_t = _overlap_table_t()
    w_at = _bf16_w_in(w_in)
    wt_v = jnp.transpose(jnp.concatenate([_w_in_cols(w_in, n) for n in ("c_v", "a_vs", "a_vw")], axis=2),
                         (0, 2, 1)).astype(BF16)
    w_br_a, w_br_b, w_br_c, w_o = (t.astype(BF16) for t in (w_br_a, w_br_b, w_br_c, w_o))
    w_mlp_in, w_mlp_out = w_mlp_in.astype(BF16), w_mlp_out.astype(BF16)
    x2 = x.reshape(batch * seq, d)
    h = rmsnorm(x2, attn_norm_g[0], BF16)
    for l in range(depth):
        last = l == depth - 1
        g_next = final_norm_g if last else attn_norm_g[l + 1]
        x2, h = _layer(x2, h, batch, l, tabs, overlap_t, w_at, wt_v,
                       cmp_pe_k[l], cmp_w1_k[l], cmp_w2_k[l], cmp_pe_v[l], cmp_w1_v[l], cmp_w2_v[l],
                       w_br_a, w_br_b, w_br_c, w_o, mlp_norm_g[l], w_mlp_in, w_mlp_out, g_next, last)
    return h.reshape(batch, seq, d)
```

```python
import functools
import math

import numpy as np
import jax
import jax.numpy as jnp
from jax import lax
from jax.experimental import pallas as pl
from jax.experimental.pallas import tpu as pltpu

D_MODEL = 2048
SEQ = 2048
HEAD_DIM = 128
ROPE_THETA = 500000.0
ROPE_DIM = HEAD_DIM // 4
ROPE_HALF = ROPE_DIM // 2
NORM_EPS = 1e-6
NEG_INF = -1e30
Q_SCALE = HEAD_DIM ** -0.5 * math.log2(math.e)

NSA_HEADS = 8
NSA_KV_HEADS = 2
NSA_REP = NSA_HEADS // NSA_KV_HEADS
NSA_CMP_LEN = 32
NSA_CMP_STRIDE = 16
NSA_CMP_HIDDEN = 256
NSA_SEL_BLOCK = 64
NSA_SEL_TOPN = 16
NSA_WINDOW = 512
NSA_FORCE_BONUS = 1e4
NSA_NB = SEQ // NSA_SEL_BLOCK
NSA_M_PAD = SEQ // NSA_CMP_STRIDE

DIL_GROUPS = ((128, 1), (512, 4), (2048, 16))
DIL_HEADS_PER_GROUP = 4
DIL_HEADS = DIL_HEADS_PER_GROUP * len(DIL_GROUPS)
DIL_TILE = 128
DIL_UNROLL = 8

MOBA_HEADS = 8
MOBA_BLOCK = 256
MOBA_TOPK = 3
MOBA_NB = SEQ // MOBA_BLOCK

D_FF = 4 * D_MODEL
A_Q = NSA_HEADS * HEAD_DIM
A_KV = NSA_KV_HEADS * HEAD_DIM
A_G = 3 * NSA_HEADS
B_QKV = DIL_HEADS * HEAD_DIM
C_QKV = MOBA_HEADS * HEAD_DIM
IN_SPLIT_SIZES = (A_Q, A_KV, A_KV, A_KV, A_KV, A_KV, A_KV, A_G,
                  B_QKV, B_QKV, B_QKV, C_QKV, C_QKV, C_QKV,
                  D_MODEL, D_MODEL, D_MODEL)

LANES = 128
PROJ_TN_MAX = 512
PROJ_VMEM_BUDGET = 46 * 1024 * 1024
VMEM_LIMIT = 56 * 1024 * 1024

BF16 = jnp.bfloat16
F32 = jnp.float32


def _params(*sem):
    return pltpu.CompilerParams(dimension_semantics=sem, vmem_limit_bytes=VMEM_LIMIT)


def _dot(a, b):
    return jnp.dot(a, b, preferred_element_type=F32)


def _dot_nt(a, b):
    return lax.dot_general(a, b, (((1,), (1,)), ((), ())), preferred_element_type=F32)


def _iota(shape, axis):
    return lax.broadcasted_iota(jnp.int32, shape, axis)


def _rmsnorm_rows(x, g):
    y = x * lax.rsqrt(jnp.mean(x * x, axis=-1, keepdims=True) + NORM_EPS)
    return y * g


def _rmsnorm_kernel(x_ref, g_ref, h_ref):
    h_ref[...] = _rmsnorm_rows(x_ref[...], g_ref[...]).astype(h_ref.dtype)


def rmsnorm(x2, g, out_dtype, tm=512):
    m, d = x2.shape
    return pl.pallas_call(
        _rmsnorm_kernel,
        grid=(m // tm,),
        in_specs=[pl.BlockSpec((tm, d), lambda i: (i, 0)), pl.BlockSpec((1, d), lambda i: (0, 0))],
        out_specs=pl.BlockSpec((tm, d), lambda i: (i, 0)),
        out_shape=jax.ShapeDtypeStruct((m, d), out_dtype),
        compiler_params=_params("parallel"),
        name="rmsnorm",
    )(x2, g.reshape(1, d))


def _rope_lanes(acc, c, s_up, s_dn):
    tn = acc.shape[1]
    reps = tn // HEAD_DIM
    if reps > 1:
        c = jnp.concatenate([c] * reps, axis=1)
        s_up = jnp.concatenate([s_up] * reps, axis=1)
        s_dn = jnp.concatenate([s_dn] * reps, axis=1)
    up = pltpu.roll(acc, tn - ROPE_HALF, axis=1)
    dn = pltpu.roll(acc, ROPE_HALF, axis=1)
    return acc * c + up * s_up + dn * s_dn


def _proj_kernel(*refs, mode):
    if mode in ("rope", "both"):
        h_ref, w_ref, c_ref, su_ref, sd_ref, cs_ref = refs[:6]
        outs = refs[6:]
    else:
        h_ref, w_ref = refs[:2]
        outs = refs[2:]
    acc = _dot(h_ref[...], w_ref[...])
    if mode == "plain":
        outs[0][...] = acc.astype(outs[0].dtype)
    elif mode == "sigmoid":
        outs[0][...] = jax.nn.sigmoid(acc).astype(outs[0].dtype)
    else:
        col_scale = cs_ref[...]
        roped = _rope_lanes(acc, c_ref[...], su_ref[...], sd_ref[...]) * col_scale
        if mode == "both":
            outs[0][...] = (acc * col_scale).astype(outs[0].dtype)
            outs[1][...] = roped.astype(outs[1].dtype)
        else:
            outs[0][...] = roped.astype(outs[0].dtype)


def _proj_tiles(k, n, col0, out_bytes, n_out, rope):
    tn = math.gcd(math.gcd(PROJ_TN_MAX, n), col0) if col0 else math.gcd(PROJ_TN_MAX, n)
    for tm in (SEQ, SEQ // 2):
        blocks = tm * k * 2 + k * tn * 2 + n_out * tm * tn * out_bytes + (3 * tm * HEAD_DIM * 4 if rope else 0)
        if 2 * blocks <= PROJ_VMEM_BUDGET:
            break
    return tm, tn


def project(h, w_pad, layer, col0, n, mode, out_dtype, rope_tabs=None, col_scale=None):
    m, k = h.shape
    n_out = 2 if mode == "both" else 1
    rope = mode in ("rope", "both")
    tm, tn = _proj_tiles(k, n, col0, jnp.dtype(out_dtype).itemsize, n_out, rope)
    assert m % tm == 0 and n % tn == 0 and col0 % tn == 0 and SEQ % tm == 0
    j0 = col0 // tn
    in_specs = [pl.BlockSpec((tm, k), lambda i, j: (i, 0)),
                pl.BlockSpec((None, k, tn), lambda i, j: (layer, 0, j0 + j))]
    args = [h, w_pad]
    if rope:
        pos_blocks = SEQ // tm
        for t in rope_tabs:
            in_specs.append(pl.BlockSpec((tm, HEAD_DIM), lambda i, j: (i % pos_blocks, 0)))
            args.append(t)
        in_specs.append(pl.BlockSpec((1, tn), lambda i, j: (0, j)))
        args.append(col_scale)
    out_spec = pl.BlockSpec((tm, tn), lambda i, j: (i, j))
    out_shape = jax.ShapeDtypeStruct((m, n), out_dtype)
    res = pl.pallas_call(
        functools.partial(_proj_kernel, mode=mode),
        grid=(m // tm, n // tn),
        in_specs=in_specs,
        out_specs=[out_spec] * n_out,
        out_shape=[out_shape] * n_out,
        compiler_params=_params("parallel", "parallel"),
        name="proj_" + mode,
    )(*args)
    return res if n_out == 2 else res[0]


def _proj_t_kernel(wt_ref, h_ref, out_ref):
    out_ref[...] = _dot_nt(wt_ref[...], h_ref[...]).astype(out_ref.dtype)


def project_transposed(h, wt, layer, out_dtype, tm=1024, tn=512):
    m, k = h.shape
    n = wt.shape[1]
    return pl.pallas_call(
        _proj_t_kernel,
        grid=(n // tn, m // tm),
        in_specs=[pl.BlockSpec((None, tn, k), lambda j, i: (layer, j, 0)),
                  pl.BlockSpec((tm, k), lambda j, i: (i, 0))],
        out_specs=pl.BlockSpec((tn, tm), lambda j, i: (j, i)),
        out_shape=jax.ShapeDtypeStruct((n, m), out_dtype),
        compiler_params=_params("parallel", "parallel"),
        name="proj_transposed",
    )(wt, h)


def _compress_kernel(k0_ref, k1_ref, v0_ref, v1_ref, pek_ref, w1k_ref, w2k_ref, pev_ref, w1v_ref, w2v_ref,
                     out_ref):
    half = NSA_CMP_STRIDE * HEAD_DIM
    for idx, src_ref in enumerate((k0_ref, k1_ref, v0_ref, v1_ref)):
        is_k = idx < NSA_KV_HEADS
        pe_ref, w1_ref, w2_ref = (pek_ref, w1k_ref, w2k_ref) if is_k else (pev_ref, w1v_ref, w2v_ref)
        x = jnp.concatenate(
            [src_ref[pl.ds(l, NSA_M_PAD, stride=NSA_CMP_STRIDE), :]
             for l in range(NSA_CMP_STRIDE)], axis=1)
        pe = pe_ref[...]
        first = _dot((x + pe[:, :half]).astype(BF16), w1_ref[:half, :])
        second = _dot((x + pe[:, half:]).astype(BF16), w1_ref[half:, :])
        hid = jax.nn.gelu(first + pltpu.roll(second, NSA_M_PAD - 1, axis=0))
        out = _dot(hid.astype(BF16), w2_ref[...])
        out_ref[idx] = (out if is_k else jnp.transpose(out)).astype(out_ref.dtype)


def nsa_compress(pf, pe_k, w1_k, w2_k, pe_v, w1_v, w2_v, batch):
    flat = NSA_CMP_LEN * HEAD_DIM
    const = lambda shape: pl.BlockSpec(shape, lambda b: (0,) * len(shape))
    return pl.pallas_call(
        _compress_kernel,
        grid=(batch,),
        in_specs=[pl.BlockSpec((SEQ, HEAD_DIM), lambda b, c=c: (b, c)) for c in range(4)] + [
                  const((1, flat)), const((flat, NSA_CMP_HIDDEN)), const((NSA_CMP_HIDDEN, HEAD_DIM)),
                  const((1, flat)), const((flat, NSA_CMP_HIDDEN)), const((NSA_CMP_HIDDEN, HEAD_DIM))],
        out_specs=pl.BlockSpec((None, 4, NSA_M_PAD, HEAD_DIM), lambda b: (b, 0, 0, 0)),
        out_shape=jax.ShapeDtypeStruct((batch, 4, NSA_M_PAD, HEAD_DIM), BF16),
        compiler_params=_params("parallel"),
        name="nsa_compress",
    )(pf, pf, pf, pf, pe_k, w1_k, w2_k, pe_v, w1_v, w2_v)


NSA_TQ = 256
NSA_KC = 256
NSA_BLK_PER_CHUNK = NSA_KC // NSA_SEL_BLOCK


def _nsa_kernel(qraw_ref, qrot_ref, cmp_ref, ksel_ref, vselt_ref, kwin_ref, vwint_ref,
                gate_ref, ovt_ref, o_ref, bias_scr):
    i = pl.program_id(1)
    tq, kc_w, rep, d = NSA_TQ, NSA_KC, NSA_REP, HEAD_DIM
    groups = range(NSA_KV_HEADS)
    width = rep * tq
    bpc = NSA_BLK_PER_CHUNK
    t0 = i * tq
    head_cols = lambda g, r: slice((g * rep + r) * d, (g * rep + r + 1) * d)
    stack = lambda ref, g: jnp.concatenate([ref[:, head_cols(g, r)] for r in range(rep)], axis=0)
    q_raw = [stack(qraw_ref, g) for g in groups]
    q_rot = [stack(qrot_ref, g) for g in groups]
    k_of = lambda ref, g, kj: ref[pl.ds(pl.multiple_of(kj * kc_w, kc_w), kc_w), g * d:(g + 1) * d]
    vt_of = lambda ref, g, kj: ref[g * d:(g + 1) * d, pl.ds(pl.multiple_of(kj * kc_w, kc_w), kc_w)]
    col_max = lambda s: jnp.max(s, axis=0, keepdims=True)
    col_sum = lambda p: jnp.sum(p, axis=0, keepdims=True)

    q_pos = t0 + (_iota((NSA_M_PAD, width), 1) & (tq - 1))
    vis = (_iota((NSA_M_PAD, width), 0) * NSA_CMP_STRIDE + (NSA_CMP_LEN - 1)) <= q_pos
    sc = [jnp.where(vis, _dot_nt(cmp_ref[g], q_raw[g]), NEG_INF) for g in groups]
    ec = [jnp.where(vis, jnp.exp2(s - col_max(s)), 0.0) for s in sc]
    pc = [(e / jnp.maximum(col_sum(e), 1e-30)).astype(BF16) for e in ec]
    o_cmp = [_dot(cmp_ref[NSA_KV_HEADS + g], pc[g]) for g in groups]
    imp_heads = [_dot(ovt_ref[...], p) for p in pc]

    blk = _iota((NSA_NB, tq), 0)
    q_blk = (t0 + _iota((NSA_NB, tq), 1)) // NSA_SEL_BLOCK
    forced = (blk == 0) | (blk == q_blk) | (blk == q_blk - 1)
    for g in groups:
        imp = imp_heads[g][:, :tq]
        for r in range(1, rep):
            imp = imp + imp_heads[g][:, r * tq:(r + 1) * tq]
        score = jnp.where(blk <= q_blk, imp + jnp.where(forced, NSA_FORCE_BONUS, 0.0), NEG_INF)
        rank = jnp.zeros((NSA_NB, tq), F32)
        for j in range(NSA_NB):
            sj = score[j:j + 1, :]
            ahead = (sj > score) | ((sj == score) & (blk > j))
            rank = rank + jnp.where(ahead, 1.0, 0.0)
        bias = jnp.where((rank < float(NSA_SEL_TOPN)) & (score > NEG_INF * 0.5), 0.0, NEG_INF)
        bias = jnp.concatenate([bias] * rep, axis=1)
        for c in range(SEQ // kc_w):
            bias_scr[g, c] = bias[c * bpc:(c + 1) * bpc, :]

    def sel_step(kj, ss, carry):
        bs = [bias_scr[g, kj] for g in groups]
        s3 = [s.reshape(bpc, NSA_SEL_BLOCK, width) for s in ss]
        m_new = [jnp.maximum(carry[g][0], col_max(jnp.max(s3[g], axis=1) + bs[g])) for g in groups]
        alpha = [jnp.exp2(carry[g][0] - m_new[g]) for g in groups]
        ps = [jnp.exp2(s3[g] - (m_new[g] - bs[g])[:, None, :]).reshape(kc_w, width) for g in groups]
        ls = [alpha[g] * carry[g][1] + col_sum(ps[g]) for g in groups]
        pvs = [_dot(vt_of(vselt_ref, g, kj), ps[g].astype(BF16)) for g in groups]
        return tuple((m_new[g], ls[g], alpha[g] * carry[g][2] + pvs[g]) for g in groups)

    init = tuple((jnp.full((1, width), NEG_INF, F32), jnp.zeros((1, width), F32), jnp.zeros((d, width), F32))
                 for g in groups)
    carry = lax.fori_loop(
        0, i, lambda kj, c: sel_step(kj, [_dot_nt(k_of(ksel_ref, g, kj), q_rot[g]) for g in groups], c), init)

    key_row = _iota((kc_w, width), 0)
    q_col = _iota((kc_w, width), 1) & (tq - 1)
    causal = key_row <= q_col
    diag = [jnp.where(causal, _dot_nt(k_of(ksel_ref, g, i), q_rot[g]), NEG_INF) for g in groups]
    sel = sel_step(i, diag, carry)

    far = i - 2
    near = i - 1
    far_ok = jnp.where(far >= 0, 0.0, NEG_INF)
    near_ok = jnp.where(near >= 0, 0.0, NEG_INF)
    far_c = jnp.maximum(far, 0)
    near_c = jnp.maximum(near, 0)
    s_own = [jnp.where(causal, _dot_nt(k_of(kwin_ref, g, i), q_rot[g]), NEG_INF) for g in groups]
    s_near = [_dot_nt(k_of(kwin_ref, g, near_c), q_rot[g]) + near_ok for g in groups]
    s_far = [jnp.where(key_row > q_col, _dot_nt(k_of(kwin_ref, g, far_c), q_rot[g]), NEG_INF) + far_ok
             for g in groups]
    m_w = [jnp.maximum(jnp.maximum(col_max(s_own[g]), col_max(s_near[g])), col_max(s_far[g])) for g in groups]
    p_own = [jnp.exp2(s_own[g] - m_w[g]) for g in groups]
    p_near = [jnp.exp2(s_near[g] - m_w[g]) for g in groups]
    p_far = [jnp.exp2(s_far[g] - m_w[g]) for g in groups]
    l_win = [col_sum(p_own[g]) + col_sum(p_near[g]) + col_sum(p_far[g]) for g in groups]
    acc_win = [_dot(vt_of(vwint_ref, g, i), p_own[g].astype(BF16))
               + _dot(vt_of(vwint_ref, g, near_c), p_near[g].astype(BF16))
               + _dot(vt_of(vwint_ref, g, far_c), p_far[g].astype(BF16)) for g in groups]

    gates = jnp.transpose(gate_ref[...])
    for g in groups:
        o_sel = sel[g][2] / sel[g][1]
        o_win = acc_win[g] / l_win[g]
        for r in range(rep):
            lanes = slice(r * tq, (r + 1) * tq)
            row = 3 * (g * rep + r)
            o = (gates[row:row + 1, :] * o_cmp[g][:, lanes] + gates[row + 1:row + 2, :] * o_sel[:, lanes]
                 + gates[row + 2:row + 3, :] * o_win[:, lanes])
            o_ref[:, head_cols(g, r)] = jnp.transpose(o).astype(o_ref.dtype)


def nsa_attention(q_raw, q_rot, cmp_kv, k_sel, k_win, vt, gates_a, overlap_t, batch):
    tq = NSA_TQ
    nq = SEQ // tq
    g_n = NSA_KV_HEADS
    row = lambda b, i: (b * nq + i, 0)
    vt_r0 = C_QKV // A_KV
    return pl.pallas_call(
        _nsa_kernel,
        grid=(batch, nq),
        in_specs=[pl.BlockSpec((tq, A_Q), row),
                  pl.BlockSpec((tq, A_Q), row),
                  pl.BlockSpec((None, 2 * g_n, NSA_M_PAD, HEAD_DIM), lambda b, i: (b, 0, 0, 0)),
                  pl.BlockSpec((SEQ, A_KV), lambda b, i: (b, 0)),
                  pl.BlockSpec((A_KV, SEQ), lambda b, i: (vt_r0, b)),
                  pl.BlockSpec((SEQ, A_KV), lambda b, i: (b, 0)),
                  pl.BlockSpec((A_KV, SEQ), lambda b, i: (vt_r0 + 1, b)),
                  pl.BlockSpec((tq, LANES), row),
                  pl.BlockSpec((NSA_NB, NSA_M_PAD), lambda b, i: (0, 0))],
        out_specs=pl.BlockSpec((tq, A_Q), row),
        out_shape=jax.ShapeDtypeStruct((batch * SEQ, A_Q), BF16),
        scratch_shapes=[pltpu.VMEM((g_n, SEQ // NSA_KC, NSA_BLK_PER_CHUNK, NSA_REP * tq), F32)],
        compiler_params=_params("parallel", "parallel"),
        name="nsa_attention",
    )(q_raw, q_rot, cmp_kv, k_sel, vt, k_win, vt, gates_a, overlap_t)


def _dilated_kernel(q0, k0, v0, q1, k1, v1, q2, k2, v2, o_ref,
                    o0_scr, o1_scr, o2_scr, l0_scr, l1_scr, l2_scr):
    tile = DIL_TILE
    groups = ((q0, k0, v0, o0_scr, l0_scr), (q1, k1, v1, o1_scr, l1_scr), (q2, k2, v2, o2_scr, l2_scr))
    for (window, dil), (q_ref, k_ref, v_ref, og_scr, lg_scr) in zip(DIL_GROUPS, groups):
        assert window // dil == tile
        per_class = SEQ // dil
        tiles_per_class = per_class // tile
        nk = tile if tiles_per_class == 1 else 2 * tile
        a_minus_a = _iota((tile, nk), 0) - _iota((tile, nk), 1)

        def step(u, carry, q_ref=q_ref, k_ref=k_ref, v_ref=v_ref, og_scr=og_scr, lg_scr=lg_scr,
                 dil=dil, tiles_per_class=tiles_per_class, nk=nk, a_minus_a=a_minus_a):
            ts = [u * DIL_UNROLL + a for a in range(DIL_UNROLL)]
            cls = [t // tiles_per_class for t in ts]
            n0 = [(t % tiles_per_class) * tile for t in ts]
            kbase = [jnp.maximum(n - (nk - tile), 0) for n in n0]
            q_rows = [pl.ds(c + dil * n, tile, stride=dil) for c, n in zip(cls, n0)]
            k_rows = [pl.ds(c + dil * kb, nk, stride=dil) for c, kb in zip(cls, kbase)]
            qs = [q_ref[r, :].astype(BF16) for r in q_rows]
            ks = [k_ref[r, :].astype(BF16) for r in k_rows]
            vs = [v_ref[r, :].astype(BF16) for r in k_rows]
            ss = [_dot_nt(q, k) for q, k in zip(qs, ks)]
            masks = []
            for n, kb in zip(n0, kbase):
                dist = a_minus_a + (n - kb)
                masks.append((dist >= 0) & (dist <= tile))
            ss = [jnp.where(mk, s, NEG_INF) for mk, s in zip(masks, ss)]
            ms = [jnp.max(s, axis=-1, keepdims=True) for s in ss]
            es = [jnp.where(mk, jnp.exp2(s - m), 0.0) for mk, s, m in zip(masks, ss, ms)]
            dens = [jnp.maximum(jnp.sum(e, axis=-1, keepdims=True), 1e-30) for e in es]
            os_ = [_dot((e / den).astype(BF16), v) for e, den, v in zip(es, dens, vs)]
            for r, o, m, den in zip(q_rows, os_, ms, dens):
                og_scr[r, :] = o
                lg_scr[r, :] = jnp.broadcast_to(m + jnp.log2(den), (tile, HEAD_DIM))
            return carry

        lax.fori_loop(0, SEQ // tile // DIL_UNROLL, step, 0)

    rows = 256

    def merge_body(c, carry):
        sl = pl.ds(pl.multiple_of(c * rows, rows), rows)
        la, lb, lc = l0_scr[sl, :], l1_scr[sl, :], l2_scr[sl, :]
        mx = jnp.maximum(jnp.maximum(la, lb), lc)
        ea, eb, ec = jnp.exp2(la - mx), jnp.exp2(lb - mx), jnp.exp2(lc - mx)
        tot = ea + eb + ec
        out = (ea / tot) * o0_scr[sl, :] + (eb / tot) * o1_scr[sl, :] + (ec / tot) * o2_scr[sl, :]
        o_ref[sl, :] = out.astype(o_ref.dtype)
        return carry

    lax.fori_loop(0, SEQ // rows, merge_body, 0)


def dilated_attention(b_qk, b_v, batch):
    hp = DIL_HEADS_PER_GROUP
    in_specs, args = [], []
    for g in range(len(DIL_GROUPS)):
        for arr, c0 in ((b_qk, 0), (b_qk, DIL_HEADS), (b_v, 0)):
            in_specs.append(pl.BlockSpec((SEQ, HEAD_DIM), lambda b, j, c0=c0, g=g: (b, c0 + g * hp + j)))
            args.append(arr)
    return pl.pallas_call(
        _dilated_kernel,
        grid=(batch, hp),
        in_specs=in_specs,
        out_specs=pl.BlockSpec((SEQ, HEAD_DIM), lambda b, j: (b, j)),
        out_shape=jax.ShapeDtypeStruct((batch * SEQ, hp * HEAD_DIM), BF16),
        scratch_shapes=[pltpu.VMEM((SEQ, HEAD_DIM), F32)] * 6,
        compiler_params=_params("parallel", "parallel"),
        name="dilated_attention",
    )(*args)


def _moba_kernel(q_ref, k_ref, vt_ref, o_ref, kmean_scr, bias_scr):
    i = pl.program_id(1)
    tq, nb, d = MOBA_BLOCK, MOBA_NB, HEAD_DIM
    heads = range(MOBA_HEADS)
    col = lambda h: slice(h * d, (h + 1) * d)

    @pl.when(i == 0)
    def _():
        avg = jnp.where(_iota((nb, SEQ), 1) // MOBA_BLOCK == _iota((nb, SEQ), 0), 1.0 / MOBA_BLOCK, 0.0)
        kmean_scr[...] = _dot(avg.astype(BF16), k_ref[...]).astype(kmean_scr.dtype)

    blk = _iota((nb, tq), 0)
    own = pl.multiple_of(i * tq, tq)
    causal = _iota((tq, tq), 0) <= _iota((tq, tq), 1)
    qs = [q_ref[:, col(h)] for h in heads]
    gates = [jnp.where(blk < i, _dot_nt(kmean_scr[:, col(h)], qs[h]), NEG_INF) for h in heads]
    ss = [_dot_nt(k_ref[pl.ds(own, tq), col(h)], qs[h]) for h in heads]
    for h in heads:
        gate = gates[h]
        rank = jnp.zeros((nb, tq), F32)
        for j in range(nb):
            gj = gate[j:j + 1, :]
            ahead = (gj > gate) | ((gj == gate) & (blk > j))
            rank = rank + jnp.where(ahead, 1.0, 0.0)
        bias_scr[h] = jnp.where((rank < float(MOBA_TOPK)) & (gate > NEG_INF * 0.5), 0.0, NEG_INF)
    ss = [jnp.where(causal, s, NEG_INF) for s in ss]
    ms = [jnp.max(s, axis=0, keepdims=True) for s in ss]
    ps = [jnp.exp2(s - m) for s, m in zip(ss, ms)]
    ls = [jnp.sum(p, axis=0, keepdims=True) for p in ps]
    accs = [_dot(vt_ref[col(h), pl.ds(own, tq)], ps[h].astype(BF16)) for h in heads]

    def body(kj, carry):
        off = pl.multiple_of(kj * tq, tq)
        bs = [bias_scr[h, pl.ds(kj, 1), :] for h in heads]
        ss = [_dot_nt(k_ref[pl.ds(off, tq), col(h)], qs[h]) for h in heads]
        m_new = [jnp.maximum(carry[h][0], jnp.max(ss[h], axis=0, keepdims=True) + bs[h]) for h in heads]
        alpha = [jnp.exp2(carry[h][0] - m_new[h]) for h in heads]
        ps = [jnp.exp2(ss[h] - (m_new[h] - bs[h])) for h in heads]
        ls = [alpha[h] * carry[h][1] + jnp.sum(ps[h], axis=0, keepdims=True) for h in heads]
        pvs = [_dot(vt_ref[col(h), pl.ds(off, tq)], ps[h].astype(BF16)) for h in heads]
        return tuple((m_new[h], ls[h], alpha[h] * carry[h][2] + pvs[h]) for h in heads)

    fin = lax.fori_loop(0, i, body, tuple((ms[h], ls[h], accs[h]) for h in heads))
    for h in heads:
        _, l, acc = fin[h]
        o_ref[:, col(h)] = jnp.transpose(acc / l).astype(o_ref.dtype)


def moba_attention(c_qk, vt, batch):
    tq = MOBA_BLOCK
    nq = SEQ // tq
    return pl.pallas_call(
        _moba_kernel,
        grid=(batch, nq),
        in_specs=[pl.BlockSpec((tq, C_QKV), lambda b, i: (b * nq + i, 0)),
                  pl.BlockSpec((SEQ, C_QKV), lambda b, i: (b, 1)),
                  pl.BlockSpec((C_QKV, SEQ), lambda b, i: (0, b))],
        out_specs=pl.BlockSpec((tq, C_QKV), lambda b, i: (b * nq + i, 0)),
        out_shape=jax.ShapeDtypeStruct((batch * SEQ, C_QKV), BF16),
        scratch_shapes=[pltpu.VMEM((MOBA_NB, C_QKV), BF16), pltpu.VMEM((MOBA_HEADS, MOBA_NB, tq), F32)],
        compiler_params=_params("parallel", "arbitrary"),
        name="moba_attention",
    )(c_qk, c_qk, vt)


def _merge_kernel(oa_ref, ob_ref, oc_ref, ga_ref, gb_ref, gc_ref, wa_ref, wb_ref, wc_ref, out_ref):
    y = ga_ref[...] * _dot(oa_ref[...], wa_ref[...])
    y = y + gb_ref[...] * _dot(ob_ref[...], wb_ref[...])
    y = y + gc_ref[...] * _dot(oc_ref[...], wc_ref[...])
    out_ref[...] = y.astype(out_ref.dtype)


def gated_merge(o_a, o_b, o_c, gates_m, w_a, w_b, w_c, layer, tm=1024, tn=512):
    m = o_a.shape[0]
    nb = D_MODEL // tn
    act = lambda width: pl.BlockSpec((tm, width), lambda j, i: (i, 0))
    gate = lambda g: pl.BlockSpec((tm, tn), lambda j, i, g=g: (i, g * nb + j))
    wgt = lambda width: pl.BlockSpec((None, width, tn), lambda j, i: (layer, 0, j))
    return pl.pallas_call(
        _merge_kernel,
        grid=(nb, m // tm),
        in_specs=[act(o_a.shape[1]), act(o_b.shape[1]), act(o_c.shape[1]),
                  gate(0), gate(1), gate(2),
                  wgt(w_a.shape[1]), wgt(w_b.shape[1]), wgt(w_c.shape[1])],
        out_specs=pl.BlockSpec((tm, tn), lambda j, i: (i, j)),
        out_shape=jax.ShapeDtypeStruct((m, D_MODEL), BF16),
        compiler_params=_params("parallel", "parallel"),
        name="gated_merge",
    )(o_a, o_b, o_c, gates_m, gates_m, gates_m, w_a, w_b, w_c)


def _out_proj_kernel(y_ref, w_ref, x_ref, g_ref, xo_ref, h_ref):
    x_new = x_ref[...] + _dot(y_ref[...], w_ref[...])
    xo_ref[...] = x_new
    h_ref[...] = _rmsnorm_rows(x_new, g_ref[...]).astype(h_ref.dtype)


def out_proj_residual_norm(y, w_o, layer, x2, g, tm=512):
    m, d = x2.shape
    row = pl.BlockSpec((tm, d), lambda i: (i, 0))
    return pl.pallas_call(
        _out_proj_kernel,
        grid=(m // tm,),
        in_specs=[row, pl.BlockSpec((None, d, d), lambda i: (layer, 0, 0)), row,
                  pl.BlockSpec((1, d), lambda i: (0, 0))],
        out_specs=[row, row],
        out_shape=[jax.ShapeDtypeStruct((m, d), F32), jax.ShapeDtypeStruct((m, d), BF16)],
        compiler_params=_params("parallel"),
        name="out_proj",
    )(y, w_o, x2, g.reshape(1, d))


def _mlp_kernel(h_ref, w1_ref, w2_ref, x_ref, g_ref, *rest):
    xo_ref = rest[0] if len(rest) == 3 else None
    hn_ref, acc_scr = rest[-2:]
    f = pl.program_id(1)

    @pl.when(f == 0)
    def _():
        acc_scr[...] = jnp.zeros_like(acc_scr)

    u = jnp.square(jnp.maximum(_dot(h_ref[...], w1_ref[...]), 0.0))
    acc_scr[...] += _dot(u.astype(BF16), w2_ref[...])

    @pl.when(f == pl.num_programs(1) - 1)
    def _():
        x_new = x_ref[...] + acc_scr[...]
        if xo_ref is not None:
            xo_ref[...] = x_new
        hn_ref[...] = _rmsnorm_rows(x_new, g_ref[...]).astype(hn_ref.dtype)


def mlp_residual_norm(h2, w1, w2, layer, x2, g_next, next_dtype, emit_x, tm=1024, tf=512):
    m, d = x2.shape
    row = pl.BlockSpec((tm, d), lambda i, f: (i, 0))
    row_once = pl.BlockSpec((tm, d), lambda i, f: (i, 0), pipeline_mode=pl.Buffered(1))
    out_shape = [jax.ShapeDtypeStruct((m, d), F32)] * emit_x + [jax.ShapeDtypeStruct((m, d), next_dtype)]
    res = pl.pallas_call(
        _mlp_kernel,
        grid=(m // tm, D_FF // tf),
        in_specs=[row, pl.BlockSpec((None, d, tf), lambda i, f: (layer, 0, f)),
                  pl.BlockSpec((None, tf, d), lambda i, f: (layer, f, 0)),
                  row_once, pl.BlockSpec((1, d), lambda i, f: (0, 0))],
        out_specs=[row_once] * len(out_shape),
        out_shape=out_shape,
        scratch_shapes=[pltpu.VMEM((tm, d), F32)],
        compiler_params=_params("parallel", "arbitrary"),
        name="mlp",
    )(h2, w1, w2, x2, g_next.reshape(1, d))
    return (res[0], res[1]) if emit_x else (None, res[0])


def _rope_tables():
    inv = ROPE_THETA ** (-jnp.arange(0, ROPE_DIM, 2, dtype=F32) / ROPE_DIM)
    ang = jnp.arange(SEQ, dtype=F32)[:, None] * inv[None, :]
    cos, sin = jnp.cos(ang), jnp.sin(ang)
    zeros = jnp.zeros((SEQ, HEAD_DIM - ROPE_DIM), F32)
    zero_h = jnp.zeros((SEQ, ROPE_HALF), F32)
    c = jnp.concatenate([cos, cos, jnp.ones_like(zeros)], axis=1)
    s_up = jnp.concatenate([-sin, zero_h, zeros], axis=1)
    s_dn = jnp.concatenate([zero_h, sin, zeros], axis=1)
    return c, s_up, s_dn


def _overlap_table_t():
    cs = np.arange(NSA_M_PAD)[None, :] * NSA_CMP_STRIDE
    bs = np.arange(NSA_NB)[:, None] * NSA_SEL_BLOCK
    ov = np.clip(np.minimum(cs + NSA_CMP_LEN, bs + NSA_SEL_BLOCK) - np.maximum(cs, bs), 0, None) / NSA_CMP_LEN
    ov[:, NSA_M_PAD - 1] = 0.0
    return jnp.asarray(ov, dtype=BF16)


W_IN_NAMES = ("a_q", "a_kc", "a_vc", "a_ks", "a_vs", "a_kw", "a_vw", "a_g",
              "b_q", "b_k", "b_v", "c_q", "c_k", "c_v", "m_a", "m_b", "m_c")
W_IN_START = dict(zip(W_IN_NAMES, np.cumsum((0,) + IN_SPLIT_SIZES[:-1]).tolist()))
W_IN_SIZE = dict(zip(W_IN_NAMES, IN_SPLIT_SIZES))


W_GATE_AT = W_IN_START["a_g"]
W_V_NAMES = ("c_v", "a_vs", "a_vw")


def _repack_kernel(w_ref, lo_ref, hi_ref, ag_ref, vt_ref):
    x = w_ref[...]
    lo_ref[...] = x[:, :W_GATE_AT].astype(BF16)
    hi_ref[...] = x[:, W_GATE_AT + A_G:].astype(BF16)
    g = x[:, W_GATE_AT:W_GATE_AT + LANES]
    ag_ref[...] = jnp.where(_iota(g.shape, 1) < A_G, g, 0.0).astype(BF16)
    v = jnp.concatenate([x[:, W_IN_START[n]:W_IN_START[n] + W_IN_SIZE[n]] for n in W_V_NAMES], axis=1)
    vt_ref[...] = jnp.transpose(v).astype(BF16)


def repack_w_in(w_in, tr=128):
    depth, k, n = w_in.shape
    hi_w = n - W_GATE_AT - A_G
    v_w = sum(W_IN_SIZE[name] for name in W_V_NAMES)
    rows = lambda width: pl.BlockSpec((None, tr, width), lambda l, r: (l, r, 0))
    lo, hi, ag, wt_v = pl.pallas_call(
        _repack_kernel,
        grid=(depth, k // tr),
        in_specs=[rows(n)],
        out_specs=[rows(W_GATE_AT), rows(hi_w), rows(LANES), pl.BlockSpec((None, v_w, tr), lambda l, r: (l, 0, r))],
        out_shape=[jax.ShapeDtypeStruct((depth, k, W_GATE_AT), BF16), jax.ShapeDtypeStruct((depth, k, hi_w), BF16),
                   jax.ShapeDtypeStruct((depth, k, LANES), BF16), jax.ShapeDtypeStruct((depth, v_w, k), BF16)],
        compiler_params=_params("parallel", "parallel"),
        name="repack_w_in",
    )(w_in)
    where = {"a_g": (ag, 0)}
    for name in W_IN_NAMES:
        if W_IN_START[name] < W_GATE_AT:
            where[name] = (lo, W_IN_START[name])
        elif W_IN_START[name] > W_GATE_AT:
            where[name] = (hi, W_IN_START[name] - W_GATE_AT - A_G)
    return where, wt_v


def _scales(*widths_and_values):
    return jnp.concatenate([jnp.full((1, w), v, F32) for w, v in widths_and_values], axis=1)


def _layer(x2, h, batch, layer, tabs, overlap_t, w_at, wt_v, pe_k, w1_k, w2_k, pe_v, w1_v, w2_v,
           w_br_a, w_br_b, w_br_c, w_o, mlp_g, w_mlp_in, w_mlp_out, g_next, last):
    def proj(first, n, *args):
        piece, col0 = w_at[first]
        return project(h, piece, layer, col0, n, *args)

    q_raw, q_rot = proj("a_q", A_Q, "both", BF16, tabs, _scales((A_Q, Q_SCALE)))
    cmp_src = proj("a_kc", 2 * A_KV, "plain", F32)
    k_sel = proj("a_ks", A_KV, "rope", BF16, tabs, _scales((A_KV, 1.0)))
    k_win = proj("a_kw", A_KV, "rope", BF16, tabs, _scales((A_KV, 1.0)))
    b_qk = proj("b_q", 2 * B_QKV, "rope", F32, tabs, _scales((B_QKV, Q_SCALE), (B_QKV, 1.0)))
    b_v = proj("b_v", B_QKV, "plain", F32)
    c_qk = proj("c_q", 2 * C_QKV, "rope", BF16, tabs, _scales((C_QKV, Q_SCALE), (C_QKV, 1.0)))
    gates_m = proj("m_a", 3 * D_MODEL, "sigmoid", BF16)
    gates_a = proj("a_g", LANES, "sigmoid", F32)
    vt = project_transposed(h, wt_v, layer, BF16)

    flat = NSA_CMP_LEN * HEAD_DIM
    cmp_kv = nsa_compress(cmp_src, pe_k.reshape(1, flat), w1_k.reshape(flat, NSA_CMP_HIDDEN).astype(BF16),
                          w2_k.astype(BF16), pe_v.reshape(1, flat),
                          w1_v.reshape(flat, NSA_CMP_HIDDEN).astype(BF16), w2_v.astype(BF16), batch)
    o_a = nsa_attention(q_raw, q_rot, cmp_kv, k_sel, k_win, vt, gates_a, overlap_t, batch)
    o_b = dilated_attention(b_qk, b_v, batch)
    o_c = moba_attention(c_qk, vt, batch)

    merged = gated_merge(o_a, o_b, o_c, gates_m, w_br_a, w_br_b, w_br_c, layer)
    x2, h2 = out_proj_residual_norm(merged, w_o, layer, x2, mlp_g)
    return mlp_residual_norm(h2, w_mlp_in, w_mlp_out, layer, x2, g_next, F32 if last else BF16, emit_x=not last)


def kernel(x, attn_norm_g, w_in, cmp_pe_k, cmp_w1_k, cmp_w2_k, cmp_pe_v, cmp_w1_v, cmp_w2_v,
           w_br_a, w_br_b, w_br_c, w_o, mlp_norm_g, w_mlp_in, w_mlp_out, final_norm_g):
    batch, seq, d = x.shape
    assert seq == SEQ and d == D_MODEL
    depth = w_in.shape[0]
    tabs = _rope_tables()
    overlap_t = _overlap_table_t()
    w_at, wt_v = repack_w_in(w_in)
    w_br_a, w_br_b, w_br_c, w_o = (t.astype(BF16) for t in (w_br_a, w_br_b, w_br_c, w_o))
    w_mlp_in, w_mlp_out = w_mlp_in.astype(BF16), w_mlp_out.astype(BF16)
    x2 = x.reshape(batch * seq, d)
    h = rmsnorm(x2, attn_norm_g[0], BF16)
    for l in range(depth):
        last = l == depth - 1
        g_next = final_norm_g if last else attn_norm_g[l + 1]
        x2, h = _layer(x2, h, batch, l, tabs, overlap_t, w_at, wt_v,
                       cmp_pe_k[l], cmp_w1_k[l], cmp_w2_k[l], cmp_pe_v[l], cmp_w1_v[l], cmp_w2_v[l],
                       w_br_a, w_br_b, w_br_c, w_o, mlp_norm_g[l], w_mlp_in, w_mlp_out, g_next, last)
    return h.reshape(batch, seq, d)
```

```python
import functools
import math

import numpy as np
import jax
import jax.numpy as jnp
from jax import lax
from jax.experimental import pallas as pl
from jax.experimental.pallas import tpu as pltpu

D_MODEL = 2048
SEQ = 2048
HEAD_DIM = 128
ROPE_THETA = 500000.0
ROPE_DIM = HEAD_DIM // 4
ROPE_HALF = ROPE_DIM // 2
NORM_EPS = 1e-6
NEG_INF = -1e30
Q_SCALE = HEAD_DIM ** -0.5 * math.log2(math.e)

NSA_HEADS = 8
NSA_KV_HEADS = 2
NSA_REP = NSA_HEADS // NSA_KV_HEADS
NSA_CMP_LEN = 32
NSA_CMP_STRIDE = 16
NSA_CMP_HIDDEN = 256
NSA_SEL_BLOCK = 64
NSA_SEL_TOPN = 16
NSA_WINDOW = 512
NSA_FORCE_BONUS = 1e4
NSA_NB = SEQ // NSA_SEL_BLOCK
NSA_M_PAD = SEQ // NSA_CMP_STRIDE

DIL_GROUPS = ((128, 1), (512, 4), (2048, 16))
DIL_HEADS_PER_GROUP = 4
DIL_HEADS = DIL_HEADS_PER_GROUP * len(DIL_GROUPS)
DIL_TILE = 128
DIL_UNROLL = 8

MOBA_HEADS = 8
MOBA_BLOCK = 256
MOBA_TOPK = 3
MOBA_NB = SEQ // MOBA_BLOCK

D_FF = 4 * D_MODEL
A_Q = NSA_HEADS * HEAD_DIM
A_KV = NSA_KV_HEADS * HEAD_DIM
A_G = 3 * NSA_HEADS
B_QKV = DIL_HEADS * HEAD_DIM
C_QKV = MOBA_HEADS * HEAD_DIM
IN_SPLIT_SIZES = (A_Q, A_KV, A_KV, A_KV, A_KV, A_KV, A_KV, A_G,
                  B_QKV, B_QKV, B_QKV, C_QKV, C_QKV, C_QKV,
                  D_MODEL, D_MODEL, D_MODEL)

LANES = 128
F32_SUBLANES = 8
PROJ_TN_MAX = 512
PROJ_VMEM_BUDGET = 46 * 1024 * 1024
VMEM_LIMIT = 56 * 1024 * 1024

BF16 = jnp.bfloat16
F32 = jnp.float32


def _params(*sem):
    return pltpu.CompilerParams(dimension_semantics=sem, vmem_limit_bytes=VMEM_LIMIT)


def _dot(a, b):
    return jnp.dot(a, b, preferred_element_type=F32)


def _dot_nt(a, b):
    return lax.dot_general(a, b, (((1,), (1,)), ((), ())), preferred_element_type=F32)


def _iota(shape, axis):
    return lax.broadcasted_iota(jnp.int32, shape, axis)


def _rmsnorm_rows(x, g):
    y = x * lax.rsqrt(jnp.mean(x * x, axis=-1, keepdims=True) + NORM_EPS)
    return y * g


def _rmsnorm_kernel(x_ref, g_ref, h_ref):
    h_ref[...] = _rmsnorm_rows(x_ref[...], g_ref[...]).astype(h_ref.dtype)


def rmsnorm(x2, g, out_dtype, tm=512):
    m, d = x2.shape
    return pl.pallas_call(
        _rmsnorm_kernel,
        grid=(m // tm,),
        in_specs=[pl.BlockSpec((tm, d), lambda i: (i, 0)), pl.BlockSpec((1, d), lambda i: (0, 0))],
        out_specs=pl.BlockSpec((tm, d), lambda i: (i, 0)),
        out_shape=jax.ShapeDtypeStruct((m, d), out_dtype),
        compiler_params=_params("parallel"),
        name="rmsnorm",
    )(x2, g.reshape(1, d))


def _rope_lanes(acc, c, s_up, s_dn):
    tn = acc.shape[1]
    reps = tn // HEAD_DIM
    if reps > 1:
        c = jnp.concatenate([c] * reps, axis=1)
        s_up = jnp.concatenate([s_up] * reps, axis=1)
        s_dn = jnp.concatenate([s_dn] * reps, axis=1)
    up = pltpu.roll(acc, tn - ROPE_HALF, axis=1)
    dn = pltpu.roll(acc, ROPE_HALF, axis=1)
    return acc * c + up * s_up + dn * s_dn


def _proj_kernel(*refs, mode):
    if mode in ("rope", "both"):
        h_ref, w_ref, c_ref, su_ref, sd_ref, cs_ref = refs[:6]
        outs = refs[6:]
    else:
        h_ref, w_ref = refs[:2]
        outs = refs[2:]
    acc = _dot_nt(h_ref[...], w_ref[...])
    if mode == "plain":
        outs[0][...] = acc.astype(outs[0].dtype)
    elif mode == "sigmoid":
        outs[0][...] = jax.nn.sigmoid(acc).astype(outs[0].dtype)
    else:
        col_scale = cs_ref[...]
        roped = _rope_lanes(acc, c_ref[...], su_ref[...], sd_ref[...]) * col_scale
        if mode == "both":
            outs[0][...] = (acc * col_scale).astype(outs[0].dtype)
            outs[1][...] = roped.astype(outs[1].dtype)
        else:
            outs[0][...] = roped.astype(outs[0].dtype)


def _proj_tiles(k, n, col0, out_bytes, n_out, rope):
    tn = math.gcd(math.gcd(PROJ_TN_MAX, n), col0) if col0 else math.gcd(PROJ_TN_MAX, n)
    for tm in (SEQ, SEQ // 2):
        blocks = tm * k * 2 + k * tn * 2 + n_out * tm * tn * out_bytes + (3 * tm * HEAD_DIM * 4 if rope else 0)
        if 2 * blocks <= PROJ_VMEM_BUDGET:
            break
    return tm, tn


def project(h, wt, layer, col0, n, mode, out_dtype, rope_tabs=None, col_scale=None):
    m, k = h.shape
    n_out = 2 if mode == "both" else 1
    rope = mode in ("rope", "both")
    tm, tn = _proj_tiles(k, n, col0, jnp.dtype(out_dtype).itemsize, n_out, rope)
    assert m % tm == 0 and n % tn == 0 and col0 % tn == 0 and SEQ % tm == 0
    j0 = col0 // tn
    in_specs = [pl.BlockSpec((tm, k), lambda i, j: (i, 0)),
                pl.BlockSpec((None, tn, k), lambda i, j: (layer, j0 + j, 0))]
    args = [h, wt]
    if rope:
        pos_blocks = SEQ // tm
        for t in rope_tabs:
            in_specs.append(pl.BlockSpec((tm, HEAD_DIM), lambda i, j: (i % pos_blocks, 0)))
            args.append(t)
        in_specs.append(pl.BlockSpec((1, tn), lambda i, j: (0, j)))
        args.append(col_scale)
    out_spec = pl.BlockSpec((tm, tn), lambda i, j: (i, j))
    out_shape = jax.ShapeDtypeStruct((m, n), out_dtype)
    res = pl.pallas_call(
        functools.partial(_proj_kernel, mode=mode),
        grid=(m // tm, n // tn),
        in_specs=in_specs,
        out_specs=[out_spec] * n_out,
        out_shape=[out_shape] * n_out,
        compiler_params=_params("parallel", "parallel"),
        name="proj_" + mode,
    )(*args)
    return res if n_out == 2 else res[0]


def _proj_t_kernel(wt_ref, h_ref, out_ref):
    out_ref[...] = _dot_nt(wt_ref[...], h_ref[...]).astype(out_ref.dtype)


def project_transposed(h, wt, layer, col0, n, out_dtype, tm=1024):
    m, k = h.shape
    tn = math.gcd(math.gcd(PROJ_TN_MAX, n), col0) if col0 else math.gcd(PROJ_TN_MAX, n)
    j0 = col0 // tn
    return pl.pallas_call(
        _proj_t_kernel,
        grid=(n // tn, m // tm),
        in_specs=[pl.BlockSpec((None, tn, k), lambda j, i: (layer, j0 + j, 0)),
                  pl.BlockSpec((tm, k), lambda j, i: (i, 0))],
        out_specs=pl.BlockSpec((tn, tm), lambda j, i: (j, i)),
        out_shape=jax.ShapeDtypeStruct((n, m), out_dtype),
        compiler_params=_params("parallel", "parallel"),
        name="proj_transposed",
    )(wt, h)


def _compress_kernel(k0_ref, k1_ref, v0_ref, v1_ref, pek_ref, w1k_ref, w2k_ref, pev_ref, w1v_ref, w2v_ref,
                     out_ref):
    half = NSA_CMP_STRIDE * HEAD_DIM
    for idx, src_ref in enumerate((k0_ref, k1_ref, v0_ref, v1_ref)):
        is_k = idx < NSA_KV_HEADS
        pe_ref, w1_ref, w2_ref = (pek_ref, w1k_ref, w2k_ref) if is_k else (pev_ref, w1v_ref, w2v_ref)
        x = jnp.concatenate(
            [src_ref[pl.ds(l, NSA_M_PAD, stride=NSA_CMP_STRIDE), :]
             for l in range(NSA_CMP_STRIDE)], axis=1)
        pe = pe_ref[...]
        first = _dot((x + pe[:, :half]).astype(BF16), w1_ref[:half, :])
        second = _dot((x + pe[:, half:]).astype(BF16), w1_ref[half:, :])
        hid = jax.nn.gelu(first + pltpu.roll(second, NSA_M_PAD - 1, axis=0))
        out = _dot(hid.astype(BF16), w2_ref[...])
        out_ref[idx] = (out if is_k else jnp.transpose(out)).astype(out_ref.dtype)


def nsa_compress(pf, pe_k, w1_k, w2_k, pe_v, w1_v, w2_v, batch):
    flat = NSA_CMP_LEN * HEAD_DIM
    const = lambda shape: pl.BlockSpec(shape, lambda b: (0,) * len(shape))
    return pl.pallas_call(
        _compress_kernel,
        grid=(batch,),
        in_specs=[pl.BlockSpec((SEQ, HEAD_DIM), lambda b, c=c: (b, c)) for c in range(4)] + [
                  const((1, flat)), const((flat, NSA_CMP_HIDDEN)), const((NSA_CMP_HIDDEN, HEAD_DIM)),
                  const((1, flat)), const((flat, NSA_CMP_HIDDEN)), const((NSA_CMP_HIDDEN, HEAD_DIM))],
        out_specs=pl.BlockSpec((None, 4, NSA_M_PAD, HEAD_DIM), lambda b: (b, 0, 0, 0)),
        out_shape=jax.ShapeDtypeStruct((batch, 4, NSA_M_PAD, HEAD_DIM), BF16),
        compiler_params=_params("parallel"),
        name="nsa_compress",
    )(pf, pf, pf, pf, pe_k, w1_k, w2_k, pe_v, w1_v, w2_v)


NSA_TQ = 256
NSA_KC = 256
NSA_BLK_PER_CHUNK = NSA_KC // NSA_SEL_BLOCK


def _nsa_kernel(qraw_ref, qrot_ref, cmp_ref, ksel_ref, vselt_ref, kwin_ref, vwint_ref,
                gate_ref, ovt_ref, o_ref, bias_scr):
    i = pl.program_id(1)
    tq, kc_w, rep, d = NSA_TQ, NSA_KC, NSA_REP, HEAD_DIM
    groups = range(NSA_KV_HEADS)
    width = rep * tq
    bpc = NSA_BLK_PER_CHUNK
    t0 = i * tq
    head_cols = lambda g, r: slice((g * rep + r) * d, (g * rep + r + 1) * d)
    stack = lambda ref, g: jnp.concatenate([ref[:, head_cols(g, r)] for r in range(rep)], axis=0)
    q_raw = [stack(qraw_ref, g) for g in groups]
    q_rot = [stack(qrot_ref, g) for g in groups]
    k_of = lambda ref, g, kj: ref[pl.ds(pl.multiple_of(kj * kc_w, kc_w), kc_w), g * d:(g + 1) * d]
    vt_of = lambda ref, g, kj: ref[g * d:(g + 1) * d, pl.ds(pl.multiple_of(kj * kc_w, kc_w), kc_w)]
    col_max = lambda s: jnp.max(s, axis=0, keepdims=True)
    col_sum = lambda p: jnp.sum(p, axis=0, keepdims=True)

    q_pos = t0 + (_iota((NSA_M_PAD, width), 1) & (tq - 1))
    vis = (_iota((NSA_M_PAD, width), 0) * NSA_CMP_STRIDE + (NSA_CMP_LEN - 1)) <= q_pos
    sc = [jnp.where(vis, _dot_nt(cmp_ref[g], q_raw[g]), NEG_INF) for g in groups]
    ec = [jnp.where(vis, jnp.exp2(s - col_max(s)), 0.0) for s in sc]
    pc = [(e / jnp.maximum(col_sum(e), 1e-30)).astype(BF16) for e in ec]
    o_cmp = [_dot(cmp_ref[NSA_KV_HEADS + g], pc[g]) for g in groups]
    imp_heads = [_dot(ovt_ref[...], p) for p in pc]

    blk = _iota((NSA_NB, tq), 0)
    q_blk = (t0 + _iota((NSA_NB, tq), 1)) // NSA_SEL_BLOCK
    forced = (blk == 0) | (blk == q_blk) | (blk == q_blk - 1)
    for g in groups:
        imp = imp_heads[g][:, :tq]
        for r in range(1, rep):
            imp = imp + imp_heads[g][:, r * tq:(r + 1) * tq]
        score = jnp.where(blk <= q_blk, imp + jnp.where(forced, NSA_FORCE_BONUS, 0.0), NEG_INF)
        rank = jnp.zeros((NSA_NB, tq), F32)
        for j in range(NSA_NB):
            sj = score[j:j + 1, :]
            ahead = (sj > score) | ((sj == score) & (blk > j))
            rank = rank + jnp.where(ahead, 1.0, 0.0)
        bias = jnp.where((rank < float(NSA_SEL_TOPN)) & (score > NEG_INF * 0.5), 0.0, NEG_INF)
        bias = jnp.concatenate([bias] * rep, axis=1)
        for c in range(SEQ // kc_w):
            bias_scr[g, c] = bias[c * bpc:(c + 1) * bpc, :]

    def sel_step(kj, ss, carry):
        bs = [bias_scr[g, kj] for g in groups]
        s3 = [s.reshape(bpc, NSA_SEL_BLOCK, width) for s in ss]
        m_new = [jnp.maximum(carry[g][0], col_max(jnp.max(s3[g], axis=1) + bs[g])) for g in groups]
        alpha = [jnp.exp2(carry[g][0] - m_new[g]) for g in groups]
        ps = [jnp.exp2(s3[g] - (m_new[g] - bs[g])[:, None, :]).reshape(kc_w, width) for g in groups]
        ls = [alpha[g] * carry[g][1] + col_sum(ps[g]) for g in groups]
        pvs = [_dot(vt_of(vselt_ref, g, kj), ps[g].astype(BF16)) for g in groups]
        return tuple((m_new[g], ls[g], alpha[g] * carry[g][2] + pvs[g]) for g in groups)

    init = tuple((jnp.full((1, width), NEG_INF, F32), jnp.zeros((1, width), F32), jnp.zeros((d, width), F32))
                 for g in groups)
    carry = lax.fori_loop(
        0, i, lambda kj, c: sel_step(kj, [_dot_nt(k_of(ksel_ref, g, kj), q_rot[g]) for g in groups], c), init)

    key_row = _iota((kc_w, width), 0)
    q_col = _iota((kc_w, width), 1) & (tq - 1)
    causal = key_row <= q_col
    diag = [jnp.where(causal, _dot_nt(k_of(ksel_ref, g, i), q_rot[g]), NEG_INF) for g in groups]
    sel = sel_step(i, diag, carry)

    far = i - 2
    near = i - 1
    far_ok = jnp.where(far >= 0, 0.0, NEG_INF)
    near_ok = jnp.where(near >= 0, 0.0, NEG_INF)
    far_c = jnp.maximum(far, 0)
    near_c = jnp.maximum(near, 0)
    s_own = [jnp.where(causal, _dot_nt(k_of(kwin_ref, g, i), q_rot[g]), NEG_INF) for g in groups]
    s_near = [_dot_nt(k_of(kwin_ref, g, near_c), q_rot[g]) + near_ok for g in groups]
    s_far = [jnp.where(key_row > q_col, _dot_nt(k_of(kwin_ref, g, far_c), q_rot[g]), NEG_INF) + far_ok
             for g in groups]
    m_w = [jnp.maximum(jnp.maximum(col_max(s_own[g]), col_max(s_near[g])), col_max(s_far[g])) for g in groups]
    p_own = [jnp.exp2(s_own[g] - m_w[g]) for g in groups]
    p_near = [jnp.exp2(s_near[g] - m_w[g]) for g in groups]
    p_far = [jnp.exp2(s_far[g] - m_w[g]) for g in groups]
    l_win = [col_sum(p_own[g]) + col_sum(p_near[g]) + col_sum(p_far[g]) for g in groups]
    acc_win = [_dot(vt_of(vwint_ref, g, i), p_own[g].astype(BF16))
               + _dot(vt_of(vwint_ref, g, near_c), p_near[g].astype(BF16))
               + _dot(vt_of(vwint_ref, g, far_c), p_far[g].astype(BF16)) for g in groups]

    gates = jnp.transpose(gate_ref[...])
    for g in groups:
        o_sel = sel[g][2] / sel[g][1]
        o_win = acc_win[g] / l_win[g]
        for r in range(rep):
            lanes = slice(r * tq, (r + 1) * tq)
            row = 3 * (g * rep + r)
            o = (gates[row:row + 1, :] * o_cmp[g][:, lanes] + gates[row + 1:row + 2, :] * o_sel[:, lanes]
                 + gates[row + 2:row + 3, :] * o_win[:, lanes])
            o_ref[:, head_cols(g, r)] = jnp.transpose(o).astype(o_ref.dtype)


def nsa_attention(q_raw, q_rot, cmp_kv, k_sel, k_win, vt_sel, vt_win, gates_a, overlap_t, batch):
    tq = NSA_TQ
    nq = SEQ // tq
    g_n = NSA_KV_HEADS
    row = lambda b, i: (b * nq + i, 0)
    return pl.pallas_call(
        _nsa_kernel,
        grid=(batch, nq),
        in_specs=[pl.BlockSpec((tq, A_Q), row),
                  pl.BlockSpec((tq, A_Q), row),
                  pl.BlockSpec((None, 2 * g_n, NSA_M_PAD, HEAD_DIM), lambda b, i: (b, 0, 0, 0)),
                  pl.BlockSpec((SEQ, A_KV), lambda b, i: (b, 0)),
                  pl.BlockSpec((A_KV, SEQ), lambda b, i: (0, b)),
                  pl.BlockSpec((SEQ, A_KV), lambda b, i: (b, 0)),
                  pl.BlockSpec((A_KV, SEQ), lambda b, i: (0, b)),
                  pl.BlockSpec((tq, LANES), row),
                  pl.BlockSpec((NSA_NB, NSA_M_PAD), lambda b, i: (0, 0))],
        out_specs=pl.BlockSpec((tq, A_Q), row),
        out_shape=jax.ShapeDtypeStruct((batch * SEQ, A_Q), BF16),
        scratch_shapes=[pltpu.VMEM((g_n, SEQ // NSA_KC, NSA_BLK_PER_CHUNK, NSA_REP * tq), F32)],
        compiler_params=_params("parallel", "parallel"),
        name="nsa_attention",
    )(q_raw, q_rot, cmp_kv, k_sel, vt_sel, k_win, vt_win, gates_a, overlap_t)


def _dilated_kernel(q0, k0, v0, q1, k1, v1, q2, k2, v2, o_ref,
                    o0_scr, o1_scr, o2_scr, l0_scr, l1_scr, l2_scr):
    tile = DIL_TILE
    groups = ((q0, k0, v0, o0_scr, l0_scr), (q1, k1, v1, o1_scr, l1_scr), (q2, k2, v2, o2_scr, l2_scr))
    for (window, dil), (q_ref, k_ref, v_ref, og_scr, lg_scr) in zip(DIL_GROUPS, groups):
        assert window // dil == tile
        per_class = SEQ // dil
        tiles_per_class = per_class // tile
        nk = tile if tiles_per_class == 1 else 2 * tile
        a_minus_a = _iota((tile, nk), 0) - _iota((tile, nk), 1)

        def step(u, carry, q_ref=q_ref, k_ref=k_ref, v_ref=v_ref, og_scr=og_scr, lg_scr=lg_scr,
                 dil=dil, tiles_per_class=tiles_per_class, nk=nk, a_minus_a=a_minus_a):
            ts = [u * DIL_UNROLL + a for a in range(DIL_UNROLL)]
            cls = [t // tiles_per_class for t in ts]
            n0 = [(t % tiles_per_class) * tile for t in ts]
            kbase = [jnp.maximum(n - (nk - tile), 0) for n in n0]
            q_rows = [pl.ds(c + dil * n, tile, stride=dil) for c, n in zip(cls, n0)]
            k_rows = [pl.ds(c + dil * kb, nk, stride=dil) for c, kb in zip(cls, kbase)]
            qs = [q_ref[r, :].astype(BF16) for r in q_rows]
            ks = [k_ref[r, :].astype(BF16) for r in k_rows]
            vs = [v_ref[r, :].astype(BF16) for r in k_rows]
            ss = [_dot_nt(q, k) for q, k in zip(qs, ks)]
            masks = []
            for n, kb in zip(n0, kbase):
                dist = a_minus_a + (n - kb)
                masks.append((dist >= 0) & (dist <= tile))
            ss = [jnp.where(mk, s, NEG_INF) for mk, s in zip(masks, ss)]
            ms = [jnp.max(s, axis=-1, keepdims=True) for s in ss]
            es = [jnp.where(mk, jnp.exp2(s - m), 0.0) for mk, s, m in zip(masks, ss, ms)]
            dens = [jnp.maximum(jnp.sum(e, axis=-1, keepdims=True), 1e-30) for e in es]
            os_ = [_dot((e / den).astype(BF16), v) for e, den, v in zip(es, dens, vs)]
            for r, o, m, den in zip(q_rows, os_, ms, dens):
                og_scr[r, :] = o
                lg_scr[r, :] = jnp.broadcast_to(m + jnp.log2(den), (tile, HEAD_DIM))
            return carry

        lax.fori_loop(0, SEQ // tile // DIL_UNROLL, step, 0)

    rows = 256

    def merge_body(c, carry):
        sl = pl.ds(pl.multiple_of(c * rows, rows), rows)
        la, lb, lc = l0_scr[sl, :], l1_scr[sl, :], l2_scr[sl, :]
        mx = jnp.maximum(jnp.maximum(la, lb), lc)
        ea, eb, ec = jnp.exp2(la - mx), jnp.exp2(lb - mx), jnp.exp2(lc - mx)
        tot = ea + eb + ec
        out = (ea / tot) * o0_scr[sl, :] + (eb / tot) * o1_scr[sl, :] + (ec / tot) * o2_scr[sl, :]
        o_ref[sl, :] = out.astype(o_ref.dtype)
        return carry

    lax.fori_loop(0, SEQ // rows, merge_body, 0)


def dilated_attention(b_qk, b_v, batch):
    hp = DIL_HEADS_PER_GROUP
    in_specs, args = [], []
    for g in range(len(DIL_GROUPS)):
        for arr, c0 in ((b_qk, 0), (b_qk, DIL_HEADS), (b_v, 0)):
            in_specs.append(pl.BlockSpec((SEQ, HEAD_DIM), lambda b, j, c0=c0, g=g: (b, c0 + g * hp + j)))
            args.append(arr)
    return pl.pallas_call(
        _dilated_kernel,
        grid=(batch, hp),
        in_specs=in_specs,
        out_specs=pl.BlockSpec((SEQ, HEAD_DIM), lambda b, j: (b, j)),
        out_shape=jax.ShapeDtypeStruct((batch * SEQ, hp * HEAD_DIM), BF16),
        scratch_shapes=[pltpu.VMEM((SEQ, HEAD_DIM), F32)] * 6,
        compiler_params=_params("parallel", "parallel"),
        name="dilated_attention",
    )(*args)


def _moba_kernel(q_ref, k_ref, vt_ref, o_ref, kmean_scr, bias_scr):
    i = pl.program_id(1)
    tq, nb, d = MOBA_BLOCK, MOBA_NB, HEAD_DIM
    heads = range(MOBA_HEADS)
    col = lambda h: slice(h * d, (h + 1) * d)

    @pl.when(i == 0)
    def _():
        avg = jnp.where(_iota((nb, SEQ), 1) // MOBA_BLOCK == _iota((nb, SEQ), 0), 1.0 / MOBA_BLOCK, 0.0)
        kmean_scr[...] = _dot(avg.astype(BF16), k_ref[...]).astype(kmean_scr.dtype)

    blk = _iota((nb, tq), 0)
    own = pl.multiple_of(i * tq, tq)
    causal = _iota((tq, tq), 0) <= _iota((tq, tq), 1)
    qs = [q_ref[:, col(h)] for h in heads]
    gates = [jnp.where(blk < i, _dot_nt(kmean_scr[:, col(h)], qs[h]), NEG_INF) for h in heads]
    ss = [_dot_nt(k_ref[pl.ds(own, tq), col(h)], qs[h]) for h in heads]
    for h in heads:
        gate = gates[h]
        rank = jnp.zeros((nb, tq), F32)
        for j in range(nb):
            gj = gate[j:j + 1, :]
            ahead = (gj > gate) | ((gj == gate) & (blk > j))
            rank = rank + jnp.where(ahead, 1.0, 0.0)
        bias_scr[h] = jnp.where((rank < float(MOBA_TOPK)) & (gate > NEG_INF * 0.5), 0.0, NEG_INF)
    ss = [jnp.where(causal, s, NEG_INF) for s in ss]
    ms = [jnp.max(s, axis=0, keepdims=True) for s in ss]
    ps = [jnp.exp2(s - m) for s, m in zip(ss, ms)]
    ls = [jnp.sum(p, axis=0, keepdims=True) for p in ps]
    accs = [_dot(vt_ref[col(h), pl.ds(own, tq)], ps[h].astype(BF16)) for h in heads]

    def body(kj, carry):
        off = pl.multiple_of(kj * tq, tq)
        bs = [bias_scr[h, pl.ds(kj, 1), :] for h in heads]
        ss = [_dot_nt(k_ref[pl.ds(off, tq), col(h)], qs[h]) for h in heads]
        m_new = [jnp.maximum(carry[h][0], jnp.max(ss[h], axis=0, keepdims=True) + bs[h]) for h in heads]
        alpha = [jnp.exp2(carry[h][0] - m_new[h]) for h in heads]
        ps = [jnp.exp2(ss[h] - (m_new[h] - bs[h])) for h in heads]
        ls = [alpha[h] * carry[h][1] + jnp.sum(ps[h], axis=0, keepdims=True) for h in heads]
        pvs = [_dot(vt_ref[col(h), pl.ds(off, tq)], ps[h].astype(BF16)) for h in heads]
        return tuple((m_new[h], ls[h], alpha[h] * carry[h][2] + pvs[h]) for h in heads)

    fin = lax.fori_loop(0, i, body, tuple((ms[h], ls[h], accs[h]) for h in heads))
    for h in heads:
        _, l, acc = fin[h]
        o_ref[:, col(h)] = jnp.transpose(acc / l).astype(o_ref.dtype)


def moba_attention(c_qk, vt, batch):
    tq = MOBA_BLOCK
    nq = SEQ // tq
    return pl.pallas_call(
        _moba_kernel,
        grid=(batch, nq),
        in_specs=[pl.BlockSpec((tq, C_QKV), lambda b, i: (b * nq + i, 0)),
                  pl.BlockSpec((SEQ, C_QKV), lambda b, i: (b, 1)),
                  pl.BlockSpec((C_QKV, SEQ), lambda b, i: (0, b))],
        out_specs=pl.BlockSpec((tq, C_QKV), lambda b, i: (b * nq + i, 0)),
        out_shape=jax.ShapeDtypeStruct((batch * SEQ, C_QKV), BF16),
        scratch_shapes=[pltpu.VMEM((MOBA_NB, C_QKV), BF16), pltpu.VMEM((MOBA_HEADS, MOBA_NB, tq), F32)],
        compiler_params=_params("parallel", "arbitrary"),
        name="moba_attention",
    )(c_qk, c_qk, vt)


def _merge_kernel(oa_ref, ob_ref, oc_ref, ga_ref, gb_ref, gc_ref, wa_ref, wb_ref, wc_ref, out_ref):
    y = ga_ref[...] * _dot(oa_ref[...], wa_ref[...])
    y = y + gb_ref[...] * _dot(ob_ref[...], wb_ref[...])
    y = y + gc_ref[...] * _dot(oc_ref[...], wc_ref[...])
    out_ref[...] = y.astype(out_ref.dtype)


def gated_merge(o_a, o_b, o_c, gates_m, w_a, w_b, w_c, layer, tm=1024, tn=512):
    m = o_a.shape[0]
    nb = D_MODEL // tn
    act = lambda width: pl.BlockSpec((tm, width), lambda j, i: (i, 0))
    gate = lambda g: pl.BlockSpec((tm, tn), lambda j, i, g=g: (i, g * nb + j))
    wgt = lambda width: pl.BlockSpec((None, width, tn), lambda j, i: (layer, 0, j))
    return pl.pallas_call(
        _merge_kernel,
        grid=(nb, m // tm),
        in_specs=[act(o_a.shape[1]), act(o_b.shape[1]), act(o_c.shape[1]),
                  gate(0), gate(1), gate(2),
                  wgt(w_a.shape[1]), wgt(w_b.shape[1]), wgt(w_c.shape[1])],
        out_specs=pl.BlockSpec((tm, tn), lambda j, i: (i, j)),
        out_shape=jax.ShapeDtypeStruct((m, D_MODEL), BF16),
        compiler_params=_params("parallel", "parallel"),
        name="gated_merge",
    )(o_a, o_b, o_c, gates_m, gates_m, gates_m, w_a, w_b, w_c)


def _out_proj_kernel(y_ref, w_ref, x_ref, g_ref, xo_ref, h_ref):
    x_new = x_ref[...] + _dot(y_ref[...], w_ref[...])
    xo_ref[...] = x_new
    h_ref[...] = _rmsnorm_rows(x_new, g_ref[...]).astype(h_ref.dtype)


def out_proj_residual_norm(y, w_o, layer, x2, g, tm=512):
    m, d = x2.shape
    row = pl.BlockSpec((tm, d), lambda i: (i, 0))
    return pl.pallas_call(
        _out_proj_kernel,
        grid=(m // tm,),
        in_specs=[row, pl.BlockSpec((None, d, d), lambda i: (layer, 0, 0)), row,
                  pl.BlockSpec((1, d), lambda i: (0, 0))],
        out_specs=[row, row],
        out_shape=[jax.ShapeDtypeStruct((m, d), F32), jax.ShapeDtypeStruct((m, d), BF16)],
        compiler_params=_params("parallel"),
        name="out_proj",
    )(y, w_o, x2, g.reshape(1, d))


def _mlp_kernel(h_ref, w1_ref, w2_ref, x_ref, g_ref, *rest):
    xo_ref = rest[0] if len(rest) == 3 else None
    hn_ref, acc_scr = rest[-2:]
    f = pl.program_id(1)

    @pl.when(f == 0)
    def _():
        acc_scr[...] = jnp.zeros_like(acc_scr)

    u = jnp.square(jnp.maximum(_dot(h_ref[...], w1_ref[...]), 0.0))
    acc_scr[...] += _dot(u.astype(BF16), w2_ref[...])

    @pl.when(f == pl.num_programs(1) - 1)
    def _():
        x_new = x_ref[...] + acc_scr[...]
        if xo_ref is not None:
            xo_ref[...] = x_new
        hn_ref[...] = _rmsnorm_rows(x_new, g_ref[...]).astype(hn_ref.dtype)


def mlp_residual_norm(h2, w1, w2, layer, x2, g_next, next_dtype, emit_x, tm=1024, tf=512):
    m, d = x2.shape
    row = pl.BlockSpec((tm, d), lambda i, f: (i, 0))
    row_once = pl.BlockSpec((tm, d), lambda i, f: (i, 0), pipeline_mode=pl.Buffered(1))
    out_shape = [jax.ShapeDtypeStruct((m, d), F32)] * emit_x + [jax.ShapeDtypeStruct((m, d), next_dtype)]
    res = pl.pallas_call(
        _mlp_kernel,
        grid=(m // tm, D_FF // tf),
        in_specs=[row, pl.BlockSpec((None, d, tf), lambda i, f: (layer, 0, f)),
                  pl.BlockSpec((None, tf, d), lambda i, f: (layer, f, 0)),
                  row_once, pl.BlockSpec((1, d), lambda i, f: (0, 0))],
        out_specs=[row_once] * len(out_shape),
        out_shape=out_shape,
        scratch_shapes=[pltpu.VMEM((tm, d), F32)],
        compiler_params=_params("parallel", "arbitrary"),
        name="mlp",
    )(h2, w1, w2, x2, g_next.reshape(1, d))
    return (res[0], res[1]) if emit_x else (None, res[0])


def _rope_tables():
    inv = ROPE_THETA ** (-jnp.arange(0, ROPE_DIM, 2, dtype=F32) / ROPE_DIM)
    ang = jnp.arange(SEQ, dtype=F32)[:, None] * inv[None, :]
    cos, sin = jnp.cos(ang), jnp.sin(ang)
    zeros = jnp.zeros((SEQ, HEAD_DIM - ROPE_DIM), F32)
    zero_h = jnp.zeros((SEQ, ROPE_HALF), F32)
    c = jnp.concatenate([cos, cos, jnp.ones_like(zeros)], axis=1)
    s_up = jnp.concatenate([-sin, zero_h, zeros], axis=1)
    s_dn = jnp.concatenate([zero_h, sin, zeros], axis=1)
    return c, s_up, s_dn


def _overlap_table_t():
    cs = np.arange(NSA_M_PAD)[None, :] * NSA_CMP_STRIDE
    bs = np.arange(NSA_NB)[:, None] * NSA_SEL_BLOCK
    ov = np.clip(np.minimum(cs + NSA_CMP_LEN, bs + NSA_SEL_BLOCK) - np.maximum(cs, bs), 0, None) / NSA_CMP_LEN
    ov[:, NSA_M_PAD - 1] = 0.0
    return jnp.asarray(ov, dtype=BF16)


W_IN_NAMES = ("a_q", "a_kc", "a_vc", "a_ks", "a_vs", "a_kw", "a_vw", "a_g",
              "b_q", "b_k", "b_v", "c_q", "c_k", "c_v", "m_a", "m_b", "m_c")
W_IN_START = dict(zip(W_IN_NAMES, np.cumsum((0,) + IN_SPLIT_SIZES[:-1]).tolist()))
W_IN_SIZE = dict(zip(W_IN_NAMES, IN_SPLIT_SIZES))


W_GATE_AT = W_IN_START["a_g"]


def _cast_rows_kernel(src_ref, out_ref, *, keep):
    x = src_ref[0]
    if keep < x.shape[0]:
        x = jnp.where(_iota(x.shape, 0) < keep, x, 0.0)
    out_ref[...] = x.astype(out_ref.dtype)


def cast_rows(wt, row0, n_rows, tr, keep=None):
    depth, _, k = wt.shape
    return pl.pallas_call(
        functools.partial(_cast_rows_kernel, keep=tr if keep is None else keep),
        grid=(depth, n_rows // tr),
        in_specs=[pl.BlockSpec((pl.Element(1), pl.Element(tr), pl.Element(k)),
                               lambda l, r: (l, pl.multiple_of(row0 + r * tr, F32_SUBLANES), 0))],
        out_specs=pl.BlockSpec((None, tr, k), lambda l, r: (l, r, 0)),
        out_shape=jax.ShapeDtypeStruct((depth, n_rows, k), BF16),
        compiler_params=_params("parallel", "parallel"),
        name="cast_rows",
    )(wt)


def bf16_w_in_t(w_in):
    wt = jnp.transpose(w_in, (0, 2, 1))
    hi0 = W_GATE_AT + A_G
    lo = cast_rows(wt, 0, W_GATE_AT, 512)
    hi = cast_rows(wt, hi0, wt.shape[1] - hi0, 512)
    ag = cast_rows(wt, W_GATE_AT, LANES, LANES, keep=A_G)
    where = {"a_g": (ag, 0)}
    for name in W_IN_NAMES:
        if W_IN_START[name] < W_GATE_AT:
            where[name] = (lo, W_IN_START[name])
        elif W_IN_START[name] > W_GATE_AT:
            where[name] = (hi, W_IN_START[name] - hi0)
    return where


def _scales(*widths_and_values):
    return jnp.concatenate([jnp.full((1, w), v, F32) for w, v in widths_and_values], axis=1)


def _layer(x2, h, batch, layer, tabs, overlap_t, w_at, pe_k, w1_k, w2_k, pe_v, w1_v, w2_v,
           w_br_a, w_br_b, w_br_c, w_o, mlp_g, w_mlp_in, w_mlp_out, g_next, last):
    def proj(first, n, *args):
        piece, col0 = w_at[first]
        return project(h, piece, layer, col0, n, *args)

    def proj_t(name):
        piece, col0 = w_at[name]
        return project_transposed(h, piece, layer, col0, W_IN_SIZE[name], BF16)

    q_raw, q_rot = proj("a_q", A_Q, "both", BF16, tabs, _scales((A_Q, Q_SCALE)))
    cmp_src = proj("a_kc", 2 * A_KV, "plain", F32)
    k_sel = proj("a_ks", A_KV, "rope", BF16, tabs, _scales((A_KV, 1.0)))
    k_win = proj("a_kw", A_KV, "rope", BF16, tabs, _scales((A_KV, 1.0)))
    b_qk = proj("b_q", 2 * B_QKV, "rope", F32, tabs, _scales((B_QKV, Q_SCALE), (B_QKV, 1.0)))
    b_v = proj("b_v", B_QKV, "plain", F32)
    c_qk = proj("c_q", 2 * C_QKV, "rope", BF16, tabs, _scales((C_QKV, Q_SCALE), (C_QKV, 1.0)))
    gates_m = proj("m_a", 3 * D_MODEL, "sigmoid", BF16)
    gates_a = proj("a_g", LANES, "sigmoid", F32)
    vt_sel, vt_win, vt_moba = proj_t("a_vs"), proj_t("a_vw"), proj_t("c_v")

    flat = NSA_CMP_LEN * HEAD_DIM
    cmp_kv = nsa_compress(cmp_src, pe_k.reshape(1, flat), w1_k.reshape(flat, NSA_CMP_HIDDEN).astype(BF16),
                          w2_k.astype(BF16), pe_v.reshape(1, flat),
                          w1_v.reshape(flat, NSA_CMP_HIDDEN).astype(BF16), w2_v.astype(BF16), batch)
    o_a = nsa_attention(q_raw, q_rot, cmp_kv, k_sel, k_win, vt_sel, vt_win, gates_a, overlap_t, batch)
    o_b = dilated_attention(b_qk, b_v, batch)
    o_c = moba_attention(c_qk, vt_moba, batch)

    merged = gated_merge(o_a, o_b, o_c, gates_m, w_br_a, w_br_b, w_br_c, layer)
    x2, h2 = out_proj_residual_norm(merged, w_o, layer, x2, mlp_g)
    return mlp_residual_norm(h2, w_mlp_in, w_mlp_out, layer, x2, g_next, F32 if last else BF16, emit_x=not last)


def kernel(x, attn_norm_g, w_in, cmp_pe_k, cmp_w1_k, cmp_w2_k, cmp_pe_v, cmp_w1_v, cmp_w2_v,
           w_br_a, w_br_b, w_br_c, w_o, mlp_norm_g, w_mlp_in, w_mlp_out, final_norm_g):
    batch, seq, d = x.shape
    assert seq == SEQ and d == D_MODEL
    depth = w_in.shape[0]
    tabs = _rope_tables()
    overlap_t = _overlap_table_t()
    w_at = bf16_w_in_t(w_in)
    w_br_a, w_br_b, w_br_c, w_o = (t.astype(BF16) for t in (w_br_a, w_br_b, w_br_c, w_o))
    w_mlp_in, w_mlp_out = w_mlp_in.astype(BF16), w_mlp_out.astype(BF16)
    x2 = x.reshape(batch * seq, d)
    h = rmsnorm(x2, attn_norm_g[0], BF16)
    for l in range(depth):
        last = l == depth - 1
        g_next = final_norm_g if last else attn_norm_g[l + 1]
        x2, h = _layer(x2, h, batch, l, tabs, overlap_t, w_at,
                       cmp_pe_k[l], cmp_w1_k[l], cmp_w2_k[l], cmp_pe_v[l], cmp_w1_v[l], cmp_w2_v[l],
                       w_br_a, w_br_b, w_br_c, w_o, mlp_norm_g[l], w_mlp_in, w_mlp_out, g_next, last)
    return h.reshape(batch, seq, d)
```

```python
import functools
import math

import numpy as np
import jax
import jax.numpy as jnp
from jax import lax
from jax.experimental import pallas as pl
from jax.experimental.pallas import tpu as pltpu

D_MODEL = 2048
SEQ = 2048
HEAD_DIM = 128
ROPE_THETA = 500000.0
ROPE_DIM = HEAD_DIM // 4
ROPE_HALF = ROPE_DIM // 2
NORM_EPS = 1e-6
NEG_INF = -1e30
Q_SCALE = HEAD_DIM ** -0.5 * math.log2(math.e)

NSA_HEADS = 8
NSA_KV_HEADS = 2
NSA_REP = NSA_HEADS // NSA_KV_HEADS
NSA_CMP_LEN = 32
NSA_CMP_STRIDE = 16
NSA_CMP_HIDDEN = 256
NSA_SEL_BLOCK = 64
NSA_SEL_TOPN = 16
NSA_WINDOW = 512
NSA_FORCE_BONUS = 1e4
NSA_NB = SEQ // NSA_SEL_BLOCK
NSA_M_PAD = SEQ // NSA_CMP_STRIDE

DIL_GROUPS = ((128, 1), (512, 4), (2048, 16))
DIL_HEADS_PER_GROUP = 4
DIL_HEADS = DIL_HEADS_PER_GROUP * len(DIL_GROUPS)
DIL_TILE = 128
DIL_UNROLL = 16

MOBA_HEADS = 8
MOBA_BLOCK = 256
MOBA_TOPK = 3
MOBA_NB = SEQ // MOBA_BLOCK

D_FF = 4 * D_MODEL
A_Q = NSA_HEADS * HEAD_DIM
A_KV = NSA_KV_HEADS * HEAD_DIM
A_G = 3 * NSA_HEADS
B_QKV = DIL_HEADS * HEAD_DIM
C_QKV = MOBA_HEADS * HEAD_DIM
IN_SPLIT_SIZES = (A_Q, A_KV, A_KV, A_KV, A_KV, A_KV, A_KV, A_G,
                  B_QKV, B_QKV, B_QKV, C_QKV, C_QKV, C_QKV,
                  D_MODEL, D_MODEL, D_MODEL)

LANES = 128
F32_SUBLANES = 8
PROJ_TN_MAX = 512
PROJ_VMEM_BUDGET = 46 * 1024 * 1024
VMEM_LIMIT = 60 * 1024 * 1024

BF16 = jnp.bfloat16
F32 = jnp.float32


def _params(*sem):
    return pltpu.CompilerParams(dimension_semantics=sem, vmem_limit_bytes=VMEM_LIMIT)


def _dot(a, b):
    return jnp.dot(a, b, preferred_element_type=F32)


def _dot_nt(a, b):
    return lax.dot_general(a, b, (((1,), (1,)), ((), ())), preferred_element_type=F32)


def _iota(shape, axis):
    return lax.broadcasted_iota(jnp.int32, shape, axis)


def _rmsnorm_rows(x, g):
    y = x * lax.rsqrt(jnp.mean(x * x, axis=-1, keepdims=True) + NORM_EPS)
    return y * g


def _rmsnorm_kernel(x_ref, g_ref, h_ref):
    h_ref[...] = _rmsnorm_rows(x_ref[...], g_ref[...]).astype(h_ref.dtype)


def rmsnorm(x2, g, out_dtype, tm=512):
    m, d = x2.shape
    return pl.pallas_call(
        _rmsnorm_kernel,
        grid=(m // tm,),
        in_specs=[pl.BlockSpec((tm, d), lambda i: (i, 0)), pl.BlockSpec((1, d), lambda i: (0, 0))],
        out_specs=pl.BlockSpec((tm, d), lambda i: (i, 0)),
        out_shape=jax.ShapeDtypeStruct((m, d), out_dtype),
        compiler_params=_params("parallel"),
        name="rmsnorm",
    )(x2, g.reshape(1, d))


def _rope_lanes(acc, c, s_up, s_dn):
    tn = acc.shape[1]
    reps = tn // HEAD_DIM
    if reps > 1:
        c = jnp.concatenate([c] * reps, axis=1)
        s_up = jnp.concatenate([s_up] * reps, axis=1)
        s_dn = jnp.concatenate([s_dn] * reps, axis=1)
    up = pltpu.roll(acc, tn - ROPE_HALF, axis=1)
    dn = pltpu.roll(acc, ROPE_HALF, axis=1)
    return acc * c + up * s_up + dn * s_dn


def _proj_kernel(*refs, mode):
    if mode in ("rope", "both"):
        h_ref, w_ref, c_ref, su_ref, sd_ref, cs_ref = refs[:6]
        outs = refs[6:]
    else:
        h_ref, w_ref = refs[:2]
        outs = refs[2:]
    acc = _dot_nt(h_ref[...], w_ref[...])
    if mode == "plain":
        outs[0][...] = acc.astype(outs[0].dtype)
    elif mode == "sigmoid":
        outs[0][...] = jax.nn.sigmoid(acc).astype(outs[0].dtype)
    else:
        col_scale = cs_ref[...]
        roped = _rope_lanes(acc, c_ref[...], su_ref[...], sd_ref[...]) * col_scale
        if mode == "both":
            outs[0][...] = (acc * col_scale).astype(outs[0].dtype)
            outs[1][...] = roped.astype(outs[1].dtype)
        else:
            outs[0][...] = roped.astype(outs[0].dtype)


def _proj_tiles(k, n, col0, out_bytes, n_out, rope):
    tn = math.gcd(math.gcd(PROJ_TN_MAX, n), col0) if col0 else math.gcd(PROJ_TN_MAX, n)
    for tm in (SEQ, SEQ // 2):
        blocks = tm * k * 2 + k * tn * 2 + n_out * tm * tn * out_bytes + (3 * tm * HEAD_DIM * 4 if rope else 0)
        if 2 * blocks <= PROJ_VMEM_BUDGET:
            break
    return tm, tn


def project(h, wt, layer, col0, n, mode, out_dtype, rope_tabs=None, col_scale=None):
    m, k = h.shape
    n_out = 2 if mode == "both" else 1
    rope = mode in ("rope", "both")
    tm, tn = _proj_tiles(k, n, col0, jnp.dtype(out_dtype).itemsize, n_out, rope)
    assert m % tm == 0 and n % tn == 0 and col0 % tn == 0 and SEQ % tm == 0
    j0 = col0 // tn
    in_specs = [pl.BlockSpec((tm, k), lambda i, j: (i, 0)),
                pl.BlockSpec((None, tn, k), lambda i, j: (layer, j0 + j, 0))]
    args = [h, wt]
    if rope:
        pos_blocks = SEQ // tm
        for t in rope_tabs:
            in_specs.append(pl.BlockSpec((tm, HEAD_DIM), lambda i, j: (i % pos_blocks, 0)))
            args.append(t)
        in_specs.append(pl.BlockSpec((1, tn), lambda i, j: (0, j)))
        args.append(col_scale)
    out_spec = pl.BlockSpec((tm, tn), lambda i, j: (i, j))
    out_shape = jax.ShapeDtypeStruct((m, n), out_dtype)
    res = pl.pallas_call(
        functools.partial(_proj_kernel, mode=mode),
        grid=(m // tm, n // tn),
        in_specs=in_specs,
        out_specs=[out_spec] * n_out,
        out_shape=[out_shape] * n_out,
        compiler_params=_params("parallel", "parallel"),
        name="proj_" + mode,
    )(*args)
    return res if n_out == 2 else res[0]


def _proj_t_kernel(wt_ref, h_ref, out_ref):
    out_ref[...] = _dot_nt(wt_ref[...], h_ref[...]).astype(out_ref.dtype)


def project_transposed(h, wt, layer, col0, n, out_dtype, tm=1024):
    m, k = h.shape
    tn = math.gcd(math.gcd(PROJ_TN_MAX, n), col0) if col0 else math.gcd(PROJ_TN_MAX, n)
    j0 = col0 // tn
    return pl.pallas_call(
        _proj_t_kernel,
        grid=(n // tn, m // tm),
        in_specs=[pl.BlockSpec((None, tn, k), lambda j, i: (layer, j0 + j, 0)),
                  pl.BlockSpec((tm, k), lambda j, i: (i, 0))],
        out_specs=pl.BlockSpec((tn, tm), lambda j, i: (j, i)),
        out_shape=jax.ShapeDtypeStruct((n, m), out_dtype),
        compiler_params=_params("parallel", "parallel"),
        name="proj_transposed",
    )(wt, h)


def _compress_kernel(k0_ref, k1_ref, v0_ref, v1_ref, pek_ref, w1k_ref, w2k_ref, pev_ref, w1v_ref, w2v_ref,
                     out_ref):
    half = NSA_CMP_STRIDE * HEAD_DIM
    for idx, src_ref in enumerate((k0_ref, k1_ref, v0_ref, v1_ref)):
        is_k = idx < NSA_KV_HEADS
        pe_ref, w1_ref, w2_ref = (pek_ref, w1k_ref, w2k_ref) if is_k else (pev_ref, w1v_ref, w2v_ref)
        x = jnp.concatenate(
            [src_ref[pl.ds(l, NSA_M_PAD, stride=NSA_CMP_STRIDE), :]
             for l in range(NSA_CMP_STRIDE)], axis=1)
        pe = pe_ref[...]
        first = _dot((x + pe[:, :half]).astype(BF16), w1_ref[:half, :])
        second = _dot((x + pe[:, half:]).astype(BF16), w1_ref[half:, :])
        hid = jax.nn.gelu(first + pltpu.roll(second, NSA_M_PAD - 1, axis=0))
        out = _dot(hid.astype(BF16), w2_ref[...])
        out_ref[idx] = (out if is_k else jnp.transpose(out)).astype(out_ref.dtype)


def nsa_compress(pf, pe_k, w1_k, w2_k, pe_v, w1_v, w2_v, batch):
    flat = NSA_CMP_LEN * HEAD_DIM
    const = lambda shape: pl.BlockSpec(shape, lambda b: (0,) * len(shape))
    return pl.pallas_call(
        _compress_kernel,
        grid=(batch,),
        in_specs=[pl.BlockSpec((SEQ, HEAD_DIM), lambda b, c=c: (b, c)) for c in range(4)] + [
                  const((1, flat)), const((flat, NSA_CMP_HIDDEN)), const((NSA_CMP_HIDDEN, HEAD_DIM)),
                  const((1, flat)), const((flat, NSA_CMP_HIDDEN)), const((NSA_CMP_HIDDEN, HEAD_DIM))],
        out_specs=pl.BlockSpec((None, 4, NSA_M_PAD, HEAD_DIM), lambda b: (b, 0, 0, 0)),
        out_shape=jax.ShapeDtypeStruct((batch, 4, NSA_M_PAD, HEAD_DIM), BF16),
        compiler_params=_params("parallel"),
        name="nsa_compress",
    )(pf, pf, pf, pf, pe_k, w1_k, w2_k, pe_v, w1_v, w2_v)


NSA_TQ = 256
NSA_KC = 256
NSA_BLK_PER_CHUNK = NSA_KC // NSA_SEL_BLOCK


def _nsa_kernel(qraw_ref, qrot_ref, cmp_ref, ksel_ref, vselt_ref, kwin_ref, vwint_ref,
                gate_ref, ovt_ref, o_ref, bias_scr):
    i = pl.program_id(1)
    tq, kc_w, rep, d = NSA_TQ, NSA_KC, NSA_REP, HEAD_DIM
    groups = range(NSA_KV_HEADS)
    width = rep * tq
    bpc = NSA_BLK_PER_CHUNK
    t0 = i * tq
    head_cols = lambda g, r: slice((g * rep + r) * d, (g * rep + r + 1) * d)
    stack = lambda ref, g: jnp.concatenate([ref[:, head_cols(g, r)] for r in range(rep)], axis=0)
    q_raw = [stack(qraw_ref, g) for g in groups]
    q_rot = [stack(qrot_ref, g) for g in groups]
    k_of = lambda ref, g, kj: ref[pl.ds(pl.multiple_of(kj * kc_w, kc_w), kc_w), g * d:(g + 1) * d]
    vt_of = lambda ref, g, kj: ref[g * d:(g + 1) * d, pl.ds(pl.multiple_of(kj * kc_w, kc_w), kc_w)]
    col_max = lambda s: jnp.max(s, axis=0, keepdims=True)
    col_sum = lambda p: jnp.sum(p, axis=0, keepdims=True)

    q_pos = t0 + (_iota((NSA_M_PAD, width), 1) & (tq - 1))
    vis = (_iota((NSA_M_PAD, width), 0) * NSA_CMP_STRIDE + (NSA_CMP_LEN - 1)) <= q_pos
    sc = [jnp.where(vis, _dot_nt(cmp_ref[g], q_raw[g]), NEG_INF) for g in groups]
    ec = [jnp.where(vis, jnp.exp2(s - col_max(s)), 0.0) for s in sc]
    pc = [(e / jnp.maximum(col_sum(e), 1e-30)).astype(BF16) for e in ec]
    o_cmp = [_dot(cmp_ref[NSA_KV_HEADS + g], pc[g]) for g in groups]
    imp_heads = [_dot(ovt_ref[...], p) for p in pc]

    blk = _iota((NSA_NB, tq), 0)
    q_blk = (t0 + _iota((NSA_NB, tq), 1)) // NSA_SEL_BLOCK
    forced = (blk == 0) | (blk == q_blk) | (blk == q_blk - 1)
    for g in groups:
        imp = imp_heads[g][:, :tq]
        for r in range(1, rep):
            imp = imp + imp_heads[g][:, r * tq:(r + 1) * tq]
        score = jnp.where(blk <= q_blk, imp + jnp.where(forced, NSA_FORCE_BONUS, 0.0), NEG_INF)
        rank = jnp.zeros((NSA_NB, tq), F32)
        for j in range(NSA_NB):
            sj = score[j:j + 1, :]
            ahead = (sj > score) | ((sj == score) & (blk > j))
            rank = rank + jnp.where(ahead, 1.0, 0.0)
        bias = jnp.where((rank < float(NSA_SEL_TOPN)) & (score > NEG_INF * 0.5), 0.0, NEG_INF)
        bias = jnp.concatenate([bias] * rep, axis=1)
        for c in range(SEQ // kc_w):
            bias_scr[g, c] = bias[c * bpc:(c + 1) * bpc, :]

    def sel_step(kj, ss, carry):
        bs = [bias_scr[g, kj] for g in groups]
        s3 = [s.reshape(bpc, NSA_SEL_BLOCK, width) for s in ss]
        m_new = [jnp.maximum(carry[g][0], col_max(jnp.max(s3[g], axis=1) + bs[g])) for g in groups]
        alpha = [jnp.exp2(carry[g][0] - m_new[g]) for g in groups]
        ps = [jnp.exp2(s3[g] - (m_new[g] - bs[g])[:, None, :]).reshape(kc_w, width) for g in groups]
        ls = [alpha[g] * carry[g][1] + col_sum(ps[g]) for g in groups]
        pvs = [_dot(vt_of(vselt_ref, g, kj), ps[g].astype(BF16)) for g in groups]
        return tuple((m_new[g], ls[g], alpha[g] * carry[g][2] + pvs[g]) for g in groups)

    init = tuple((jnp.full((1, width), NEG_INF, F32), jnp.zeros((1, width), F32), jnp.zeros((d, width), F32))
                 for g in groups)
    carry = lax.fori_loop(
        0, i, lambda kj, c: sel_step(kj, [_dot_nt(k_of(ksel_ref, g, kj), q_rot[g]) for g in groups], c), init)

    key_row = _iota((kc_w, width), 0)
    q_col = _iota((kc_w, width), 1) & (tq - 1)
    causal = key_row <= q_col
    diag = [jnp.where(causal, _dot_nt(k_of(ksel_ref, g, i), q_rot[g]), NEG_INF) for g in groups]
    sel = sel_step(i, diag, carry)

    far = i - 2
    near = i - 1
    far_ok = jnp.where(far >= 0, 0.0, NEG_INF)
    near_ok = jnp.where(near >= 0, 0.0, NEG_INF)
    far_c = jnp.maximum(far, 0)
    near_c = jnp.maximum(near, 0)
    s_own = [jnp.where(causal, _dot_nt(k_of(kwin_ref, g, i), q_rot[g]), NEG_INF) for g in groups]
    s_near = [_dot_nt(k_of(kwin_ref, g, near_c), q_rot[g]) + near_ok for g in groups]
    s_far = [jnp.where(key_row > q_col, _dot_nt(k_of(kwin_ref, g, far_c), q_rot[g]), NEG_INF) + far_ok
             for g in groups]
    m_w = [jnp.maximum(jnp.maximum(col_max(s_own[g]), col_max(s_near[g])), col_max(s_far[g])) for g in groups]
    p_own = [jnp.exp2(s_own[g] - m_w[g]) for g in groups]
    p_near = [jnp.exp2(s_near[g] - m_w[g]) for g in groups]
    p_far = [jnp.exp2(s_far[g] - m_w[g]) for g in groups]
    l_win = [col_sum(p_own[g]) + col_sum(p_near[g]) + col_sum(p_far[g]) for g in groups]
    acc_win = [_dot(vt_of(vwint_ref, g, i), p_own[g].astype(BF16))
               + _dot(vt_of(vwint_ref, g, near_c), p_near[g].astype(BF16))
               + _dot(vt_of(vwint_ref, g, far_c), p_far[g].astype(BF16)) for g in groups]

    gates = jnp.transpose(gate_ref[...])
    for g in groups:
        o_sel = sel[g][2] / sel[g][1]
        o_win = acc_win[g] / l_win[g]
        for r in range(rep):
            lanes = slice(r * tq, (r + 1) * tq)
            row = 3 * (g * rep + r)
            o = (gates[row:row + 1, :] * o_cmp[g][:, lanes] + gates[row + 1:row + 2, :] * o_sel[:, lanes]
                 + gates[row + 2:row + 3, :] * o_win[:, lanes])
            o_ref[:, head_cols(g, r)] = jnp.transpose(o).astype(o_ref.dtype)


def nsa_attention(q_raw, q_rot, cmp_kv, k_sel, k_win, vt_sel, vt_win, gates_a, overlap_t, batch):
    tq = NSA_TQ
    nq = SEQ // tq
    g_n = NSA_KV_HEADS
    row = lambda b, i: (b * nq + i, 0)
    return pl.pallas_call(
        _nsa_kernel,
        grid=(batch, nq),
        in_specs=[pl.BlockSpec((tq, A_Q), row),
                  pl.BlockSpec((tq, A_Q), row),
                  pl.BlockSpec((None, 2 * g_n, NSA_M_PAD, HEAD_DIM), lambda b, i: (b, 0, 0, 0)),
                  pl.BlockSpec((SEQ, A_KV), lambda b, i: (b, 0)),
                  pl.BlockSpec((A_KV, SEQ), lambda b, i: (0, b)),
                  pl.BlockSpec((SEQ, A_KV), lambda b, i: (b, 0)),
                  pl.BlockSpec((A_KV, SEQ), lambda b, i: (0, b)),
                  pl.BlockSpec((tq, LANES), row),
                  pl.BlockSpec((NSA_NB, NSA_M_PAD), lambda b, i: (0, 0))],
        out_specs=pl.BlockSpec((tq, A_Q), row),
        out_shape=jax.ShapeDtypeStruct((batch * SEQ, A_Q), BF16),
        scratch_shapes=[pltpu.VMEM((g_n, SEQ // NSA_KC, NSA_BLK_PER_CHUNK, NSA_REP * tq), F32)],
        compiler_params=_params("parallel", "parallel"),
        name="nsa_attention",
    )(q_raw, q_rot, cmp_kv, k_sel, vt_sel, k_win, vt_win, gates_a, overlap_t)


def _dilated_kernel(q0, k0, v0, q1, k1, v1, q2, k2, v2, o_ref,
                    o0_scr, o1_scr, o2_scr, l0_scr, l1_scr, l2_scr):
    tile = DIL_TILE
    groups = ((q0, k0, v0, o0_scr, l0_scr), (q1, k1, v1, o1_scr, l1_scr), (q2, k2, v2, o2_scr, l2_scr))
    for (window, dil), (q_ref, k_ref, v_ref, og_scr, lg_scr) in zip(DIL_GROUPS, groups):
        assert window // dil == tile
        per_class = SEQ // dil
        tiles_per_class = per_class // tile
        nk = tile if tiles_per_class == 1 else 2 * tile
        a_minus_a = _iota((tile, nk), 0) - _iota((tile, nk), 1)

        def step(u, carry, q_ref=q_ref, k_ref=k_ref, v_ref=v_ref, og_scr=og_scr, lg_scr=lg_scr,
                 dil=dil, tiles_per_class=tiles_per_class, nk=nk, a_minus_a=a_minus_a):
            ts = [u * DIL_UNROLL + a for a in range(DIL_UNROLL)]
            cls = [t // tiles_per_class for t in ts]
            n0 = [(t % tiles_per_class) * tile for t in ts]
            kbase = [jnp.maximum(n - (nk - tile), 0) for n in n0]
            q_rows = [pl.ds(c + dil * n, tile, stride=dil) for c, n in zip(cls, n0)]
            k_rows = [pl.ds(c + dil * kb, nk, stride=dil) for c, kb in zip(cls, kbase)]
            qs = [q_ref[r, :].astype(BF16) for r in q_rows]
            ks = [k_ref[r, :].astype(BF16) for r in k_rows]
            vs = [v_ref[r, :].astype(BF16) for r in k_rows]
            ss = [_dot_nt(q, k) for q, k in zip(qs, ks)]
            masks = []
            for n, kb in zip(n0, kbase):
                dist = a_minus_a + (n - kb)
                masks.append((dist >= 0) & (dist <= tile))
            ss = [jnp.where(mk, s, NEG_INF) for mk, s in zip(masks, ss)]
            ms = [jnp.max(s, axis=-1, keepdims=True) for s in ss]
            es = [jnp.where(mk, jnp.exp2(s - m), 0.0) for mk, s, m in zip(masks, ss, ms)]
            dens = [jnp.maximum(jnp.sum(e, axis=-1, keepdims=True), 1e-30) for e in es]
            os_ = [_dot((e / den).astype(BF16), v) for e, den, v in zip(es, dens, vs)]
            for r, o, m, den in zip(q_rows, os_, ms, dens):
                og_scr[r, :] = o
                lg_scr[r, :] = jnp.broadcast_to(m + jnp.log2(den), (tile, HEAD_DIM))
            return carry

        lax.fori_loop(0, SEQ // tile // DIL_UNROLL, step, 0)

    rows = 256

    def merge_body(c, carry):
        sl = pl.ds(pl.multiple_of(c * rows, rows), rows)
        la, lb, lc = l0_scr[sl, :], l1_scr[sl, :], l2_scr[sl, :]
        mx = jnp.maximum(jnp.maximum(la, lb), lc)
        ea, eb, ec = jnp.exp2(la - mx), jnp.exp2(lb - mx), jnp.exp2(lc - mx)
        tot = ea + eb + ec
        out = (ea / tot) * o0_scr[sl, :] + (eb / tot) * o1_scr[sl, :] + (ec / tot) * o2_scr[sl, :]
        o_ref[sl, :] = out.astype(o_ref.dtype)
        return carry

    lax.fori_loop(0, SEQ // rows, merge_body, 0)


def dilated_attention(b_qk, b_v, batch):
    hp = DIL_HEADS_PER_GROUP
    in_specs, args = [], []
    for g in range(len(DIL_GROUPS)):
        for arr, c0 in ((b_qk, 0), (b_qk, DIL_HEADS), (b_v, 0)):
            in_specs.append(pl.BlockSpec((SEQ, HEAD_DIM), lambda b, j, c0=c0, g=g: (b, c0 + g * hp + j)))
            args.append(arr)
    return pl.pallas_call(
        _dilated_kernel,
        grid=(batch, hp),
        in_specs=in_specs,
        out_specs=pl.BlockSpec((SEQ, HEAD_DIM), lambda b, j: (b, j)),
        out_shape=jax.ShapeDtypeStruct((batch * SEQ, hp * HEAD_DIM), BF16),
        scratch_shapes=[pltpu.VMEM((SEQ, HEAD_DIM), F32)] * 6,
        compiler_params=_params("parallel", "parallel"),
        name="dilated_attention",
    )(*args)


def _moba_kernel(q_ref, k_ref, vt_ref, o_ref, kmean_scr, bias_scr):
    i = pl.program_id(1)
    tq, nb, d = MOBA_BLOCK, MOBA_NB, HEAD_DIM
    heads = range(MOBA_HEADS)
    col = lambda h: slice(h * d, (h + 1) * d)

    @pl.when(i == 0)
    def _():
        avg = jnp.where(_iota((nb, SEQ), 1) // MOBA_BLOCK == _iota((nb, SEQ), 0), 1.0 / MOBA_BLOCK, 0.0)
        kmean_scr[...] = _dot(avg.astype(BF16), k_ref[...]).astype(kmean_scr.dtype)

    blk = _iota((nb, tq), 0)
    own = pl.multiple_of(i * tq, tq)
    causal = _iota((tq, tq), 0) <= _iota((tq, tq), 1)
    qs = [q_ref[:, col(h)] for h in heads]
    gates = [jnp.where(blk < i, _dot_nt(kmean_scr[:, col(h)], qs[h]), NEG_INF) for h in heads]
    ss = [_dot_nt(k_ref[pl.ds(own, tq), col(h)], qs[h]) for h in heads]
    for h in heads:
        gate = gates[h]
        rank = jnp.zeros((nb, tq), F32)
        for j in range(nb):
            gj = gate[j:j + 1, :]
            ahead = (gj > gate) | ((gj == gate) & (blk > j))
            rank = rank + jnp.where(ahead, 1.0, 0.0)
        bias_scr[h] = jnp.where((rank < float(MOBA_TOPK)) & (gate > NEG_INF * 0.5), 0.0, NEG_INF)
    ss = [jnp.where(causal, s, NEG_INF) for s in ss]
    ms = [jnp.max(s, axis=0, keepdims=True) for s in ss]
    ps = [jnp.exp2(s - m) for s, m in zip(ss, ms)]
    ls = [jnp.sum(p, axis=0, keepdims=True) for p in ps]
    accs = [_dot(vt_ref[col(h), pl.ds(own, tq)], ps[h].astype(BF16)) for h in heads]

    def body(kj, carry):
        off = pl.multiple_of(kj * tq, tq)
        bs = [bias_scr[h, pl.ds(kj, 1), :] for h in heads]
        ss = [_dot_nt(k_ref[pl.ds(off, tq), col(h)], qs[h]) for h in heads]
        m_new = [jnp.maximum(carry[h][0], jnp.max(ss[h], axis=0, keepdims=True) + bs[h]) for h in heads]
        alpha = [jnp.exp2(carry[h][0] - m_new[h]) for h in heads]
        ps = [jnp.exp2(ss[h] - (m_new[h] - bs[h])) for h in heads]
        ls = [alpha[h] * carry[h][1] + jnp.sum(ps[h], axis=0, keepdims=True) for h in heads]
        pvs = [_dot(vt_ref[col(h), pl.ds(off, tq)], ps[h].astype(BF16)) for h in heads]
        return tuple((m_new[h], ls[h], alpha[h] * carry[h][2] + pvs[h]) for h in heads)

    fin = lax.fori_loop(0, i, body, tuple((ms[h], ls[h], accs[h]) for h in heads))
    for h in heads:
        _, l, acc = fin[h]
        o_ref[:, col(h)] = jnp.transpose(acc / l).astype(o_ref.dtype)


def moba_attention(c_qk, vt, batch):
    tq = MOBA_BLOCK
    nq = SEQ // tq
    return pl.pallas_call(
        _moba_kernel,
        grid=(batch, nq),
        in_specs=[pl.BlockSpec((tq, C_QKV), lambda b, i: (b * nq + i, 0)),
                  pl.BlockSpec((SEQ, C_QKV), lambda b, i: (b, 1)),
                  pl.BlockSpec((C_QKV, SEQ), lambda b, i: (0, b))],
        out_specs=pl.BlockSpec((tq, C_QKV), lambda b, i: (b * nq + i, 0)),
        out_shape=jax.ShapeDtypeStruct((batch * SEQ, C_QKV), BF16),
        scratch_shapes=[pltpu.VMEM((MOBA_NB, C_QKV), BF16), pltpu.VMEM((MOBA_HEADS, MOBA_NB, tq), F32)],
        compiler_params=_params("parallel", "arbitrary"),
        name="moba_attention",
    )(c_qk, c_qk, vt)


def _merge_kernel(oa_ref, ob_ref, oc_ref, ga_ref, gb_ref, gc_ref, wa_ref, wb_ref, wc_ref, out_ref):
    y = ga_ref[...] * _dot(oa_ref[...], wa_ref[...])
    y = y + gb_ref[...] * _dot(ob_ref[...], wb_ref[...])
    y = y + gc_ref[...] * _dot(oc_ref[...], wc_ref[...])
    out_ref[...] = y.astype(out_ref.dtype)


def gated_merge(o_a, o_b, o_c, gates_m, w_a, w_b, w_c, layer, tm=1024, tn=512):
    m = o_a.shape[0]
    nb = D_MODEL // tn
    act = lambda width: pl.BlockSpec((tm, width), lambda j, i: (i, 0))
    gate = lambda g: pl.BlockSpec((tm, tn), lambda j, i, g=g: (i, g * nb + j))
    wgt = lambda width: pl.BlockSpec((None, width, tn), lambda j, i: (layer, 0, j))
    return pl.pallas_call(
        _merge_kernel,
        grid=(nb, m // tm),
        in_specs=[act(o_a.shape[1]), act(o_b.shape[1]), act(o_c.shape[1]),
                  gate(0), gate(1), gate(2),
                  wgt(w_a.shape[1]), wgt(w_b.shape[1]), wgt(w_c.shape[1])],
        out_specs=pl.BlockSpec((tm, tn), lambda j, i: (i, j)),
        out_shape=jax.ShapeDtypeStruct((m, D_MODEL), BF16),
        compiler_params=_params("parallel", "parallel"),
        name="gated_merge",
    )(o_a, o_b, o_c, gates_m, gates_m, gates_m, w_a, w_b, w_c)


def _out_proj_kernel(y_ref, w_ref, x_ref, g_ref, xo_ref, h_ref):
    x_new = x_ref[...] + _dot(y_ref[...], w_ref[...])
    xo_ref[...] = x_new
    h_ref[...] = _rmsnorm_rows(x_new, g_ref[...]).astype(h_ref.dtype)


def out_proj_residual_norm(y, w_o, layer, x2, g, tm=512):
    m, d = x2.shape
    row = pl.BlockSpec((tm, d), lambda i: (i, 0))
    return pl.pallas_call(
        _out_proj_kernel,
        grid=(m // tm,),
        in_specs=[row, pl.BlockSpec((None, d, d), lambda i: (layer, 0, 0)), row,
                  pl.BlockSpec((1, d), lambda i: (0, 0))],
        out_specs=[row, row],
        out_shape=[jax.ShapeDtypeStruct((m, d), F32), jax.ShapeDtypeStruct((m, d), BF16)],
        compiler_params=_params("parallel"),
        name="out_proj",
    )(y, w_o, x2, g.reshape(1, d))


def _mlp_kernel(h_ref, w1_ref, w2_ref, x_ref, g_ref, *rest):
    xo_ref = rest[0] if len(rest) == 3 else None
    hn_ref, acc_scr = rest[-2:]
    f = pl.program_id(1)

    @pl.when(f == 0)
    def _():
        acc_scr[...] = jnp.zeros_like(acc_scr)

    u = jnp.square(jnp.maximum(_dot(h_ref[...], w1_ref[...]), 0.0))
    acc_scr[...] += _dot(u.astype(BF16), w2_ref[...].astype(BF16))

    @pl.when(f == pl.num_programs(1) - 1)
    def _():
        x_new = x_ref[...] + acc_scr[...]
        if xo_ref is not None:
            xo_ref[...] = x_new
        hn_ref[...] = _rmsnorm_rows(x_new, g_ref[...]).astype(hn_ref.dtype)


def mlp_residual_norm(h2, w1, w2, layer, x2, g_next, next_dtype, emit_x, tm=1024, tf=512):
    m, d = x2.shape
    row = pl.BlockSpec((tm, d), lambda i, f: (i, 0))
    row_once = pl.BlockSpec((tm, d), lambda i, f: (i, 0), pipeline_mode=pl.Buffered(1))
    out_shape = [jax.ShapeDtypeStruct((m, d), F32)] * emit_x + [jax.ShapeDtypeStruct((m, d), next_dtype)]
    res = pl.pallas_call(
        _mlp_kernel,
        grid=(m // tm, D_FF // tf),
        in_specs=[row, pl.BlockSpec((None, d, tf), lambda i, f: (layer, 0, f)),
                  pl.BlockSpec((None, tf, d), lambda i, f: (layer, f, 0)),
                  row_once, pl.BlockSpec((1, d), lambda i, f: (0, 0))],
        out_specs=[row_once] * len(out_shape),
        out_shape=out_shape,
        scratch_shapes=[pltpu.VMEM((tm, d), F32)],
        compiler_params=_params("parallel", "arbitrary"),
        name="mlp",
    )(h2, w1, w2, x2, g_next.reshape(1, d))
    return (res[0], res[1]) if emit_x else (None, res[0])


def _rope_tables():
    inv = ROPE_THETA ** (-jnp.arange(0, ROPE_DIM, 2, dtype=F32) / ROPE_DIM)
    ang = jnp.arange(SEQ, dtype=F32)[:, None] * inv[None, :]
    cos, sin = jnp.cos(ang), jnp.sin(ang)
    zeros = jnp.zeros((SEQ, HEAD_DIM - ROPE_DIM), F32)
    zero_h = jnp.zeros((SEQ, ROPE_HALF), F32)
    c = jnp.concatenate([cos, cos, jnp.ones_like(zeros)], axis=1)
    s_up = jnp.concatenate([-sin, zero_h, zeros], axis=1)
    s_dn = jnp.concatenate([zero_h, sin, zeros], axis=1)
    return c, s_up, s_dn


def _overlap_table_t():
    cs = np.arange(NSA_M_PAD)[None, :] * NSA_CMP_STRIDE
    bs = np.arange(NSA_NB)[:, None] * NSA_SEL_BLOCK
    ov = np.clip(np.minimum(cs + NSA_CMP_LEN, bs + NSA_SEL_BLOCK) - np.maximum(cs, bs), 0, None) / NSA_CMP_LEN
    ov[:, NSA_M_PAD - 1] = 0.0
    return jnp.asarray(ov, dtype=BF16)


W_IN_NAMES = ("a_q", "a_kc", "a_vc", "a_ks", "a_vs", "a_kw", "a_vw", "a_g",
              "b_q", "b_k", "b_v", "c_q", "c_k", "c_v", "m_a", "m_b", "m_c")
W_IN_START = dict(zip(W_IN_NAMES, np.cumsum((0,) + IN_SPLIT_SIZES[:-1]).tolist()))
W_IN_SIZE = dict(zip(W_IN_NAMES, IN_SPLIT_SIZES))


W_GATE_AT = W_IN_START["a_g"]


def _cast_rows_kernel(src_ref, out_ref, *, keep):
    x = src_ref[0]
    if keep < x.shape[0]:
        x = jnp.where(_iota(x.shape, 0) < keep, x, 0.0)
    out_ref[...] = x.astype(out_ref.dtype)


def cast_rows(wt, row0, n_rows, tr, keep=None):
    depth, _, k = wt.shape
    return pl.pallas_call(
        functools.partial(_cast_rows_kernel, keep=tr if keep is None else keep),
        grid=(depth, n_rows // tr),
        in_specs=[pl.BlockSpec((pl.Element(1), pl.Element(tr), pl.Element(k)),
                               lambda l, r: (l, pl.multiple_of(row0 + r * tr, F32_SUBLANES), 0))],
        out_specs=pl.BlockSpec((None, tr, k), lambda l, r: (l, r, 0)),
        out_shape=jax.ShapeDtypeStruct((depth, n_rows, k), BF16),
        compiler_params=_params("parallel", "parallel"),
        name="cast_rows",
    )(wt)


def bf16_w_in_t(w_in):
    wt = jnp.transpose(w_in, (0, 2, 1))
    hi0 = W_GATE_AT + A_G
    lo = cast_rows(wt, 0, W_GATE_AT, 512)
    hi = cast_rows(wt, hi0, wt.shape[1] - hi0, 512)
    ag = cast_rows(wt, W_GATE_AT, LANES, LANES, keep=A_G)
    where = {"a_g": (ag, 0)}
    for name in W_IN_NAMES:
        if W_IN_START[name] < W_GATE_AT:
            where[name] = (lo, W_IN_START[name])
        elif W_IN_START[name] > W_GATE_AT:
            where[name] = (hi, W_IN_START[name] - hi0)
    return where


def _scales(*widths_and_values):
    return jnp.concatenate([jnp.full((1, w), v, F32) for w, v in widths_and_values], axis=1)


def _layer(x2, h, batch, layer, tabs, overlap_t, w_at, pe_k, w1_k, w2_k, pe_v, w1_v, w2_v,
           w_br_a, w_br_b, w_br_c, w_o, mlp_g, w_mlp_in, w_mlp_out, g_next, last):
    def proj(first, n, *args):
        piece, col0 = w_at[first]
        return project(h, piece, layer, col0, n, *args)

    def proj_t(name):
        piece, col0 = w_at[name]
        return project_transposed(h, piece, layer, col0, W_IN_SIZE[name], BF16)

    q_raw, q_rot = proj("a_q", A_Q, "both", BF16, tabs, _scales((A_Q, Q_SCALE)))
    cmp_src = proj("a_kc", 2 * A_KV, "plain", F32)
    k_sel = proj("a_ks", A_KV, "rope", BF16, tabs, _scales((A_KV, 1.0)))
    k_win = proj("a_kw", A_KV, "rope", BF16, tabs, _scales((A_KV, 1.0)))
    b_qk = proj("b_q", 2 * B_QKV, "rope", F32, tabs, _scales((B_QKV, Q_SCALE), (B_QKV, 1.0)))
    b_v = proj("b_v", B_QKV, "plain", F32)
    c_qk = proj("c_q", 2 * C_QKV, "rope", BF16, tabs, _scales((C_QKV, Q_SCALE), (C_QKV, 1.0)))
    gates_m = proj("m_a", 3 * D_MODEL, "sigmoid", BF16)
    gates_a = proj("a_g", LANES, "sigmoid", F32)
    vt_sel, vt_win, vt_moba = proj_t("a_vs"), proj_t("a_vw"), proj_t("c_v")

    flat = NSA_CMP_LEN * HEAD_DIM
    cmp_kv = nsa_compress(cmp_src, pe_k.reshape(1, flat), w1_k.reshape(flat, NSA_CMP_HIDDEN).astype(BF16),
                          w2_k.astype(BF16), pe_v.reshape(1, flat),
                          w1_v.reshape(flat, NSA_CMP_HIDDEN).astype(BF16), w2_v.astype(BF16), batch)
    o_a = nsa_attention(q_raw, q_rot, cmp_kv, k_sel, k_win, vt_sel, vt_win, gates_a, overlap_t, batch)
    o_b = dilated_attention(b_qk, b_v, batch)
    o_c = moba_attention(c_qk, vt_moba, batch)

    merged = gated_merge(o_a, o_b, o_c, gates_m, w_br_a, w_br_b, w_br_c, layer)
    x2, h2 = out_proj_residual_norm(merged, w_o, layer, x2, mlp_g)
    return mlp_residual_norm(h2, w_mlp_in, w_mlp_out, layer, x2, g_next, F32 if last else BF16, emit_x=not last)


def kernel(x, attn_norm_g, w_in, cmp_pe_k, cmp_w1_k, cmp_w2_k, cmp_pe_v, cmp_w1_v, cmp_w2_v,
           w_br_a, w_br_b, w_br_c, w_o, mlp_norm_g, w_mlp_in, w_mlp_out, final_norm_g):
    batch, seq, d = x.shape
    assert seq == SEQ and d == D_MODEL
    depth = w_in.shape[0]
    tabs = _rope_tables()
    overlap_t = _overlap_table_t()
    w_at = bf16_w_in_t(w_in)
    w_br_a, w_br_b, w_br_c, w_o = (t.astype(BF16) for t in (w_br_a, w_br_b, w_br_c, w_o))
    w_mlp_in = w_mlp_in.astype(BF16)
    x2 = x.reshape(batch * seq, d)
    h = rmsnorm(x2, attn_norm_g[0], BF16)
    for l in range(depth):
        last = l == depth - 1
        g_next = final_norm_g if last else attn_norm_g[l + 1]
        x2, h = _layer(x2, h, batch, l, tabs, overlap_t, w_at,
                       cmp_pe_k[l], cmp_w1_k[l], cmp_w2_k[l], cmp_pe_v[l], cmp_w1_v[l], cmp_w2_v[l],
                       w_br_a, w_br_b, w_br_c, w_o, mlp_norm_g[l], w_mlp_in, w_mlp_out, g_next, last)
    return h.reshape(batch, seq, d)
```

```python
import functools
import math

import numpy as np
import jax
import jax.numpy as jnp
from jax import lax
from jax.experimental import pallas as pl
from jax.experimental.pallas import tpu as pltpu

D_MODEL = 2048
SEQ = 2048
HEAD_DIM = 128
ROPE_THETA = 500000.0
ROPE_DIM = HEAD_DIM // 4
ROPE_HALF = ROPE_DIM // 2
NORM_EPS = 1e-6
NEG_INF = -1e30
Q_SCALE = HEAD_DIM ** -0.5 * math.log2(math.e)

NSA_HEADS = 8
NSA_KV_HEADS = 2
NSA_REP = NSA_HEADS // NSA_KV_HEADS
NSA_CMP_LEN = 32
NSA_CMP_STRIDE = 16
NSA_CMP_HIDDEN = 256
NSA_SEL_BLOCK = 64
NSA_SEL_TOPN = 16
NSA_WINDOW = 512
NSA_FORCE_BONUS = 1e4
NSA_NB = SEQ // NSA_SEL_BLOCK
NSA_M_PAD = SEQ // NSA_CMP_STRIDE

DIL_GROUPS = ((128, 1), (512, 4), (2048, 16))
DIL_HEADS_PER_GROUP = 4
DIL_HEADS = DIL_HEADS_PER_GROUP * len(DIL_GROUPS)
DIL_TILE = 128
DIL_UNROLL = 16

MOBA_HEADS = 8
MOBA_BLOCK = 256
MOBA_TOPK = 3
MOBA_NB = SEQ // MOBA_BLOCK

D_FF = 4 * D_MODEL
A_Q = NSA_HEADS * HEAD_DIM
A_KV = NSA_KV_HEADS * HEAD_DIM
A_G = 3 * NSA_HEADS
B_QKV = DIL_HEADS * HEAD_DIM
C_QKV = MOBA_HEADS * HEAD_DIM
IN_SPLIT_SIZES = (A_Q, A_KV, A_KV, A_KV, A_KV, A_KV, A_KV, A_G,
                  B_QKV, B_QKV, B_QKV, C_QKV, C_QKV, C_QKV,
                  D_MODEL, D_MODEL, D_MODEL)

LANES = 128
F32_SUBLANES = 8
PROJ_TN_MAX = 512
PROJ_VMEM_BUDGET = 46 * 1024 * 1024
VMEM_LIMIT = 60 * 1024 * 1024

BF16 = jnp.bfloat16
F32 = jnp.float32


def _params(*sem):
    return pltpu.CompilerParams(dimension_semantics=sem, vmem_limit_bytes=VMEM_LIMIT)


def _dot(a, b):
    return jnp.dot(a, b, preferred_element_type=F32)


def _dot_nt(a, b):
    return lax.dot_general(a, b, (((1,), (1,)), ((), ())), preferred_element_type=F32)


def _iota(shape, axis):
    return lax.broadcasted_iota(jnp.int32, shape, axis)


def _rmsnorm_rows(x, g):
    y = x * lax.rsqrt(jnp.mean(x * x, axis=-1, keepdims=True) + NORM_EPS)
    return y * g


def _rmsnorm_kernel(x_ref, g_ref, h_ref):
    h_ref[...] = _rmsnorm_rows(x_ref[...], g_ref[...]).astype(h_ref.dtype)


def rmsnorm(x2, g, out_dtype, tm=512):
    m, d = x2.shape
    return pl.pallas_call(
        _rmsnorm_kernel,
        grid=(m // tm,),
        in_specs=[pl.BlockSpec((tm, d), lambda i: (i, 0)), pl.BlockSpec((1, d), lambda i: (0, 0))],
        out_specs=pl.BlockSpec((tm, d), lambda i: (i, 0)),
        out_shape=jax.ShapeDtypeStruct((m, d), out_dtype),
        compiler_params=_params("parallel"),
        name="rmsnorm",
    )(x2, g.reshape(1, d))


def _rope_lanes(acc, c, s_up, s_dn):
    tn = acc.shape[1]
    reps = tn // HEAD_DIM
    if reps > 1:
        c = jnp.concatenate([c] * reps, axis=1)
        s_up = jnp.concatenate([s_up] * reps, axis=1)
        s_dn = jnp.concatenate([s_dn] * reps, axis=1)
    up = pltpu.roll(acc, tn - ROPE_HALF, axis=1)
    dn = pltpu.roll(acc, ROPE_HALF, axis=1)
    return acc * c + up * s_up + dn * s_dn


def _proj_kernel(*refs, mode):
    if mode in ("rope", "both"):
        h_ref, w_ref, c_ref, su_ref, sd_ref, cs_ref = refs[:6]
        outs = refs[6:]
    else:
        h_ref, w_ref = refs[:2]
        outs = refs[2:]
    acc = _dot_nt(h_ref[...], w_ref[...])
    if mode == "plain":
        outs[0][...] = acc.astype(outs[0].dtype)
    elif mode == "sigmoid":
        outs[0][...] = jax.nn.sigmoid(acc).astype(outs[0].dtype)
    else:
        col_scale = cs_ref[...]
        roped = _rope_lanes(acc, c_ref[...], su_ref[...], sd_ref[...]) * col_scale
        if mode == "both":
            outs[0][...] = (acc * col_scale).astype(outs[0].dtype)
            outs[1][...] = roped.astype(outs[1].dtype)
        else:
            outs[0][...] = roped.astype(outs[0].dtype)


def _proj_tiles(k, n, col0, out_bytes, n_out, rope):
    tn = math.gcd(math.gcd(PROJ_TN_MAX, n), col0) if col0 else math.gcd(PROJ_TN_MAX, n)
    for tm in (SEQ, SEQ // 2):
        blocks = tm * k * 2 + k * tn * 2 + n_out * tm * tn * out_bytes + (3 * tm * HEAD_DIM * 4 if rope else 0)
        if 2 * blocks <= PROJ_VMEM_BUDGET:
            break
    return tm, tn


def project(h, wt, layer, col0, n, mode, out_dtype, rope_tabs=None, col_scale=None):
    m, k = h.shape
    n_out = 2 if mode == "both" else 1
    rope = mode in ("rope", "both")
    tm, tn = _proj_tiles(k, n, col0, jnp.dtype(out_dtype).itemsize, n_out, rope)
    assert m % tm == 0 and n % tn == 0 and col0 % tn == 0 and SEQ % tm == 0
    j0 = col0 // tn
    in_specs = [pl.BlockSpec((tm, k), lambda i, j: (i, 0)),
                pl.BlockSpec((None, tn, k), lambda i, j: (layer, j0 + j, 0))]
    args = [h, wt]
    if rope:
        pos_blocks = SEQ // tm
        for t in rope_tabs:
            in_specs.append(pl.BlockSpec((tm, HEAD_DIM), lambda i, j: (i % pos_blocks, 0)))
            args.append(t)
        in_specs.append(pl.BlockSpec((1, tn), lambda i, j: (0, j)))
        args.append(col_scale)
    out_spec = pl.BlockSpec((tm, tn), lambda i, j: (i, j))
    out_shape = jax.ShapeDtypeStruct((m, n), out_dtype)
    res = pl.pallas_call(
        functools.partial(_proj_kernel, mode=mode),
        grid=(m // tm, n // tn),
        in_specs=in_specs,
        out_specs=[out_spec] * n_out,
        out_shape=[out_shape] * n_out,
        compiler_params=_params("parallel", "parallel"),
        name="proj_" + mode,
    )(*args)
    return res if n_out == 2 else res[0]


def _proj_t_kernel(wt_ref, h_ref, out_ref):
    out_ref[...] = _dot_nt(wt_ref[...], h_ref[...]).astype(out_ref.dtype)


def project_transposed(h, wt, layer, col0, n, out_dtype, tm=1024):
    m, k = h.shape
    tn = math.gcd(math.gcd(PROJ_TN_MAX, n), col0) if col0 else math.gcd(PROJ_TN_MAX, n)
    j0 = col0 // tn
    return pl.pallas_call(
        _proj_t_kernel,
        grid=(n // tn, m // tm),
        in_specs=[pl.BlockSpec((None, tn, k), lambda j, i: (layer, j0 + j, 0)),
                  pl.BlockSpec((tm, k), lambda j, i: (i, 0))],
        out_specs=pl.BlockSpec((tn, tm), lambda j, i: (j, i)),
        out_shape=jax.ShapeDtypeStruct((n, m), out_dtype),
        compiler_params=_params("parallel", "parallel"),
        name="proj_transposed",
    )(wt, h)


def _compress_kernel(k0_ref, k1_ref, v0_ref, v1_ref, pek_ref, w1k_ref, w2k_ref, pev_ref, w1v_ref, w2v_ref,
                     out_ref):
    half = NSA_CMP_STRIDE * HEAD_DIM
    for idx, src_ref in enumerate((k0_ref, k1_ref, v0_ref, v1_ref)):
        is_k = idx < NSA_KV_HEADS
        pe_ref, w1_ref, w2_ref = (pek_ref, w1k_ref, w2k_ref) if is_k else (pev_ref, w1v_ref, w2v_ref)
        x = jnp.concatenate(
            [src_ref[pl.ds(l, NSA_M_PAD, stride=NSA_CMP_STRIDE), :]
             for l in range(NSA_CMP_STRIDE)], axis=1)
        pe = pe_ref[...]
        first = _dot((x + pe[:, :half]).astype(BF16), w1_ref[:half, :])
        second = _dot((x + pe[:, half:]).astype(BF16), w1_ref[half:, :])
        hid = jax.nn.gelu(first + pltpu.roll(second, NSA_M_PAD - 1, axis=0))
        out = _dot(hid.astype(BF16), w2_ref[...])
        out_ref[idx] = (out if is_k else jnp.transpose(out)).astype(out_ref.dtype)


def nsa_compress(pf, pe_k, w1_k, w2_k, pe_v, w1_v, w2_v, batch):
    flat = NSA_CMP_LEN * HEAD_DIM
    const = lambda shape: pl.BlockSpec(shape, lambda b: (0,) * len(shape))
    return pl.pallas_call(
        _compress_kernel,
        grid=(batch,),
        in_specs=[pl.BlockSpec((SEQ, HEAD_DIM), lambda b, c=c: (b, c)) for c in range(4)] + [
                  const((1, flat)), const((flat, NSA_CMP_HIDDEN)), const((NSA_CMP_HIDDEN, HEAD_DIM)),
                  const((1, flat)), const((flat, NSA_CMP_HIDDEN)), const((NSA_CMP_HIDDEN, HEAD_DIM))],
        out_specs=pl.BlockSpec((None, 4, NSA_M_PAD, HEAD_DIM), lambda b: (b, 0, 0, 0)),
        out_shape=jax.ShapeDtypeStruct((batch, 4, NSA_M_PAD, HEAD_DIM), BF16),
        compiler_params=_params("parallel"),
        name="nsa_compress",
    )(pf, pf, pf, pf, pe_k, w1_k, w2_k, pe_v, w1_v, w2_v)


NSA_TQ = 256
NSA_KC = 256
NSA_BLK_PER_CHUNK = NSA_KC // NSA_SEL_BLOCK


def _nsa_kernel(qraw_ref, qrot_ref, cmp_ref, ksel_ref, vselt_ref, kwin_ref, vwint_ref,
                gate_ref, ovt_ref, o_ref, bias_scr):
    i = pl.program_id(1)
    tq, kc_w, rep, d = NSA_TQ, NSA_KC, NSA_REP, HEAD_DIM
    groups = range(NSA_KV_HEADS)
    width = rep * tq
    bpc = NSA_BLK_PER_CHUNK
    t0 = i * tq
    head_cols = lambda g, r: slice((g * rep + r) * d, (g * rep + r + 1) * d)
    stack = lambda ref, g: jnp.concatenate([ref[:, head_cols(g, r)] for r in range(rep)], axis=0)
    q_raw = [stack(qraw_ref, g) for g in groups]
    q_rot = [stack(qrot_ref, g) for g in groups]
    k_of = lambda ref, g, kj: ref[pl.ds(pl.multiple_of(kj * kc_w, kc_w), kc_w), g * d:(g + 1) * d]
    vt_of = lambda ref, g, kj: ref[g * d:(g + 1) * d, pl.ds(pl.multiple_of(kj * kc_w, kc_w), kc_w)]
    col_max = lambda s: jnp.max(s, axis=0, keepdims=True)
    col_sum = lambda p: jnp.sum(p, axis=0, keepdims=True)

    q_pos = t0 + (_iota((NSA_M_PAD, width), 1) & (tq - 1))
    vis = (_iota((NSA_M_PAD, width), 0) * NSA_CMP_STRIDE + (NSA_CMP_LEN - 1)) <= q_pos
    sc = [jnp.where(vis, _dot_nt(cmp_ref[g], q_raw[g]), NEG_INF) for g in groups]
    ec = [jnp.where(vis, jnp.exp2(s - col_max(s)), 0.0) for s in sc]
    pc = [(e / jnp.maximum(col_sum(e), 1e-30)).astype(BF16) for e in ec]
    o_cmp = [_dot(cmp_ref[NSA_KV_HEADS + g], pc[g]) for g in groups]
    imp_heads = [_dot(ovt_ref[...], p) for p in pc]

    blk = _iota((NSA_NB, tq), 0)
    q_blk = (t0 + _iota((NSA_NB, tq), 1)) // NSA_SEL_BLOCK
    forced = (blk == 0) | (blk == q_blk) | (blk == q_blk - 1)
    for g in groups:
        imp = imp_heads[g][:, :tq]
        for r in range(1, rep):
            imp = imp + imp_heads[g][:, r * tq:(r + 1) * tq]
        score = jnp.where(blk <= q_blk, imp + jnp.where(forced, NSA_FORCE_BONUS, 0.0), NEG_INF)
        rank = jnp.zeros((NSA_NB, tq), F32)
        for j in range(NSA_NB):
            sj = score[j:j + 1, :]
            ahead = (sj > score) | ((sj == score) & (blk > j))
            rank = rank + jnp.where(ahead, 1.0, 0.0)
        bias = jnp.where((rank < float(NSA_SEL_TOPN)) & (score > NEG_INF * 0.5), 0.0, NEG_INF)
        bias = jnp.concatenate([bias] * rep, axis=1)
        for c in range(SEQ // kc_w):
            bias_scr[g, c] = bias[c * bpc:(c + 1) * bpc, :]

    def sel_step(kj, ss, carry):
        bs = [bias_scr[g, kj] for g in groups]
        s3 = [s.reshape(bpc, NSA_SEL_BLOCK, width) for s in ss]
        m_new = [jnp.maximum(carry[g][0], col_max(jnp.max(s3[g], axis=1) + bs[g])) for g in groups]
        alpha = [jnp.exp2(carry[g][0] - m_new[g]) for g in groups]
        ps = [jnp.exp2(s3[g] - (m_new[g] - bs[g])[:, None, :]).reshape(kc_w, width) for g in groups]
        ls = [alpha[g] * carry[g][1] + col_sum(ps[g]) for g in groups]
        pvs = [_dot(vt_of(vselt_ref, g, kj), ps[g].astype(BF16)) for g in groups]
        return tuple((m_new[g], ls[g], alpha[g] * carry[g][2] + pvs[g]) for g in groups)

    init = tuple((jnp.full((1, width), NEG_INF, F32), jnp.zeros((1, width), F32), jnp.zeros((d, width), F32))
                 for g in groups)
    carry = lax.fori_loop(
        0, i, lambda kj, c: sel_step(kj, [_dot_nt(k_of(ksel_ref, g, kj), q_rot[g]) for g in groups], c), init)

    key_row = _iota((kc_w, width), 0)
    q_col = _iota((kc_w, width), 1) & (tq - 1)
    causal = key_row <= q_col
    diag = [jnp.where(causal, _dot_nt(k_of(ksel_ref, g, i), q_rot[g]), NEG_INF) for g in groups]
    sel = sel_step(i, diag, carry)

    far = i - 2
    near = i - 1
    far_ok = jnp.where(far >= 0, 0.0, NEG_INF)
    near_ok = jnp.where(near >= 0, 0.0, NEG_INF)
    far_c = jnp.maximum(far, 0)
    near_c = jnp.maximum(near, 0)
    s_own = [jnp.where(causal, _dot_nt(k_of(kwin_ref, g, i), q_rot[g]), NEG_INF) for g in groups]
    s_near = [_dot_nt(k_of(kwin_ref, g, near_c), q_rot[g]) + near_ok for g in groups]
    s_far = [jnp.where(key_row > q_col, _dot_nt(k_of(kwin_ref, g, far_c), q_rot[g]), NEG_INF) + far_ok
             for g in groups]
    m_w = [jnp.maximum(jnp.maximum(col_max(s_own[g]), col_max(s_near[g])), col_max(s_far[g])) for g in groups]
    p_own = [jnp.exp2(s_own[g] - m_w[g]) for g in groups]
    p_near = [jnp.exp2(s_near[g] - m_w[g]) for g in groups]
    p_far = [jnp.exp2(s_far[g] - m_w[g]) for g in groups]
    l_win = [col_sum(p_own[g]) + col_sum(p_near[g]) + col_sum(p_far[g]) for g in groups]
    acc_win = [_dot(vt_of(vwint_ref, g, i), p_own[g].astype(BF16))
               + _dot(vt_of(vwint_ref, g, near_c), p_near[g].astype(BF16))
               + _dot(vt_of(vwint_ref, g, far_c), p_far[g].astype(BF16)) for g in groups]

    gates = jnp.transpose(gate_ref[...])
    for g in groups:
        o_sel = sel[g][2] / sel[g][1]
        o_win = acc_win[g] / l_win[g]
        for r in range(rep):
            lanes = slice(r * tq, (r + 1) * tq)
            row = 3 * (g * rep + r)
            o = (gates[row:row + 1, :] * o_cmp[g][:, lanes] + gates[row + 1:row + 2, :] * o_sel[:, lanes]
                 + gates[row + 2:row + 3, :] * o_win[:, lanes])
            o_ref[:, head_cols(g, r)] = jnp.transpose(o).astype(o_ref.dtype)


def nsa_attention(q_raw, q_rot, cmp_kv, k_sel, k_win, vt_sel, vt_win, gates_a, overlap_t, batch):
    tq = NSA_TQ
    nq = SEQ // tq
    g_n = NSA_KV_HEADS
    row = lambda b, i: (b * nq + i, 0)
    return pl.pallas_call(
        _nsa_kernel,
        grid=(batch, nq),
        in_specs=[pl.BlockSpec((tq, A_Q), row),
                  pl.BlockSpec((tq, A_Q), row),
                  pl.BlockSpec((None, 2 * g_n, NSA_M_PAD, HEAD_DIM), lambda b, i: (b, 0, 0, 0)),
                  pl.BlockSpec((SEQ, A_KV), lambda b, i: (b, 0)),
                  pl.BlockSpec((A_KV, SEQ), lambda b, i: (0, b)),
                  pl.BlockSpec((SEQ, A_KV), lambda b, i: (b, 0)),
                  pl.BlockSpec((A_KV, SEQ), lambda b, i: (0, b)),
                  pl.BlockSpec((tq, LANES), row),
                  pl.BlockSpec((NSA_NB, NSA_M_PAD), lambda b, i: (0, 0))],
        out_specs=pl.BlockSpec((tq, A_Q), row),
        out_shape=jax.ShapeDtypeStruct((batch * SEQ, A_Q), BF16),
        scratch_shapes=[pltpu.VMEM((g_n, SEQ // NSA_KC, NSA_BLK_PER_CHUNK, NSA_REP * tq), F32)],
        compiler_params=_params("parallel", "parallel"),
        name="nsa_attention",
    )(q_raw, q_rot, cmp_kv, k_sel, vt_sel, k_win, vt_win, gates_a, overlap_t)


def _dilated_kernel(q0, k0, v0, q1, k1, v1, q2, k2, v2, o_ref,
                    o0_scr, o1_scr, o2_scr, l0_scr, l1_scr, l2_scr):
    tile = DIL_TILE
    groups = ((q0, k0, v0, o0_scr, l0_scr), (q1, k1, v1, o1_scr, l1_scr), (q2, k2, v2, o2_scr, l2_scr))
    for (window, dil), (q_ref, k_ref, v_ref, og_scr, lg_scr) in zip(DIL_GROUPS, groups):
        assert window // dil == tile
        per_class = SEQ // dil
        tiles_per_class = per_class // tile
        nk = tile if tiles_per_class == 1 else 2 * tile
        a_minus_a = _iota((tile, nk), 0) - _iota((tile, nk), 1)

        def step(u, carry, q_ref=q_ref, k_ref=k_ref, v_ref=v_ref, og_scr=og_scr, lg_scr=lg_scr,
                 dil=dil, tiles_per_class=tiles_per_class, nk=nk, a_minus_a=a_minus_a):
            ts = [u * DIL_UNROLL + a for a in range(DIL_UNROLL)]
            cls = [t // tiles_per_class for t in ts]
            n0 = [(t % tiles_per_class) * tile for t in ts]
            kbase = [jnp.maximum(n - (nk - tile), 0) for n in n0]
            q_rows = [pl.ds(c + dil * n, tile, stride=dil) for c, n in zip(cls, n0)]
            k_rows = [pl.ds(c + dil * kb, nk, stride=dil) for c, kb in zip(cls, kbase)]
            qs = [q_ref[r, :].astype(BF16) for r in q_rows]
            ks = [k_ref[r, :].astype(BF16) for r in k_rows]
            vs = [v_ref[r, :].astype(BF16) for r in k_rows]
            ss = [_dot_nt(q, k) for q, k in zip(qs, ks)]
            masks = []
            for n, kb in zip(n0, kbase):
                dist = a_minus_a + (n - kb)
                masks.append((dist >= 0) & (dist <= tile))
            ss = [jnp.where(mk, s, NEG_INF) for mk, s in zip(masks, ss)]
            ms = [jnp.max(s, axis=-1, keepdims=True) for s in ss]
            es = [jnp.where(mk, jnp.exp2(s - m), 0.0) for mk, s, m in zip(masks, ss, ms)]
            dens = [jnp.maximum(jnp.sum(e, axis=-1, keepdims=True), 1e-30) for e in es]
            os_ = [_dot((e / den).astype(BF16), v) for e, den, v in zip(es, dens, vs)]
            for r, o, m, den in zip(q_rows, os_, ms, dens):
                og_scr[r, :] = o
                lg_scr[r, :] = jnp.broadcast_to(m + jnp.log2(den), (tile, HEAD_DIM))
            return carry

        lax.fori_loop(0, SEQ // tile // DIL_UNROLL, step, 0)

    rows = 256

    def merge_body(c, carry):
        sl = pl.ds(pl.multiple_of(c * rows, rows), rows)
        la, lb, lc = l0_scr[sl, :], l1_scr[sl, :], l2_scr[sl, :]
        mx = jnp.maximum(jnp.maximum(la, lb), lc)
        ea, eb, ec = jnp.exp2(la - mx), jnp.exp2(lb - mx), jnp.exp2(lc - mx)
        tot = ea + eb + ec
        out = (ea / tot) * o0_scr[sl, :] + (eb / tot) * o1_scr[sl, :] + (ec / tot) * o2_scr[sl, :]
        o_ref[sl, :] = out.astype(o_ref.dtype)
        return carry

    lax.fori_loop(0, SEQ // rows, merge_body, 0)


def dilated_attention(b_qk, b_v, batch):
    hp = DIL_HEADS_PER_GROUP
    in_specs, args = [], []
    for g in range(len(DIL_GROUPS)):
        for arr, c0 in ((b_qk, 0), (b_qk, DIL_HEADS), (b_v, 0)):
            in_specs.append(pl.BlockSpec((SEQ, HEAD_DIM), lambda b, j, c0=c0, g=g: (b, c0 + g * hp + j)))
            args.append(arr)
    return pl.pallas_call(
        _dilated_kernel,
        grid=(batch, hp),
        in_specs=in_specs,
        out_specs=pl.BlockSpec((SEQ, HEAD_DIM), lambda b, j: (b, j)),
        out_shape=jax.ShapeDtypeStruct((batch * SEQ, hp * HEAD_DIM), BF16),
        scratch_shapes=[pltpu.VMEM((SEQ, HEAD_DIM), F32)] * 6,
        compiler_params=_params("parallel", "parallel"),
        name="dilated_attention",
    )(*args)


def _moba_kernel(q_ref, k_ref, vt_ref, o_ref, kmean_scr, bias_scr):
    i = pl.program_id(1)
    tq, nb, d = MOBA_BLOCK, MOBA_NB, HEAD_DIM
    heads = range(MOBA_HEADS)
    col = lambda h: slice(h * d, (h + 1) * d)

    @pl.when(i == 0)
    def _():
        avg = jnp.where(_iota((nb, SEQ), 1) // MOBA_BLOCK == _iota((nb, SEQ), 0), 1.0 / MOBA_BLOCK, 0.0)
        kmean_scr[...] = _dot(avg.astype(BF16), k_ref[...]).astype(kmean_scr.dtype)

    blk = _iota((nb, tq), 0)
    own = pl.multiple_of(i * tq, tq)
    causal = _iota((tq, tq), 0) <= _iota((tq, tq), 1)
    qs = [q_ref[:, col(h)] for h in heads]
    gates = [jnp.where(blk < i, _dot_nt(kmean_scr[:, col(h)], qs[h]), NEG_INF) for h in heads]
    ss = [_dot_nt(k_ref[pl.ds(own, tq), col(h)], qs[h]) for h in heads]
    for h in heads:
        gate = gates[h]
        rank = jnp.zeros((nb, tq), F32)
        for j in range(nb):
            gj = gate[j:j + 1, :]
            ahead = (gj > gate) | ((gj == gate) & (blk > j))
            rank = rank + jnp.where(ahead, 1.0, 0.0)
        bias_scr[h] = jnp.where((rank < float(MOBA_TOPK)) & (gate > NEG_INF * 0.5), 0.0, NEG_INF)
    ss = [jnp.where(causal, s, NEG_INF) for s in ss]
    ms = [jnp.max(s, axis=0, keepdims=True) for s in ss]
    ps = [jnp.exp2(s - m) for s, m in zip(ss, ms)]
    ls = [jnp.sum(p, axis=0, keepdims=True) for p in ps]
    accs = [_dot(vt_ref[col(h), pl.ds(own, tq)], ps[h].astype(BF16)) for h in heads]

    def body(kj, carry):
        off = pl.multiple_of(kj * tq, tq)
        bs = [bias_scr[h, pl.ds(kj, 1), :] for h in heads]
        ss = [_dot_nt(k_ref[pl.ds(off, tq), col(h)], qs[h]) for h in heads]
        m_new = [jnp.maximum(carry[h][0], jnp.max(ss[h], axis=0, keepdims=True) + bs[h]) for h in heads]
        alpha = [jnp.exp2(carry[h][0] - m_new[h]) for h in heads]
        ps = [jnp.exp2(ss[h] - (m_new[h] - bs[h])) for h in heads]
        ls = [alpha[h] * carry[h][1] + jnp.sum(ps[h], axis=0, keepdims=True) for h in heads]
        pvs = [_dot(vt_ref[col(h), pl.ds(off, tq)], ps[h].astype(BF16)) for h in heads]
        return tuple((m_new[h], ls[h], alpha[h] * carry[h][2] + pvs[h]) for h in heads)

    fin = lax.fori_loop(0, i, body, tuple((ms[h], ls[h], accs[h]) for h in heads))
    for h in heads:
        _, l, acc = fin[h]
        o_ref[:, col(h)] = jnp.transpose(acc / l).astype(o_ref.dtype)


def moba_attention(c_qk, vt, batch):
    tq = MOBA_BLOCK
    nq = SEQ // tq
    return pl.pallas_call(
        _moba_kernel,
        grid=(batch, nq),
        in_specs=[pl.BlockSpec((tq, C_QKV), lambda b, i: (b * nq + i, 0)),
                  pl.BlockSpec((SEQ, C_QKV), lambda b, i: (b, 1)),
                  pl.BlockSpec((C_QKV, SEQ), lambda b, i: (0, b))],
        out_specs=pl.BlockSpec((tq, C_QKV), lambda b, i: (b * nq + i, 0)),
        out_shape=jax.ShapeDtypeStruct((batch * SEQ, C_QKV), BF16),
        scratch_shapes=[pltpu.VMEM((MOBA_NB, C_QKV), BF16), pltpu.VMEM((MOBA_HEADS, MOBA_NB, tq), F32)],
        compiler_params=_params("parallel", "arbitrary"),
        name="moba_attention",
    )(c_qk, c_qk, vt)


def _merge_kernel(oa_ref, ob_ref, oc_ref, ga_ref, gb_ref, gc_ref, wa_ref, wb_ref, wc_ref, out_ref):
    y = ga_ref[...] * _dot(oa_ref[...], wa_ref[...].astype(BF16))
    y = y + gb_ref[...] * _dot(ob_ref[...], wb_ref[...].astype(BF16))
    y = y + gc_ref[...] * _dot(oc_ref[...], wc_ref[...].astype(BF16))
    out_ref[...] = y.astype(out_ref.dtype)


def gated_merge(o_a, o_b, o_c, gates_m, w_a, w_b, w_c, layer, tm=1024, tn=512):
    m = o_a.shape[0]
    nb = D_MODEL // tn
    act = lambda width: pl.BlockSpec((tm, width), lambda j, i: (i, 0))
    gate = lambda g: pl.BlockSpec((tm, tn), lambda j, i, g=g: (i, g * nb + j))
    wgt = lambda width: pl.BlockSpec((None, width, tn), lambda j, i: (layer, 0, j))
    return pl.pallas_call(
        _merge_kernel,
        grid=(nb, m // tm),
        in_specs=[act(o_a.shape[1]), act(o_b.shape[1]), act(o_c.shape[1]),
                  gate(0), gate(1), gate(2),
                  wgt(w_a.shape[1]), wgt(w_b.shape[1]), wgt(w_c.shape[1])],
        out_specs=pl.BlockSpec((tm, tn), lambda j, i: (i, j)),
        out_shape=jax.ShapeDtypeStruct((m, D_MODEL), BF16),
        compiler_params=_params("parallel", "parallel"),
        name="gated_merge",
    )(o_a, o_b, o_c, gates_m, gates_m, gates_m, w_a, w_b, w_c)


def _out_proj_kernel(y_ref, w_ref, x_ref, g_ref, xo_ref, h_ref):
    x_new = x_ref[...] + _dot(y_ref[...], w_ref[...])
    xo_ref[...] = x_new
    h_ref[...] = _rmsnorm_rows(x_new, g_ref[...]).astype(h_ref.dtype)


def out_proj_residual_norm(y, w_o, layer, x2, g, tm=512):
    m, d = x2.shape
    row = pl.BlockSpec((tm, d), lambda i: (i, 0))
    return pl.pallas_call(
        _out_proj_kernel,
        grid=(m // tm,),
        in_specs=[row, pl.BlockSpec((None, d, d), lambda i: (layer, 0, 0)), row,
                  pl.BlockSpec((1, d), lambda i: (0, 0))],
        out_specs=[row, row],
        out_shape=[jax.ShapeDtypeStruct((m, d), F32), jax.ShapeDtypeStruct((m, d), BF16)],
        compiler_params=_params("parallel"),
        name="out_proj",
    )(y, w_o, x2, g.reshape(1, d))


def _mlp_kernel(h_ref, w1_ref, w2_ref, x_ref, g_ref, *outs):
    acc_ref, hn_ref = outs[0], outs[-1]
    f = pl.program_id(1)

    @pl.when(f == 0)
    def _():
        acc_ref[...] = jnp.zeros_like(acc_ref)

    u = jnp.square(jnp.maximum(_dot(h_ref[...], w1_ref[...].astype(BF16)), 0.0))
    acc_ref[...] += _dot(u.astype(BF16), w2_ref[...].astype(BF16))

    @pl.when(f == pl.num_programs(1) - 1)
    def _():
        x_new = x_ref[...] + acc_ref[...]
        if len(outs) == 2:
            acc_ref[...] = x_new
        hn_ref[...] = _rmsnorm_rows(x_new, g_ref[...]).astype(hn_ref.dtype)


def mlp_residual_norm(h2, w1, w2, layer, x2, g_next, next_dtype, emit_x, tm=1024, tf=512):
    m, d = x2.shape
    row = pl.BlockSpec((tm, d), lambda i, f: (i, 0))
    row_once = pl.BlockSpec((tm, d), lambda i, f: (i, 0), pipeline_mode=pl.Buffered(1))
    out_shape = [jax.ShapeDtypeStruct((m, d), F32)] * emit_x + [jax.ShapeDtypeStruct((m, d), next_dtype)]
    res = pl.pallas_call(
        _mlp_kernel,
        grid=(m // tm, D_FF // tf),
        in_specs=[row, pl.BlockSpec((None, d, tf), lambda i, f: (layer, 0, f)),
                  pl.BlockSpec((None, tf, d), lambda i, f: (layer, f, 0)),
                  row_once, pl.BlockSpec((1, d), lambda i, f: (0, 0))],
        out_specs=[row_once] * len(out_shape),
        out_shape=out_shape,
        compiler_params=_params("parallel", "arbitrary"),
        name="mlp",
    )(h2, w1, w2, x2, g_next.reshape(1, d))
    return (res[0], res[1]) if emit_x else (None, res[0])


def _rope_tables():
    inv = ROPE_THETA ** (-jnp.arange(0, ROPE_DIM, 2, dtype=F32) / ROPE_DIM)
    ang = jnp.arange(SEQ, dtype=F32)[:, None] * inv[None, :]
    cos, sin = jnp.cos(ang), jnp.sin(ang)
    zeros = jnp.zeros((SEQ, HEAD_DIM - ROPE_DIM), F32)
    zero_h = jnp.zeros((SEQ, ROPE_HALF), F32)
    c = jnp.concatenate([cos, cos, jnp.ones_like(zeros)], axis=1)
    s_up = jnp.concatenate([-sin, zero_h, zeros], axis=1)
    s_dn = jnp.concatenate([zero_h, sin, zeros], axis=1)
    return c, s_up, s_dn


def _overlap_table_t():
    cs = np.arange(NSA_M_PAD)[None, :] * NSA_CMP_STRIDE
    bs = np.arange(NSA_NB)[:, None] * NSA_SEL_BLOCK
    ov = np.clip(np.minimum(cs + NSA_CMP_LEN, bs + NSA_SEL_BLOCK) - np.maximum(cs, bs), 0, None) / NSA_CMP_LEN
    ov[:, NSA_M_PAD - 1] = 0.0
    return jnp.asarray(ov, dtype=BF16)


W_IN_NAMES = ("a_q", "a_kc", "a_vc", "a_ks", "a_vs", "a_kw", "a_vw", "a_g",
              "b_q", "b_k", "b_v", "c_q", "c_k", "c_v", "m_a", "m_b", "m_c")
W_IN_START = dict(zip(W_IN_NAMES, np.cumsum((0,) + IN_SPLIT_SIZES[:-1]).tolist()))
W_IN_SIZE = dict(zip(W_IN_NAMES, IN_SPLIT_SIZES))


W_GATE_AT = W_IN_START["a_g"]


def _cast_rows_kernel(src_ref, out_ref, *, keep):
    x = src_ref[0]
    if keep < x.shape[0]:
        x = jnp.where(_iota(x.shape, 0) < keep, x, 0.0)
    out_ref[...] = x.astype(out_ref.dtype)


def cast_rows(wt, row0, n_rows, tr, keep=None):
    depth, _, k = wt.shape
    return pl.pallas_call(
        functools.partial(_cast_rows_kernel, keep=tr if keep is None else keep),
        grid=(depth, n_rows // tr),
        in_specs=[pl.BlockSpec((pl.Element(1), pl.Element(tr), pl.Element(k)),
                               lambda l, r: (l, pl.multiple_of(row0 + r * tr, F32_SUBLANES), 0))],
        out_specs=pl.BlockSpec((None, tr, k), lambda l, r: (l, r, 0)),
        out_shape=jax.ShapeDtypeStruct((depth, n_rows, k), BF16),
        compiler_params=_params("parallel", "parallel"),
        name="cast_rows",
    )(wt)


def bf16_w_in_t(w_in):
    wt = jnp.transpose(w_in, (0, 2, 1))
    hi0 = W_GATE_AT + A_G
    lo = cast_rows(wt, 0, W_GATE_AT, 512)
    hi = cast_rows(wt, hi0, wt.shape[1] - hi0, 512)
    ag = cast_rows(wt, W_GATE_AT, LANES, LANES, keep=A_G)
    where = {"a_g": (ag, 0)}
    for name in W_IN_NAMES:
        if W_IN_START[name] < W_GATE_AT:
            where[name] = (lo, W_IN_START[name])
        elif W_IN_START[name] > W_GATE_AT:
            where[name] = (hi, W_IN_START[name] - hi0)
    return where


def _scales(*widths_and_values):
    return jnp.concatenate([jnp.full((1, w), v, F32) for w, v in widths_and_values], axis=1)


def _layer(x2, h, batch, layer, tabs, overlap_t, w_at, pe_k, w1_k, w2_k, pe_v, w1_v, w2_v,
           w_br_a, w_br_b, w_br_c, w_o, mlp_g, w_mlp_in, w_mlp_out, g_next, last):
    def proj(first, n, *args):
        piece, col0 = w_at[first]
        return project(h, piece, layer, col0, n, *args)

    def proj_t(name):
        piece, col0 = w_at[name]
        return project_transposed(h, piece, layer, col0, W_IN_SIZE[name], BF16)

    q_raw, q_rot = proj("a_q", A_Q, "both", BF16, tabs, _scales((A_Q, Q_SCALE)))
    cmp_src = proj("a_kc", 2 * A_KV, "plain", F32)
    k_sel = proj("a_ks", A_KV, "rope", BF16, tabs, _scales((A_KV, 1.0)))
    k_win = proj("a_kw", A_KV, "rope", BF16, tabs, _scales((A_KV, 1.0)))
    b_qk = proj("b_q", 2 * B_QKV, "rope", F32, tabs, _scales((B_QKV, Q_SCALE), (B_QKV, 1.0)))
    b_v = proj("b_v", B_QKV, "plain", F32)
    c_qk = proj("c_q", 2 * C_QKV, "rope", BF16, tabs, _scales((C_QKV, Q_SCALE), (C_QKV, 1.0)))
    gates_m = proj("m_a", 3 * D_MODEL, "sigmoid", BF16)
    gates_a = proj("a_g", LANES, "sigmoid", F32)
    vt_sel, vt_win, vt_moba = proj_t("a_vs"), proj_t("a_vw"), proj_t("c_v")

    flat = NSA_CMP_LEN * HEAD_DIM
    cmp_kv = nsa_compress(cmp_src, pe_k.reshape(1, flat), w1_k.reshape(flat, NSA_CMP_HIDDEN).astype(BF16),
                          w2_k.astype(BF16), pe_v.reshape(1, flat),
                          w1_v.reshape(flat, NSA_CMP_HIDDEN).astype(BF16), w2_v.astype(BF16), batch)
    o_a = nsa_attention(q_raw, q_rot, cmp_kv, k_sel, k_win, vt_sel, vt_win, gates_a, overlap_t, batch)
    o_b = dilated_attention(b_qk, b_v, batch)
    o_c = moba_attention(c_qk, vt_moba, batch)

    merged = gated_merge(o_a, o_b, o_c, gates_m, w_br_a, w_br_b, w_br_c, layer)
    x2, h2 = out_proj_residual_norm(merged, w_o, layer, x2, mlp_g)
    return mlp_residual_norm(h2, w_mlp_in, w_mlp_out, layer, x2, g_next, F32 if last else BF16, emit_x=not last)


def kernel(x, attn_norm_g, w_in, cmp_pe_k, cmp_w1_k, cmp_w2_k, cmp_pe_v, cmp_w1_v, cmp_w2_v,
           w_br_a, w_br_b, w_br_c, w_o, mlp_norm_g, w_mlp_in, w_mlp_out, final_norm_g):
    batch, seq, d = x.shape
    assert seq == SEQ and d == D_MODEL
    depth = w_in.shape[0]
    tabs = _rope_tables()
    overlap_t = _overlap_table_t()
    w_at = bf16_w_in_t(w_in)
    w_o = w_o.astype(BF16)
    x2 = x.reshape(batch * seq, d)
    h = rmsnorm(x2, attn_norm_g[0], BF16)
    for l in range(depth):
        last = l == depth - 1
        g_next = final_norm_g if last else attn_norm_g[l + 1]
        x2, h = _layer(x2, h, batch, l, tabs, overlap_t, w_at,
                       cmp_pe_k[l], cmp_w1_k[l], cmp_w2_k[l], cmp_pe_v[l], cmp_w1_v[l], cmp_w2_v[l],
                       w_br_a, w_br_b, w_br_c, w_o, mlp_norm_g[l], w_mlp_in, w_mlp_out, g_next, last)
    return h.reshape(batch, seq, d)
```

```python
import functools
import math

import numpy as np
import jax
import jax.numpy as jnp
from jax import lax
from jax.experimental import pallas as pl
from jax.experimental.pallas import tpu as pltpu

D_MODEL = 2048
SEQ = 2048
HEAD_DIM = 128
ROPE_THETA = 500000.0
ROPE_DIM = HEAD_DIM // 4
ROPE_HALF = ROPE_DIM // 2
NORM_EPS = 1e-6
NEG_INF = -1e30
Q_SCALE = HEAD_DIM ** -0.5 * math.log2(math.e)

NSA_HEADS = 8
NSA_KV_HEADS = 2
NSA_REP = NSA_HEADS // NSA_KV_HEADS
NSA_CMP_LEN = 32
NSA_CMP_STRIDE = 16
NSA_CMP_HIDDEN = 256
NSA_SEL_BLOCK = 64
NSA_SEL_TOPN = 16
NSA_WINDOW = 512
NSA_FORCE_BONUS = 1e4
NSA_NB = SEQ // NSA_SEL_BLOCK
NSA_M_PAD = SEQ // NSA_CMP_STRIDE

DIL_GROUPS = ((128, 1), (512, 4), (2048, 16))
DIL_HEADS_PER_GROUP = 4
DIL_HEADS = DIL_HEADS_PER_GROUP * len(DIL_GROUPS)
DIL_TILE = 128
DIL_UNROLL = 16

MOBA_HEADS = 8
MOBA_BLOCK = 256
MOBA_TOPK = 3
MOBA_NB = SEQ // MOBA_BLOCK

D_FF = 4 * D_MODEL
A_Q = NSA_HEADS * HEAD_DIM
A_KV = NSA_KV_HEADS * HEAD_DIM
A_G = 3 * NSA_HEADS
B_QKV = DIL_HEADS * HEAD_DIM
C_QKV = MOBA_HEADS * HEAD_DIM
IN_SPLIT_SIZES = (A_Q, A_KV, A_KV, A_KV, A_KV, A_KV, A_KV, A_G,
                  B_QKV, B_QKV, B_QKV, C_QKV, C_QKV, C_QKV,
                  D_MODEL, D_MODEL, D_MODEL)

LANES = 128
F32_SUBLANES = 8
PROJ_TN_MAX = 512
PROJ_VMEM_BUDGET = 46 * 1024 * 1024
VMEM_LIMIT = 60 * 1024 * 1024

BF16 = jnp.bfloat16
F32 = jnp.float32


def _params(*sem):
    return pltpu.CompilerParams(dimension_semantics=sem, vmem_limit_bytes=VMEM_LIMIT)


def _dot(a, b):
    return jnp.dot(a, b, preferred_element_type=F32)


def _dot_nt(a, b):
    return lax.dot_general(a, b, (((1,), (1,)), ((), ())), preferred_element_type=F32)


def _iota(shape, axis):
    return lax.broadcasted_iota(jnp.int32, shape, axis)


def _rmsnorm_rows(x, g):
    y = x * lax.rsqrt(jnp.mean(x * x, axis=-1, keepdims=True) + NORM_EPS)
    return y * g


def _rmsnorm_kernel(x_ref, g_ref, h_ref):
    h_ref[...] = _rmsnorm_rows(x_ref[...], g_ref[...]).astype(h_ref.dtype)


def rmsnorm(x2, g, out_dtype, tm=512):
    m, d = x2.shape
    return pl.pallas_call(
        _rmsnorm_kernel,
        grid=(m // tm,),
        in_specs=[pl.BlockSpec((tm, d), lambda i: (i, 0)), pl.BlockSpec((1, d), lambda i: (0, 0))],
        out_specs=pl.BlockSpec((tm, d), lambda i: (i, 0)),
        out_shape=jax.ShapeDtypeStruct((m, d), out_dtype),
        compiler_params=_params("parallel"),
        name="rmsnorm",
    )(x2, g.reshape(1, d))


def _rope_lanes(acc, c, s_up, s_dn):
    tn = acc.shape[1]
    reps = tn // HEAD_DIM
    if reps > 1:
        c = jnp.concatenate([c] * reps, axis=1)
        s_up = jnp.concatenate([s_up] * reps, axis=1)
        s_dn = jnp.concatenate([s_dn] * reps, axis=1)
    up = pltpu.roll(acc, tn - ROPE_HALF, axis=1)
    dn = pltpu.roll(acc, ROPE_HALF, axis=1)
    return acc * c + up * s_up + dn * s_dn


def _proj_kernel(*refs, mode):
    if mode in ("rope", "both"):
        h_ref, w_ref, c_ref, su_ref, sd_ref, cs_ref = refs[:6]
        outs = refs[6:]
    else:
        h_ref, w_ref = refs[:2]
        outs = refs[2:]
    acc = _dot_nt(h_ref[...], w_ref[0].astype(BF16))
    if mode == "plain":
        outs[0][...] = acc.astype(outs[0].dtype)
    elif mode == "sigmoid":
        outs[0][...] = jax.nn.sigmoid(acc).astype(outs[0].dtype)
    else:
        col_scale = cs_ref[...]
        roped = _rope_lanes(acc, c_ref[...], su_ref[...], sd_ref[...]) * col_scale
        if mode == "both":
            outs[0][...] = (acc * col_scale).astype(outs[0].dtype)
            outs[1][...] = roped.astype(outs[1].dtype)
        else:
            outs[0][...] = roped.astype(outs[0].dtype)


def _proj_tiles(k, n, w_bytes, out_bytes, n_out, rope):
    tn = math.gcd(PROJ_TN_MAX, n)
    for tm in (SEQ, SEQ // 2):
        blocks = (tm * k * 2 + k * tn * w_bytes + n_out * tm * tn * out_bytes
                  + (3 * tm * HEAD_DIM * 4 if rope else 0))
        if 2 * blocks <= PROJ_VMEM_BUDGET:
            break
    return tm, tn


def _weight_rows_spec(tn, k, layer, row0, step_axis):
    assert row0 % F32_SUBLANES == 0
    return pl.BlockSpec(
        (pl.Element(1), pl.Element(tn), pl.Element(k)),
        lambda *idx: (layer, pl.multiple_of(row0 + idx[step_axis] * tn, F32_SUBLANES), 0))


def project(h, wt, layer, col0, n, mode, out_dtype, rope_tabs=None, col_scale=None):
    m, k = h.shape
    n_out = 2 if mode == "both" else 1
    rope = mode in ("rope", "both")
    tm, tn = _proj_tiles(k, n, wt.dtype.itemsize, jnp.dtype(out_dtype).itemsize, n_out, rope)
    assert m % tm == 0 and n % tn == 0 and SEQ % tm == 0
    in_specs = [pl.BlockSpec((tm, k), lambda i, j: (i, 0)), _weight_rows_spec(tn, k, layer, col0, 1)]
    args = [h, wt]
    if rope:
        pos_blocks = SEQ // tm
        for t in rope_tabs:
            in_specs.append(pl.BlockSpec((tm, HEAD_DIM), lambda i, j: (i % pos_blocks, 0)))
            args.append(t)
        in_specs.append(pl.BlockSpec((1, tn), lambda i, j: (0, j)))
        args.append(col_scale)
    out_spec = pl.BlockSpec((tm, tn), lambda i, j: (i, j))
    out_shape = jax.ShapeDtypeStruct((m, n), out_dtype)
    res = pl.pallas_call(
        functools.partial(_proj_kernel, mode=mode),
        grid=(m // tm, n // tn),
        in_specs=in_specs,
        out_specs=[out_spec] * n_out,
        out_shape=[out_shape] * n_out,
        compiler_params=_params("parallel", "parallel"),
        name="proj_" + mode,
    )(*args)
    return res if n_out == 2 else res[0]


def _proj_t_kernel(wt_ref, h_ref, out_ref):
    out_ref[...] = _dot_nt(wt_ref[0].astype(BF16), h_ref[...]).astype(out_ref.dtype)


def project_transposed(h, wt, layer, col0, n, out_dtype, tm=1024):
    m, k = h.shape
    tn = math.gcd(PROJ_TN_MAX, n)
    return pl.pallas_call(
        _proj_t_kernel,
        grid=(n // tn, m // tm),
        in_specs=[_weight_rows_spec(tn, k, layer, col0, 0), pl.BlockSpec((tm, k), lambda j, i: (i, 0))],
        out_specs=pl.BlockSpec((tn, tm), lambda j, i: (j, i)),
        out_shape=jax.ShapeDtypeStruct((n, m), out_dtype),
        compiler_params=_params("parallel", "parallel"),
        name="proj_transposed",
    )(wt, h)


def _compress_kernel(k0_ref, k1_ref, v0_ref, v1_ref, pek_ref, w1k_ref, w2k_ref, pev_ref, w1v_ref, w2v_ref,
                     out_ref):
    half = NSA_CMP_STRIDE * HEAD_DIM
    for idx, src_ref in enumerate((k0_ref, k1_ref, v0_ref, v1_ref)):
        is_k = idx < NSA_KV_HEADS
        pe_ref, w1_ref, w2_ref = (pek_ref, w1k_ref, w2k_ref) if is_k else (pev_ref, w1v_ref, w2v_ref)
        x = jnp.concatenate(
            [src_ref[pl.ds(l, NSA_M_PAD, stride=NSA_CMP_STRIDE), :]
             for l in range(NSA_CMP_STRIDE)], axis=1)
        pe = pe_ref[...]
        first = _dot((x + pe[:, :half]).astype(BF16), w1_ref[:half, :])
        second = _dot((x + pe[:, half:]).astype(BF16), w1_ref[half:, :])
        hid = jax.nn.gelu(first + pltpu.roll(second, NSA_M_PAD - 1, axis=0))
        out = _dot(hid.astype(BF16), w2_ref[...])
        out_ref[idx] = (out if is_k else jnp.transpose(out)).astype(out_ref.dtype)


def nsa_compress(pf, pe_k, w1_k, w2_k, pe_v, w1_v, w2_v, batch):
    flat = NSA_CMP_LEN * HEAD_DIM
    const = lambda shape: pl.BlockSpec(shape, lambda b: (0,) * len(shape))
    return pl.pallas_call(
        _compress_kernel,
        grid=(batch,),
        in_specs=[pl.BlockSpec((SEQ, HEAD_DIM), lambda b, c=c: (b, c)) for c in range(4)] + [
                  const((1, flat)), const((flat, NSA_CMP_HIDDEN)), const((NSA_CMP_HIDDEN, HEAD_DIM)),
                  const((1, flat)), const((flat, NSA_CMP_HIDDEN)), const((NSA_CMP_HIDDEN, HEAD_DIM))],
        out_specs=pl.BlockSpec((None, 4, NSA_M_PAD, HEAD_DIM), lambda b: (b, 0, 0, 0)),
        out_shape=jax.ShapeDtypeStruct((batch, 4, NSA_M_PAD, HEAD_DIM), BF16),
        compiler_params=_params("parallel"),
        name="nsa_compress",
    )(pf, pf, pf, pf, pe_k, w1_k, w2_k, pe_v, w1_v, w2_v)


NSA_TQ = 256
NSA_KC = 256
NSA_BLK_PER_CHUNK = NSA_KC // NSA_SEL_BLOCK


def _nsa_kernel(qraw_ref, qrot_ref, cmp_ref, ksel_ref, vselt_ref, kwin_ref, vwint_ref,
                gate_ref, ovt_ref, o_ref, bias_scr):
    i = pl.program_id(1)
    tq, kc_w, rep, d = NSA_TQ, NSA_KC, NSA_REP, HEAD_DIM
    groups = range(NSA_KV_HEADS)
    width = rep * tq
    bpc = NSA_BLK_PER_CHUNK
    t0 = i * tq
    head_cols = lambda g, r: slice((g * rep + r) * d, (g * rep + r + 1) * d)
    stack = lambda ref, g: jnp.concatenate([ref[:, head_cols(g, r)] for r in range(rep)], axis=0)
    q_raw = [stack(qraw_ref, g) for g in groups]
    q_rot = [stack(qrot_ref, g) for g in groups]
    k_of = lambda ref, g, kj: ref[pl.ds(pl.multiple_of(kj * kc_w, kc_w), kc_w), g * d:(g + 1) * d]
    vt_of = lambda ref, g, kj: ref[g * d:(g + 1) * d, pl.ds(pl.multiple_of(kj * kc_w, kc_w), kc_w)]
    col_max = lambda s: jnp.max(s, axis=0, keepdims=True)
    col_sum = lambda p: jnp.sum(p, axis=0, keepdims=True)

    q_pos = t0 + (_iota((NSA_M_PAD, width), 1) & (tq - 1))
    vis = (_iota((NSA_M_PAD, width), 0) * NSA_CMP_STRIDE + (NSA_CMP_LEN - 1)) <= q_pos
    sc = [jnp.where(vis, _dot_nt(cmp_ref[g], q_raw[g]), NEG_INF) for g in groups]
    ec = [jnp.where(vis, jnp.exp2(s - col_max(s)), 0.0) for s in sc]
    pc = [(e / jnp.maximum(col_sum(e), 1e-30)).astype(BF16) for e in ec]
    o_cmp = [_dot(cmp_ref[NSA_KV_HEADS + g], pc[g]) for g in groups]
    imp_heads = [_dot(ovt_ref[...], p) for p in pc]

    blk = _iota((NSA_NB, tq), 0)
    q_blk = (t0 + _iota((NSA_NB, tq), 1)) // NSA_SEL_BLOCK
    forced = (blk == 0) | (blk == q_blk) | (blk == q_blk - 1)
    for g in groups:
        imp = imp_heads[g][:, :tq]
        for r in range(1, rep):
            imp = imp + imp_heads[g][:, r * tq:(r + 1) * tq]
        score = jnp.where(blk <= q_blk, imp + jnp.where(forced, NSA_FORCE_BONUS, 0.0), NEG_INF)
        rank = jnp.zeros((NSA_NB, tq), F32)
        for j in range(NSA_NB):
            sj = score[j:j + 1, :]
            ahead = (sj > score) | ((sj == score) & (blk > j))
            rank = rank + jnp.where(ahead, 1.0, 0.0)
        bias = jnp.where((rank < float(NSA_SEL_TOPN)) & (score > NEG_INF * 0.5), 0.0, NEG_INF)
        bias = jnp.concatenate([bias] * rep, axis=1)
        for c in range(SEQ // kc_w):
            bias_scr[g, c] = bias[c * bpc:(c + 1) * bpc, :]

    def sel_step(kj, ss, carry):
        bs = [bias_scr[g, kj] for g in groups]
        s3 = [s.reshape(bpc, NSA_SEL_BLOCK, width) for s in ss]
        m_new = [jnp.maximum(carry[g][0], col_max(jnp.max(s3[g], axis=1) + bs[g])) for g in groups]
        alpha = [jnp.exp2(carry[g][0] - m_new[g]) for g in groups]
        ps = [jnp.exp2(s3[g] - (m_new[g] - bs[g])[:, None, :]).reshape(kc_w, width) for g in groups]
        ls = [alpha[g] * carry[g][1] + col_sum(ps[g]) for g in groups]
        pvs = [_dot(vt_of(vselt_ref, g, kj), ps[g].astype(BF16)) for g in groups]
        return tuple((m_new[g], ls[g], alpha[g] * carry[g][2] + pvs[g]) for g in groups)

    init = tuple((jnp.full((1, width), NEG_INF, F32), jnp.zeros((1, width), F32), jnp.zeros((d, width), F32))
                 for g in groups)
    carry = lax.fori_loop(
        0, i, lambda kj, c: sel_step(kj, [_dot_nt(k_of(ksel_ref, g, kj), q_rot[g]) for g in groups], c), init)

    key_row = _iota((kc_w, width), 0)
    q_col = _iota((kc_w, width), 1) & (tq - 1)
    causal = key_row <= q_col
    diag = [jnp.where(causal, _dot_nt(k_of(ksel_ref, g, i), q_rot[g]), NEG_INF) for g in groups]
    sel = sel_step(i, diag, carry)

    far = i - 2
    near = i - 1
    far_ok = jnp.where(far >= 0, 0.0, NEG_INF)
    near_ok = jnp.where(near >= 0, 0.0, NEG_INF)
    far_c = jnp.maximum(far, 0)
    near_c = jnp.maximum(near, 0)
    s_own = [jnp.where(causal, _dot_nt(k_of(kwin_ref, g, i), q_rot[g]), NEG_INF) for g in groups]
    s_near = [_dot_nt(k_of(kwin_ref, g, near_c), q_rot[g]) + near_ok for g in groups]
    s_far = [jnp.where(key_row > q_col, _dot_nt(k_of(kwin_ref, g, far_c), q_rot[g]), NEG_INF) + far_ok
             for g in groups]
    m_w = [jnp.maximum(jnp.maximum(col_max(s_own[g]), col_max(s_near[g])), col_max(s_far[g])) for g in groups]
    p_own = [jnp.exp2(s_own[g] - m_w[g]) for g in groups]
    p_near = [jnp.exp2(s_near[g] - m_w[g]) for g in groups]
    p_far = [jnp.exp2(s_far[g] - m_w[g]) for g in groups]
    l_win = [col_sum(p_own[g]) + col_sum(p_near[g]) + col_sum(p_far[g]) for g in groups]
    acc_win = [_dot(vt_of(vwint_ref, g, i), p_own[g].astype(BF16))
               + _dot(vt_of(vwint_ref, g, near_c), p_near[g].astype(BF16))
               + _dot(vt_of(vwint_ref, g, far_c), p_far[g].astype(BF16)) for g in groups]

    gates = jnp.transpose(gate_ref[...])
    for g in groups:
        o_sel = sel[g][2] / sel[g][1]
        o_win = acc_win[g] / l_win[g]
        for r in range(rep):
            lanes = slice(r * tq, (r + 1) * tq)
            row = 3 * (g * rep + r)
            o = (gates[row:row + 1, :] * o_cmp[g][:, lanes] + gates[row + 1:row + 2, :] * o_sel[:, lanes]
                 + gates[row + 2:row + 3, :] * o_win[:, lanes])
            o_ref[:, head_cols(g, r)] = jnp.transpose(o).astype(o_ref.dtype)


def nsa_attention(q_raw, q_rot, cmp_kv, k_sel, k_win, vt_sel, vt_win, gates_a, overlap_t, batch):
    tq = NSA_TQ
    nq = SEQ // tq
    g_n = NSA_KV_HEADS
    row = lambda b, i: (b * nq + i, 0)
    return pl.pallas_call(
        _nsa_kernel,
        grid=(batch, nq),
        in_specs=[pl.BlockSpec((tq, A_Q), row),
                  pl.BlockSpec((tq, A_Q), row),
                  pl.BlockSpec((None, 2 * g_n, NSA_M_PAD, HEAD_DIM), lambda b, i: (b, 0, 0, 0)),
                  pl.BlockSpec((SEQ, A_KV), lambda b, i: (b, 0)),
                  pl.BlockSpec((A_KV, SEQ), lambda b, i: (0, b)),
                  pl.BlockSpec((SEQ, A_KV), lambda b, i: (b, 0)),
                  pl.BlockSpec((A_KV, SEQ), lambda b, i: (0, b)),
                  pl.BlockSpec((tq, LANES), row),
                  pl.BlockSpec((NSA_NB, NSA_M_PAD), lambda b, i: (0, 0))],
        out_specs=pl.BlockSpec((tq, A_Q), row),
        out_shape=jax.ShapeDtypeStruct((batch * SEQ, A_Q), BF16),
        scratch_shapes=[pltpu.VMEM((g_n, SEQ // NSA_KC, NSA_BLK_PER_CHUNK, NSA_REP * tq), F32)],
        compiler_params=_params("parallel", "parallel"),
        name="nsa_attention",
    )(q_raw, q_rot, cmp_kv, k_sel, vt_sel, k_win, vt_win, gates_a, overlap_t)


def _dilated_kernel(q0, k0, v0, q1, k1, v1, q2, k2, v2, o_ref,
                    o0_scr, o1_scr, o2_scr, l0_scr, l1_scr, l2_scr):
    tile = DIL_TILE
    groups = ((q0, k0, v0, o0_scr, l0_scr), (q1, k1, v1, o1_scr, l1_scr), (q2, k2, v2, o2_scr, l2_scr))
    for (window, dil), (q_ref, k_ref, v_ref, og_scr, lg_scr) in zip(DIL_GROUPS, groups):
        assert window // dil == tile
        per_class = SEQ // dil
        tiles_per_class = per_class // tile
        nk = tile if tiles_per_class == 1 else 2 * tile
        a_minus_a = _iota((tile, nk), 0) - _iota((tile, nk), 1)

        def step(u, carry, q_ref=q_ref, k_ref=k_ref, v_ref=v_ref, og_scr=og_scr, lg_scr=lg_scr,
                 dil=dil, tiles_per_class=tiles_per_class, nk=nk, a_minus_a=a_minus_a):
            ts = [u * DIL_UNROLL + a for a in range(DIL_UNROLL)]
            cls = [t // tiles_per_class for t in ts]
            n0 = [(t % tiles_per_class) * tile for t in ts]
            kbase = [jnp.maximum(n - (nk - tile), 0) for n in n0]
            q_rows = [pl.ds(c + dil * n, tile, stride=dil) for c, n in zip(cls, n0)]
            k_rows = [pl.ds(c + dil * kb, nk, stride=dil) for c, kb in zip(cls, kbase)]
            qs = [q_ref[r, :].astype(BF16) for r in q_rows]
            ks = [k_ref[r, :].astype(BF16) for r in k_rows]
            vs = [v_ref[r, :].astype(BF16) for r in k_rows]
            ss = [_dot_nt(q, k) for q, k in zip(qs, ks)]
            masks = []
            for n, kb in zip(n0, kbase):
                dist = a_minus_a + (n - kb)
                masks.append((dist >= 0) & (dist <= tile))
            ss = [jnp.where(mk, s, NEG_INF) for mk, s in zip(masks, ss)]
            ms = [jnp.max(s, axis=-1, keepdims=True) for s in ss]
            es = [jnp.where(mk, jnp.exp2(s - m), 0.0) for mk, s, m in zip(masks, ss, ms)]
            dens = [jnp.maximum(jnp.sum(e, axis=-1, keepdims=True), 1e-30) for e in es]
            os_ = [_dot((e / den).astype(BF16), v) for e, den, v in zip(es, dens, vs)]
            for r, o, m, den in zip(q_rows, os_, ms, dens):
                og_scr[r, :] = o
                lg_scr[r, :] = jnp.broadcast_to(m + jnp.log2(den), (tile, HEAD_DIM))
            return carry

        lax.fori_loop(0, SEQ // tile // DIL_UNROLL, step, 0)

    rows = 256

    def merge_body(c, carry):
        sl = pl.ds(pl.multiple_of(c * rows, rows), rows)
        la, lb, lc = l0_scr[sl, :], l1_scr[sl, :], l2_scr[sl, :]
        mx = jnp.maximum(jnp.maximum(la, lb), lc)
        ea, eb, ec = jnp.exp2(la - mx), jnp.exp2(lb - mx), jnp.exp2(lc - mx)
        tot = ea + eb + ec
        out = (ea / tot) * o0_scr[sl, :] + (eb / tot) * o1_scr[sl, :] + (ec / tot) * o2_scr[sl, :]
        o_ref[sl, :] = out.astype(o_ref.dtype)
        return carry

    lax.fori_loop(0, SEQ // rows, merge_body, 0)


def dilated_attention(b_qk, b_v, batch):
    hp = DIL_HEADS_PER_GROUP
    in_specs, args = [], []
    for g in range(len(DIL_GROUPS)):
        for arr, c0 in ((b_qk, 0), (b_qk, DIL_HEADS), (b_v, 0)):
            in_specs.append(pl.BlockSpec((SEQ, HEAD_DIM), lambda b, j, c0=c0, g=g: (b, c0 + g * hp + j)))
            args.append(arr)
    return pl.pallas_call(
        _dilated_kernel,
        grid=(batch, hp),
        in_specs=in_specs,
        out_specs=pl.BlockSpec((SEQ, HEAD_DIM), lambda b, j: (b, j)),
        out_shape=jax.ShapeDtypeStruct((batch * SEQ, hp * HEAD_DIM), BF16),
        scratch_shapes=[pltpu.VMEM((SEQ, HEAD_DIM), F32)] * 6,
        compiler_params=_params("parallel", "parallel"),
        name="dilated_attention",
    )(*args)


def _moba_kernel(q_ref, k_ref, vt_ref, o_ref, kmean_scr, bias_scr):
    i = pl.program_id(1)
    tq, nb, d = MOBA_BLOCK, MOBA_NB, HEAD_DIM
    heads = range(MOBA_HEADS)
    col = lambda h: slice(h * d, (h + 1) * d)

    @pl.when(i == 0)
    def _():
        avg = jnp.where(_iota((nb, SEQ), 1) // MOBA_BLOCK == _iota((nb, SEQ), 0), 1.0 / MOBA_BLOCK, 0.0)
        kmean_scr[...] = _dot(avg.astype(BF16), k_ref[...]).astype(kmean_scr.dtype)

    blk = _iota((nb, tq), 0)
    own = pl.multiple_of(i * tq, tq)
    causal = _iota((tq, tq), 0) <= _iota((tq, tq), 1)
    qs = [q_ref[:, col(h)] for h in heads]
    gates = [jnp.where(blk < i, _dot_nt(kmean_scr[:, col(h)], qs[h]), NEG_INF) for h in heads]
    ss = [_dot_nt(k_ref[pl.ds(own, tq), col(h)], qs[h]) for h in heads]
    for h in heads:
        gate = gates[h]
        rank = jnp.zeros((nb, tq), F32)
        for j in range(nb):
            gj = gate[j:j + 1, :]
            ahead = (gj > gate) | ((gj == gate) & (blk > j))
            rank = rank + jnp.where(ahead, 1.0, 0.0)
        bias_scr[h] = jnp.where((rank < float(MOBA_TOPK)) & (gate > NEG_INF * 0.5), 0.0, NEG_INF)
    ss = [jnp.where(causal, s, NEG_INF) for s in ss]
    ms = [jnp.max(s, axis=0, keepdims=True) for s in ss]
    ps = [jnp.exp2(s - m) for s, m in zip(ss, ms)]
    ls = [jnp.sum(p, axis=0, keepdims=True) for p in ps]
    accs = [_dot(vt_ref[col(h), pl.ds(own, tq)], ps[h].astype(BF16)) for h in heads]

    def body(kj, carry):
        off = pl.multiple_of(kj * tq, tq)
        bs = [bias_scr[h, pl.ds(kj, 1), :] for h in heads]
        ss = [_dot_nt(k_ref[pl.ds(off, tq), col(h)], qs[h]) for h in heads]
        m_new = [jnp.maximum(carry[h][0], jnp.max(ss[h], axis=0, keepdims=True) + bs[h]) for h in heads]
        alpha = [jnp.exp2(carry[h][0] - m_new[h]) for h in heads]
        ps = [jnp.exp2(ss[h] - (m_new[h] - bs[h])) for h in heads]
        ls = [alpha[h] * carry[h][1] + jnp.sum(ps[h], axis=0, keepdims=True) for h in heads]
        pvs = [_dot(vt_ref[col(h), pl.ds(off, tq)], ps[h].astype(BF16)) for h in heads]
        return tuple((m_new[h], ls[h], alpha[h] * carry[h][2] + pvs[h]) for h in heads)

    fin = lax.fori_loop(0, i, body, tuple((ms[h], ls[h], accs[h]) for h in heads))
    for h in heads:
        _, l, acc = fin[h]
        o_ref[:, col(h)] = jnp.transpose(acc / l).astype(o_ref.dtype)


def moba_attention(c_qk, vt, batch):
    tq = MOBA_BLOCK
    nq = SEQ // tq
    return pl.pallas_call(
        _moba_kernel,
        grid=(batch, nq),
        in_specs=[pl.BlockSpec((tq, C_QKV), lambda b, i: (b * nq + i, 0)),
                  pl.BlockSpec((SEQ, C_QKV), lambda b, i: (b, 1)),
                  pl.BlockSpec((C_QKV, SEQ), lambda b, i: (0, b))],
        out_specs=pl.BlockSpec((tq, C_QKV), lambda b, i: (b * nq + i, 0)),
        out_shape=jax.ShapeDtypeStruct((batch * SEQ, C_QKV), BF16),
        scratch_shapes=[pltpu.VMEM((MOBA_NB, C_QKV), BF16), pltpu.VMEM((MOBA_HEADS, MOBA_NB, tq), F32)],
        compiler_params=_params("parallel", "arbitrary"),
        name="moba_attention",
    )(c_qk, c_qk, vt)


def _merge_kernel(oa_ref, ob_ref, oc_ref, ga_ref, gb_ref, gc_ref, wa_ref, wb_ref, wc_ref, out_ref):
    y = ga_ref[...] * _dot(oa_ref[...], wa_ref[...].astype(BF16))
    y = y + gb_ref[...] * _dot(ob_ref[...], wb_ref[...].astype(BF16))
    y = y + gc_ref[...] * _dot(oc_ref[...], wc_ref[...].astype(BF16))
    out_ref[...] = y.astype(out_ref.dtype)


def gated_merge(o_a, o_b, o_c, gates_m, w_a, w_b, w_c, layer, tm=1024, tn=512):
    m = o_a.shape[0]
    nb = D_MODEL // tn
    act = lambda width: pl.BlockSpec((tm, width), lambda j, i: (i, 0))
    gate = lambda g: pl.BlockSpec((tm, tn), lambda j, i, g=g: (i, g * nb + j))
    wgt = lambda width: pl.BlockSpec((None, width, tn), lambda j, i: (layer, 0, j))
    return pl.pallas_call(
        _merge_kernel,
        grid=(nb, m // tm),
        in_specs=[act(o_a.shape[1]), act(o_b.shape[1]), act(o_c.shape[1]),
                  gate(0), gate(1), gate(2),
                  wgt(w_a.shape[1]), wgt(w_b.shape[1]), wgt(w_c.shape[1])],
        out_specs=pl.BlockSpec((tm, tn), lambda j, i: (i, j)),
        out_shape=jax.ShapeDtypeStruct((m, D_MODEL), BF16),
        compiler_params=_params("parallel", "parallel"),
        name="gated_merge",
    )(o_a, o_b, o_c, gates_m, gates_m, gates_m, w_a, w_b, w_c)


def _out_proj_kernel(y_ref, w_ref, x_ref, g_ref, xo_ref, h_ref):
    x_new = x_ref[...] + _dot(y_ref[...], w_ref[...])
    xo_ref[...] = x_new
    h_ref[...] = _rmsnorm_rows(x_new, g_ref[...]).astype(h_ref.dtype)


def out_proj_residual_norm(y, w_o, layer, x2, g, tm=512):
    m, d = x2.shape
    row = pl.BlockSpec((tm, d), lambda i: (i, 0))
    return pl.pallas_call(
        _out_proj_kernel,
        grid=(m // tm,),
        in_specs=[row, pl.BlockSpec((None, d, d), lambda i: (layer, 0, 0)), row,
                  pl.BlockSpec((1, d), lambda i: (0, 0))],
        out_specs=[row, row],
        out_shape=[jax.ShapeDtypeStruct((m, d), F32), jax.ShapeDtypeStruct((m, d), BF16)],
        compiler_params=_params("parallel"),
        name="out_proj",
    )(y, w_o, x2, g.reshape(1, d))


def _mlp_kernel(h_ref, w1_ref, w2_ref, x_ref, g_ref, *outs):
    acc_ref, hn_ref = outs[0], outs[-1]
    f = pl.program_id(1)

    @pl.when(f == 0)
    def _():
        acc_ref[...] = jnp.zeros_like(acc_ref)

    u = jnp.square(jnp.maximum(_dot(h_ref[...], w1_ref[...].astype(BF16)), 0.0))
    acc_ref[...] += _dot(u.astype(BF16), w2_ref[...].astype(BF16))

    @pl.when(f == pl.num_programs(1) - 1)
    def _():
        x_new = x_ref[...] + acc_ref[...]
        if len(outs) == 2:
            acc_ref[...] = x_new
        hn_ref[...] = _rmsnorm_rows(x_new, g_ref[...]).astype(hn_ref.dtype)


def mlp_residual_norm(h2, w1, w2, layer, x2, g_next, next_dtype, emit_x, tm=1024, tf=512):
    m, d = x2.shape
    row = pl.BlockSpec((tm, d), lambda i, f: (i, 0))
    row_once = pl.BlockSpec((tm, d), lambda i, f: (i, 0), pipeline_mode=pl.Buffered(1))
    out_shape = [jax.ShapeDtypeStruct((m, d), F32)] * emit_x + [jax.ShapeDtypeStruct((m, d), next_dtype)]
    res = pl.pallas_call(
        _mlp_kernel,
        grid=(m // tm, D_FF // tf),
        in_specs=[row, pl.BlockSpec((None, d, tf), lambda i, f: (layer, 0, f)),
                  pl.BlockSpec((None, tf, d), lambda i, f: (layer, f, 0)),
                  row_once, pl.BlockSpec((1, d), lambda i, f: (0, 0))],
        out_specs=[row_once] * len(out_shape),
        out_shape=out_shape,
        compiler_params=_params("parallel", "arbitrary"),
        name="mlp",
    )(h2, w1, w2, x2, g_next.reshape(1, d))
    return (res[0], res[1]) if emit_x else (None, res[0])


def _rope_tables():
    inv = ROPE_THETA ** (-jnp.arange(0, ROPE_DIM, 2, dtype=F32) / ROPE_DIM)
    ang = jnp.arange(SEQ, dtype=F32)[:, None] * inv[None, :]
    cos, sin = jnp.cos(ang), jnp.sin(ang)
    zeros = jnp.zeros((SEQ, HEAD_DIM - ROPE_DIM), F32)
    zero_h = jnp.zeros((SEQ, ROPE_HALF), F32)
    c = jnp.concatenate([cos, cos, jnp.ones_like(zeros)], axis=1)
    s_up = jnp.concatenate([-sin, zero_h, zeros], axis=1)
    s_dn = jnp.concatenate([zero_h, sin, zeros], axis=1)
    return c, s_up, s_dn


def _overlap_table_t():
    cs = np.arange(NSA_M_PAD)[None, :] * NSA_CMP_STRIDE
    bs = np.arange(NSA_NB)[:, None] * NSA_SEL_BLOCK
    ov = np.clip(np.minimum(cs + NSA_CMP_LEN, bs + NSA_SEL_BLOCK) - np.maximum(cs, bs), 0, None) / NSA_CMP_LEN
    ov[:, NSA_M_PAD - 1] = 0.0
    return jnp.asarray(ov, dtype=BF16)


W_IN_NAMES = ("a_q", "a_kc", "a_vc", "a_ks", "a_vs", "a_kw", "a_vw", "a_g",
              "b_q", "b_k", "b_v", "c_q", "c_k", "c_v", "m_a", "m_b", "m_c")
W_IN_START = dict(zip(W_IN_NAMES, np.cumsum((0,) + IN_SPLIT_SIZES[:-1]).tolist()))
W_IN_SIZE = dict(zip(W_IN_NAMES, IN_SPLIT_SIZES))


def _scales(*widths_and_values):
    return jnp.concatenate([jnp.full((1, w), v, F32) for w, v in widths_and_values], axis=1)


def _layer(x2, h, batch, layer, tabs, overlap_t, w_in_t, pe_k, w1_k, w2_k, pe_v, w1_v, w2_v,
           w_br_a, w_br_b, w_br_c, w_o, mlp_g, w_mlp_in, w_mlp_out, g_next, last):
    def proj(first, n, *args):
        return project(h, w_in_t, layer, W_IN_START[first], n, *args)

    def proj_t(name):
        return project_transposed(h, w_in_t, layer, W_IN_START[name], W_IN_SIZE[name], BF16)

    q_raw, q_rot = proj("a_q", A_Q, "both", BF16, tabs, _scales((A_Q, Q_SCALE)))
    cmp_src = proj("a_kc", 2 * A_KV, "plain", F32)
    k_sel = proj("a_ks", A_KV, "rope", BF16, tabs, _scales((A_KV, 1.0)))
    k_win = proj("a_kw", A_KV, "rope", BF16, tabs, _scales((A_KV, 1.0)))
    b_qk = proj("b_q", 2 * B_QKV, "rope", F32, tabs, _scales((B_QKV, Q_SCALE), (B_QKV, 1.0)))
    b_v = proj("b_v", B_QKV, "plain", F32)
    c_qk = proj("c_q", 2 * C_QKV, "rope", BF16, tabs, _scales((C_QKV, Q_SCALE), (C_QKV, 1.0)))
    gates_m = proj("m_a", 3 * D_MODEL, "sigmoid", BF16)
    gates_a = proj("a_g", LANES, "sigmoid", F32)
    vt_sel, vt_win, vt_moba = proj_t("a_vs"), proj_t("a_vw"), proj_t("c_v")

    flat = NSA_CMP_LEN * HEAD_DIM
    cmp_kv = nsa_compress(cmp_src, pe_k.reshape(1, flat), w1_k.reshape(flat, NSA_CMP_HIDDEN).astype(BF16),
                          w2_k.astype(BF16), pe_v.reshape(1, flat),
                          w1_v.reshape(flat, NSA_CMP_HIDDEN).astype(BF16), w2_v.astype(BF16), batch)
    o_a = nsa_attention(q_raw, q_rot, cmp_kv, k_sel, k_win, vt_sel, vt_win, gates_a, overlap_t, batch)
    o_b = dilated_attention(b_qk, b_v, batch)
    o_c = moba_attention(c_qk, vt_moba, batch)

    merged = gated_merge(o_a, o_b, o_c, gates_m, w_br_a, w_br_b, w_br_c, layer)
    x2, h2 = out_proj_residual_norm(merged, w_o, layer, x2, mlp_g)
    return mlp_residual_norm(h2, w_mlp_in, w_mlp_out, layer, x2, g_next, F32 if last else BF16, emit_x=not last)


def kernel(x, attn_norm_g, w_in, cmp_pe_k, cmp_w1_k, cmp_w2_k, cmp_pe_v, cmp_w1_v, cmp_w2_v,
           w_br_a, w_br_b, w_br_c, w_o, mlp_norm_g, w_mlp_in, w_mlp_out, final_norm_g):
    batch, seq, d = x.shape
    assert seq == SEQ and d == D_MODEL
    depth = w_in.shape[0]
    tabs = _rope_tables()
    overlap_t = _overlap_table_t()
    w_in_t = jnp.transpose(w_in, (0, 2, 1))
    w_o = w_o.astype(BF16)
    x2 = x.reshape(batch * seq, d)
    h = rmsnorm(x2, attn_norm_g[0], BF16)
    for l in range(depth):
        last = l == depth - 1
        g_next = final_norm_g if last else attn_norm_g[l + 1]
        x2, h = _layer(x2, h, batch, l, tabs, overlap_t, w_in_t,
                       cmp_pe_k[l], cmp_w1_k[l], cmp_w2_k[l], cmp_pe_v[l], cmp_w1_v[l], cmp_w2_v[l],
                       w_br_a, w_br_b, w_br_c, w_o, mlp_norm_g[l], w_mlp_in, w_mlp_out, g_next, last)
    return h.reshape(batch, seq, d)
```

```python
import functools
import math

import numpy as np
import jax
import jax.numpy as jnp
from jax import lax
from jax.experimental import pallas as pl
from jax.experimental.pallas import tpu as pltpu

D_MODEL = 2048
SEQ = 2048
HEAD_DIM = 128
ROPE_THETA = 500000.0
ROPE_DIM = HEAD_DIM // 4
ROPE_HALF = ROPE_DIM // 2
NORM_EPS = 1e-6
NEG_INF = -1e30
Q_SCALE = HEAD_DIM ** -0.5 * math.log2(math.e)

NSA_HEADS = 8
NSA_KV_HEADS = 2
NSA_REP = NSA_HEADS // NSA_KV_HEADS
NSA_CMP_LEN = 32
NSA_CMP_STRIDE = 16
NSA_CMP_HIDDEN = 256
NSA_SEL_BLOCK = 64
NSA_SEL_TOPN = 16
NSA_WINDOW = 512
NSA_FORCE_BONUS = 1e4
NSA_NB = SEQ // NSA_SEL_BLOCK
NSA_M_PAD = SEQ // NSA_CMP_STRIDE

DIL_GROUPS = ((128, 1), (512, 4), (2048, 16))
DIL_HEADS_PER_GROUP = 4
DIL_HEADS = DIL_HEADS_PER_GROUP * len(DIL_GROUPS)
DIL_TILE = 128
DIL_UNROLL = 16

MOBA_HEADS = 8
MOBA_BLOCK = 256
MOBA_TOPK = 3
MOBA_NB = SEQ // MOBA_BLOCK

D_FF = 4 * D_MODEL
A_Q = NSA_HEADS * HEAD_DIM
A_KV = NSA_KV_HEADS * HEAD_DIM
A_G = 3 * NSA_HEADS
B_QKV = DIL_HEADS * HEAD_DIM
C_QKV = MOBA_HEADS * HEAD_DIM
IN_SPLIT_SIZES = (A_Q, A_KV, A_KV, A_KV, A_KV, A_KV, A_KV, A_G,
                  B_QKV, B_QKV, B_QKV, C_QKV, C_QKV, C_QKV,
                  D_MODEL, D_MODEL, D_MODEL)

LANES = 128
F32_SUBLANES = 8
PROJ_TN_MAX = 512
PROJ_VMEM_BUDGET = 46 * 1024 * 1024
VMEM_LIMIT = 60 * 1024 * 1024

BF16 = jnp.bfloat16
F32 = jnp.float32


def _params(*sem):
    return pltpu.CompilerParams(dimension_semantics=sem, vmem_limit_bytes=VMEM_LIMIT)


def _dot(a, b):
    return jnp.dot(a, b, preferred_element_type=F32)


def _dot_nt(a, b):
    return lax.dot_general(a, b, (((1,), (1,)), ((), ())), preferred_element_type=F32)


def _iota(shape, axis):
    return lax.broadcasted_iota(jnp.int32, shape, axis)


def _rmsnorm_rows(x, g):
    y = x * lax.rsqrt(jnp.mean(x * x, axis=-1, keepdims=True) + NORM_EPS)
    return y * g


def _rmsnorm_kernel(x_ref, g_ref, h_ref):
    h_ref[...] = _rmsnorm_rows(x_ref[...], g_ref[...]).astype(h_ref.dtype)


def rmsnorm(x2, g, out_dtype, tm=512):
    m, d = x2.shape
    return pl.pallas_call(
        _rmsnorm_kernel,
        grid=(m // tm,),
        in_specs=[pl.BlockSpec((tm, d), lambda i: (i, 0)), pl.BlockSpec((1, d), lambda i: (0, 0))],
        out_specs=pl.BlockSpec((tm, d), lambda i: (i, 0)),
        out_shape=jax.ShapeDtypeStruct((m, d), out_dtype),
        compiler_params=_params("parallel"),
        name="rmsnorm",
    )(x2, g.reshape(1, d))


def _rope_lanes(acc, c, s_up, s_dn):
    tn = acc.shape[1]
    reps = tn // HEAD_DIM
    if reps > 1:
        c = jnp.concatenate([c] * reps, axis=1)
        s_up = jnp.concatenate([s_up] * reps, axis=1)
        s_dn = jnp.concatenate([s_dn] * reps, axis=1)
    up = pltpu.roll(acc, tn - ROPE_HALF, axis=1)
    dn = pltpu.roll(acc, ROPE_HALF, axis=1)
    return acc * c + up * s_up + dn * s_dn


def _sigmoid(x):
    return 0.5 * jnp.tanh(0.5 * x) + 0.5


def _proj_kernel(*refs, mode):
    if mode in ("rope", "both"):
        h_ref, w_ref, c_ref, su_ref, sd_ref, cs_ref = refs[:6]
        outs = refs[6:]
    else:
        h_ref, w_ref = refs[:2]
        outs = refs[2:]
    acc = _dot_nt(h_ref[...], w_ref[0].astype(BF16))
    if mode == "plain":
        outs[0][...] = acc.astype(outs[0].dtype)
    elif mode == "sigmoid":
        outs[0][...] = _sigmoid(acc).astype(outs[0].dtype)
    else:
        col_scale = cs_ref[...]
        roped = _rope_lanes(acc, c_ref[...], su_ref[...], sd_ref[...]) * col_scale
        if mode == "both":
            outs[0][...] = (acc * col_scale).astype(outs[0].dtype)
            outs[1][...] = roped.astype(outs[1].dtype)
        else:
            outs[0][...] = roped.astype(outs[0].dtype)


def _proj_tiles(k, n, w_bytes, out_bytes, n_out, rope):
    tn = math.gcd(PROJ_TN_MAX, n)
    for tm in (SEQ, SEQ // 2):
        blocks = (tm * k * 2 + k * tn * w_bytes + n_out * tm * tn * out_bytes
                  + (3 * tm * HEAD_DIM * 4 if rope else 0))
        if 2 * blocks <= PROJ_VMEM_BUDGET:
            break
    return tm, tn


def _weight_rows_spec(tn, k, layer, row0, step_axis):
    assert row0 % F32_SUBLANES == 0
    return pl.BlockSpec(
        (pl.Element(1), pl.Element(tn), pl.Element(k)),
        lambda *idx: (layer, pl.multiple_of(row0 + idx[step_axis] * tn, F32_SUBLANES), 0))


def project(h, wt, layer, col0, n, mode, out_dtype, rope_tabs=None, col_scale=None):
    m, k = h.shape
    n_out = 2 if mode == "both" else 1
    rope = mode in ("rope", "both")
    tm, tn = _proj_tiles(k, n, wt.dtype.itemsize, jnp.dtype(out_dtype).itemsize, n_out, rope)
    assert m % tm == 0 and n % tn == 0 and SEQ % tm == 0
    in_specs = [pl.BlockSpec((tm, k), lambda i, j: (i, 0)), _weight_rows_spec(tn, k, layer, col0, 1)]
    args = [h, wt]
    if rope:
        pos_blocks = SEQ // tm
        for t in rope_tabs:
            in_specs.append(pl.BlockSpec((tm, HEAD_DIM), lambda i, j: (i % pos_blocks, 0)))
            args.append(t)
        in_specs.append(pl.BlockSpec((1, tn), lambda i, j: (0, j)))
        args.append(col_scale)
    out_spec = pl.BlockSpec((tm, tn), lambda i, j: (i, j))
    out_shape = jax.ShapeDtypeStruct((m, n), out_dtype)
    res = pl.pallas_call(
        functools.partial(_proj_kernel, mode=mode),
        grid=(m // tm, n // tn),
        in_specs=in_specs,
        out_specs=[out_spec] * n_out,
        out_shape=[out_shape] * n_out,
        compiler_params=_params("parallel", "parallel"),
        name="proj_" + mode,
    )(*args)
    return res if n_out == 2 else res[0]


def _proj_t_kernel(wt_ref, h_ref, out_ref):
    out_ref[...] = _dot_nt(wt_ref[0].astype(BF16), h_ref[...]).astype(out_ref.dtype)


def project_transposed(h, wt, layer, col0, n, out_dtype, tm=1024):
    m, k = h.shape
    tn = math.gcd(PROJ_TN_MAX, n)
    return pl.pallas_call(
        _proj_t_kernel,
        grid=(n // tn, m // tm),
        in_specs=[_weight_rows_spec(tn, k, layer, col0, 0), pl.BlockSpec((tm, k), lambda j, i: (i, 0))],
        out_specs=pl.BlockSpec((tn, tm), lambda j, i: (j, i)),
        out_shape=jax.ShapeDtypeStruct((n, m), out_dtype),
        compiler_params=_params("parallel", "parallel"),
        name="proj_transposed",
    )(wt, h)


NSA_SIDE_BLOCK = A_KV
NSA_SIDE_NAMES = ("a_kc", "a_vc", "a_ks", "a_vs", "a_kw", "a_vw", "a_g")


def _nsa_side_kernel(h_ref, w_ref, c_ref, su_ref, sd_ref, cmp_ref, ksel_ref, kwin_ref, vts_ref, vtw_ref, ga_ref):
    j = pl.program_id(1)
    token_major = lambda: _dot_nt(h_ref[...], w_ref[0].astype(BF16))
    feature_major = lambda: _dot_nt(w_ref[0].astype(BF16), h_ref[...])
    roped = lambda: _rope_lanes(token_major(), c_ref[...], su_ref[...], sd_ref[...])

    @pl.when(j < 2)
    def _():
        cmp_ref[...] = token_major()

    @pl.when(j == 2)
    def _():
        ksel_ref[...] = roped().astype(ksel_ref.dtype)

    @pl.when(j == 3)
    def _():
        vts_ref[...] = feature_major().astype(vts_ref.dtype)

    @pl.when(j == 4)
    def _():
        kwin_ref[...] = roped().astype(kwin_ref.dtype)

    @pl.when(j == 5)
    def _():
        vtw_ref[...] = feature_major().astype(vtw_ref.dtype)

    @pl.when(j == 6)
    def _():
        ga_ref[...] = _sigmoid(token_major())


def project_nsa_side(h, wt, layer, rope_tabs, tm=SEQ):
    m, k = h.shape
    blk = NSA_SIDE_BLOCK
    row0 = W_IN_START[NSA_SIDE_NAMES[0]]
    assert all(W_IN_START[n] == row0 + i * blk for i, n in enumerate(NSA_SIDE_NAMES))
    rows = lambda cols: pl.BlockSpec((tm, blk), cols)
    fixed = rows(lambda i, j: (i, 0))
    feat = pl.BlockSpec((blk, tm), lambda i, j: (0, i))
    tab = pl.BlockSpec((tm, HEAD_DIM), lambda i, j: (0, 0))
    return pl.pallas_call(
        _nsa_side_kernel,
        grid=(m // tm, len(NSA_SIDE_NAMES)),
        in_specs=[pl.BlockSpec((tm, k), lambda i, j: (i, 0)), _weight_rows_spec(blk, k, layer, row0, 1), tab, tab, tab],
        out_specs=[rows(lambda i, j: (i, jnp.minimum(j, 1))), fixed, fixed, feat, feat, fixed],
        out_shape=[jax.ShapeDtypeStruct((m, 2 * blk), F32), jax.ShapeDtypeStruct((m, blk), BF16),
                   jax.ShapeDtypeStruct((m, blk), BF16), jax.ShapeDtypeStruct((blk, m), BF16),
                   jax.ShapeDtypeStruct((blk, m), BF16), jax.ShapeDtypeStruct((m, blk), F32)],
        compiler_params=_params("parallel", "arbitrary"),
        name="proj_nsa_side",
    )(h, wt, *rope_tabs)


def _compress_kernel(k0_ref, k1_ref, v0_ref, v1_ref, pek_ref, w1k_ref, w2k_ref, pev_ref, w1v_ref, w2v_ref,
                     out_ref):
    half = NSA_CMP_STRIDE * HEAD_DIM
    for idx, src_ref in enumerate((k0_ref, k1_ref, v0_ref, v1_ref)):
        is_k = idx < NSA_KV_HEADS
        pe_ref, w1_ref, w2_ref = (pek_ref, w1k_ref, w2k_ref) if is_k else (pev_ref, w1v_ref, w2v_ref)
        x = jnp.concatenate(
            [src_ref[pl.ds(l, NSA_M_PAD, stride=NSA_CMP_STRIDE), :]
             for l in range(NSA_CMP_STRIDE)], axis=1)
        pe = pe_ref[...]
        first = _dot((x + pe[:, :half]).astype(BF16), w1_ref[:half, :])
        second = _dot((x + pe[:, half:]).astype(BF16), w1_ref[half:, :])
        hid = jax.nn.gelu(first + pltpu.roll(second, NSA_M_PAD - 1, axis=0))
        out = _dot(hid.astype(BF16), w2_ref[...])
        out_ref[idx] = (out if is_k else jnp.transpose(out)).astype(out_ref.dtype)


def nsa_compress(pf, pe_k, w1_k, w2_k, pe_v, w1_v, w2_v, batch):
    flat = NSA_CMP_LEN * HEAD_DIM
    const = lambda shape: pl.BlockSpec(shape, lambda b: (0,) * len(shape))
    return pl.pallas_call(
        _compress_kernel,
        grid=(batch,),
        in_specs=[pl.BlockSpec((SEQ, HEAD_DIM), lambda b, c=c: (b, c)) for c in range(4)] + [
                  const((1, flat)), const((flat, NSA_CMP_HIDDEN)), const((NSA_CMP_HIDDEN, HEAD_DIM)),
                  const((1, flat)), const((flat, NSA_CMP_HIDDEN)), const((NSA_CMP_HIDDEN, HEAD_DIM))],
        out_specs=pl.BlockSpec((None, 4, NSA_M_PAD, HEAD_DIM), lambda b: (b, 0, 0, 0)),
        out_shape=jax.ShapeDtypeStruct((batch, 4, NSA_M_PAD, HEAD_DIM), BF16),
        compiler_params=_params("parallel"),
        name="nsa_compress",
    )(pf, pf, pf, pf, pe_k, w1_k, w2_k, pe_v, w1_v, w2_v)


NSA_TQ = 256
NSA_KC = 256
NSA_BLK_PER_CHUNK = NSA_KC // NSA_SEL_BLOCK


def _nsa_kernel(qraw_ref, qrot_ref, cmp_ref, ksel_ref, vselt_ref, kwin_ref, vwint_ref,
                gate_ref, ovt_ref, o_ref, bias_scr):
    i = pl.program_id(1)
    tq, kc_w, rep, d = NSA_TQ, NSA_KC, NSA_REP, HEAD_DIM
    groups = range(NSA_KV_HEADS)
    width = rep * tq
    bpc = NSA_BLK_PER_CHUNK
    t0 = i * tq
    head_cols = lambda g, r: slice((g * rep + r) * d, (g * rep + r + 1) * d)
    stack = lambda ref, g: jnp.concatenate([ref[:, head_cols(g, r)] for r in range(rep)], axis=0)
    q_raw = [stack(qraw_ref, g) for g in groups]
    q_rot = [stack(qrot_ref, g) for g in groups]
    k_of = lambda ref, g, kj: ref[pl.ds(pl.multiple_of(kj * kc_w, kc_w), kc_w), g * d:(g + 1) * d]
    vt_of = lambda ref, g, kj: ref[g * d:(g + 1) * d, pl.ds(pl.multiple_of(kj * kc_w, kc_w), kc_w)]
    col_max = lambda s: jnp.max(s, axis=0, keepdims=True)
    col_sum = lambda p: jnp.sum(p, axis=0, keepdims=True)

    q_pos = t0 + (_iota((NSA_M_PAD, width), 1) & (tq - 1))
    vis = (_iota((NSA_M_PAD, width), 0) * NSA_CMP_STRIDE + (NSA_CMP_LEN - 1)) <= q_pos
    sc = [jnp.where(vis, _dot_nt(cmp_ref[g], q_raw[g]), NEG_INF) for g in groups]
    ec = [jnp.where(vis, jnp.exp2(s - col_max(s)), 0.0) for s in sc]
    pc = [(e / jnp.maximum(col_sum(e), 1e-30)).astype(BF16) for e in ec]
    o_cmp = [_dot(cmp_ref[NSA_KV_HEADS + g], pc[g]) for g in groups]
    imp_heads = [_dot(ovt_ref[...], p) for p in pc]

    blk = _iota((NSA_NB, tq), 0)
    q_blk = (t0 + _iota((NSA_NB, tq), 1)) // NSA_SEL_BLOCK
    forced = (blk == 0) | (blk == q_blk) | (blk == q_blk - 1)
    for g in groups:
        imp = imp_heads[g][:, :tq]
        for r in range(1, rep):
            imp = imp + imp_heads[g][:, r * tq:(r + 1) * tq]
        score = jnp.where(blk <= q_blk, imp + jnp.where(forced, NSA_FORCE_BONUS, 0.0), NEG_INF)
        rank = jnp.zeros((NSA_NB, tq), F32)
        for j in range(NSA_NB):
            sj = score[j:j + 1, :]
            ahead = (sj > score) | ((sj == score) & (blk > j))
            rank = rank + jnp.where(ahead, 1.0, 0.0)
        bias = jnp.where((rank < float(NSA_SEL_TOPN)) & (score > NEG_INF * 0.5), 0.0, NEG_INF)
        bias = jnp.concatenate([bias] * rep, axis=1)
        for c in range(SEQ // kc_w):
            bias_scr[g, c] = bias[c * bpc:(c + 1) * bpc, :]

    def sel_step(kj, ss, carry):
        bs = [bias_scr[g, kj] for g in groups]
        s3 = [s.reshape(bpc, NSA_SEL_BLOCK, width) for s in ss]
        m_new = [jnp.maximum(carry[g][0], col_max(jnp.max(s3[g], axis=1) + bs[g])) for g in groups]
        alpha = [jnp.exp2(carry[g][0] - m_new[g]) for g in groups]
        ps = [jnp.exp2(s3[g] - (m_new[g] - bs[g])[:, None, :]).reshape(kc_w, width) for g in groups]
        ls = [alpha[g] * carry[g][1] + col_sum(ps[g]) for g in groups]
        pvs = [_dot(vt_of(vselt_ref, g, kj), ps[g].astype(BF16)) for g in groups]
        return tuple((m_new[g], ls[g], alpha[g] * carry[g][2] + pvs[g]) for g in groups)

    init = tuple((jnp.full((1, width), NEG_INF, F32), jnp.zeros((1, width), F32), jnp.zeros((d, width), F32))
                 for g in groups)
    carry = lax.fori_loop(
        0, i, lambda kj, c: sel_step(kj, [_dot_nt(k_of(ksel_ref, g, kj), q_rot[g]) for g in groups], c), init)

    key_row = _iota((kc_w, width), 0)
    q_col = _iota((kc_w, width), 1) & (tq - 1)
    causal = key_row <= q_col
    diag = [jnp.where(causal, _dot_nt(k_of(ksel_ref, g, i), q_rot[g]), NEG_INF) for g in groups]
    sel = sel_step(i, diag, carry)

    far = i - 2
    near = i - 1
    far_ok = jnp.where(far >= 0, 0.0, NEG_INF)
    near_ok = jnp.where(near >= 0, 0.0, NEG_INF)
    far_c = jnp.maximum(far, 0)
    near_c = jnp.maximum(near, 0)
    s_own = [jnp.where(causal, _dot_nt(k_of(kwin_ref, g, i), q_rot[g]), NEG_INF) for g in groups]
    s_near = [_dot_nt(k_of(kwin_ref, g, near_c), q_rot[g]) + near_ok for g in groups]
    s_far = [jnp.where(key_row > q_col, _dot_nt(k_of(kwin_ref, g, far_c), q_rot[g]), NEG_INF) + far_ok
             for g in groups]
    m_w = [jnp.maximum(jnp.maximum(col_max(s_own[g]), col_max(s_near[g])), col_max(s_far[g])) for g in groups]
    p_own = [jnp.exp2(s_own[g] - m_w[g]) for g in groups]
    p_near = [jnp.exp2(s_near[g] - m_w[g]) for g in groups]
    p_far = [jnp.exp2(s_far[g] - m_w[g]) for g in groups]
    l_win = [col_sum(p_own[g]) + col_sum(p_near[g]) + col_sum(p_far[g]) for g in groups]
    acc_win = [_dot(vt_of(vwint_ref, g, i), p_own[g].astype(BF16))
               + _dot(vt_of(vwint_ref, g, near_c), p_near[g].astype(BF16))
               + _dot(vt_of(vwint_ref, g, far_c), p_far[g].astype(BF16)) for g in groups]

    gates = jnp.transpose(gate_ref[...])
    for g in groups:
        o_sel = sel[g][2] / sel[g][1]
        o_win = acc_win[g] / l_win[g]
        for r in range(rep):
            lanes = slice(r * tq, (r + 1) * tq)
            row = 3 * (g * rep + r)
            o = (gates[row:row + 1, :] * o_cmp[g][:, lanes] + gates[row + 1:row + 2, :] * o_sel[:, lanes]
                 + gates[row + 2:row + 3, :] * o_win[:, lanes])
            o_ref[:, head_cols(g, r)] = jnp.transpose(o).astype(o_ref.dtype)


def nsa_attention(q_raw, q_rot, cmp_kv, k_sel, k_win, vt_sel, vt_win, gates_a, overlap_t, batch):
    tq = NSA_TQ
    nq = SEQ // tq
    g_n = NSA_KV_HEADS
    row = lambda b, i: (b * nq + i, 0)
    return pl.pallas_call(
        _nsa_kernel,
        grid=(batch, nq),
        in_specs=[pl.BlockSpec((tq, A_Q), row),
                  pl.BlockSpec((tq, A_Q), row),
                  pl.BlockSpec((None, 2 * g_n, NSA_M_PAD, HEAD_DIM), lambda b, i: (b, 0, 0, 0)),
                  pl.BlockSpec((SEQ, A_KV), lambda b, i: (b, 0)),
                  pl.BlockSpec((A_KV, SEQ), lambda b, i: (0, b)),
                  pl.BlockSpec((SEQ, A_KV), lambda b, i: (b, 0)),
                  pl.BlockSpec((A_KV, SEQ), lambda b, i: (0, b)),
                  pl.BlockSpec((tq, LANES), row),
                  pl.BlockSpec((NSA_NB, NSA_M_PAD), lambda b, i: (0, 0))],
        out_specs=pl.BlockSpec((tq, A_Q), row),
        out_shape=jax.ShapeDtypeStruct((batch * SEQ, A_Q), BF16),
        scratch_shapes=[pltpu.VMEM((g_n, SEQ // NSA_KC, NSA_BLK_PER_CHUNK, NSA_REP * tq), F32)],
        compiler_params=_params("parallel", "parallel"),
        name="nsa_attention",
    )(q_raw, q_rot, cmp_kv, k_sel, vt_sel, k_win, vt_win, gates_a, overlap_t)


def _dilated_kernel(q0, k0, v0, q1, k1, v1, q2, k2, v2, o_ref,
                    o0_scr, o1_scr, o2_scr, l0_scr, l1_scr, l2_scr):
    tile = DIL_TILE
    groups = ((q0, k0, v0, o0_scr, l0_scr), (q1, k1, v1, o1_scr, l1_scr), (q2, k2, v2, o2_scr, l2_scr))
    for (window, dil), (q_ref, k_ref, v_ref, og_scr, lg_scr) in zip(DIL_GROUPS, groups):
        assert window // dil == tile
        per_class = SEQ // dil
        tiles_per_class = per_class // tile
        nk = tile if tiles_per_class == 1 else 2 * tile
        a_minus_a = _iota((tile, nk), 0) - _iota((tile, nk), 1)

        def step(u, carry, q_ref=q_ref, k_ref=k_ref, v_ref=v_ref, og_scr=og_scr, lg_scr=lg_scr,
                 dil=dil, tiles_per_class=tiles_per_class, nk=nk, a_minus_a=a_minus_a):
            ts = [u * DIL_UNROLL + a for a in range(DIL_UNROLL)]
            cls = [t // tiles_per_class for t in ts]
            n0 = [(t % tiles_per_class) * tile for t in ts]
            kbase = [jnp.maximum(n - (nk - tile), 0) for n in n0]
            q_rows = [pl.ds(c + dil * n, tile, stride=dil) for c, n in zip(cls, n0)]
            k_rows = [pl.ds(c + dil * kb, nk, stride=dil) for c, kb in zip(cls, kbase)]
            qs = [q_ref[r, :].astype(BF16) for r in q_rows]
            ks = [k_ref[r, :].astype(BF16) for r in k_rows]
            vs = [v_ref[r, :].astype(BF16) for r in k_rows]
            ss = [_dot_nt(q, k) for q, k in zip(qs, ks)]
            masks = []
            for n, kb in zip(n0, kbase):
                dist = a_minus_a + (n - kb)
                masks.append((dist >= 0) & (dist <= tile))
            ss = [jnp.where(mk, s, NEG_INF) for mk, s in zip(masks, ss)]
            ms = [jnp.max(s, axis=-1, keepdims=True) for s in ss]
            es = [jnp.where(mk, jnp.exp2(s - m), 0.0) for mk, s, m in zip(masks, ss, ms)]
            dens = [jnp.maximum(jnp.sum(e, axis=-1, keepdims=True), 1e-30) for e in es]
            os_ = [_dot((e / den).astype(BF16), v) for e, den, v in zip(es, dens, vs)]
            for r, o, m, den in zip(q_rows, os_, ms, dens):
                og_scr[r, :] = o
                lg_scr[r, :] = jnp.broadcast_to(m + jnp.log2(den), (tile, HEAD_DIM))
            return carry

        lax.fori_loop(0, SEQ // tile // DIL_UNROLL, step, 0)

    rows = 256

    def merge_body(c, carry):
        sl = pl.ds(pl.multiple_of(c * rows, rows), rows)
        la, lb, lc = l0_scr[sl, :], l1_scr[sl, :], l2_scr[sl, :]
        mx = jnp.maximum(jnp.maximum(la, lb), lc)
        ea, eb, ec = jnp.exp2(la - mx), jnp.exp2(lb - mx), jnp.exp2(lc - mx)
        tot = ea + eb + ec
        out = (ea / tot) * o0_scr[sl, :] + (eb / tot) * o1_scr[sl, :] + (ec / tot) * o2_scr[sl, :]
        o_ref[sl, :] = out.astype(o_ref.dtype)
        return carry

    lax.fori_loop(0, SEQ // rows, merge_body, 0)


def dilated_attention(b_qk, b_v, batch):
    hp = DIL_HEADS_PER_GROUP
    in_specs, args = [], []
    for g in range(len(DIL_GROUPS)):
        for arr, c0 in ((b_qk, 0), (b_qk, DIL_HEADS), (b_v, 0)):
            in_specs.append(pl.BlockSpec((SEQ, HEAD_DIM), lambda b, j, c0=c0, g=g: (b, c0 + g * hp + j)))
            args.append(arr)
    return pl.pallas_call(
        _dilated_kernel,
        grid=(batch, hp),
        in_specs=in_specs,
        out_specs=pl.BlockSpec((SEQ, HEAD_DIM), lambda b, j: (b, j)),
        out_shape=jax.ShapeDtypeStruct((batch * SEQ, hp * HEAD_DIM), BF16),
        scratch_shapes=[pltpu.VMEM((SEQ, HEAD_DIM), F32)] * 6,
        compiler_params=_params("parallel", "parallel"),
        name="dilated_attention",
    )(*args)


def _moba_kernel(q_ref, k_ref, vt_ref, o_ref, kmean_scr, bias_scr):
    i = pl.program_id(1)
    tq, nb, d = MOBA_BLOCK, MOBA_NB, HEAD_DIM
    heads = range(MOBA_HEADS)
    col = lambda h: slice(h * d, (h + 1) * d)

    @pl.when(i == 0)
    def _():
        avg = jnp.where(_iota((nb, SEQ), 1) // MOBA_BLOCK == _iota((nb, SEQ), 0), 1.0 / MOBA_BLOCK, 0.0)
        kmean_scr[...] = _dot(avg.astype(BF16), k_ref[...]).astype(kmean_scr.dtype)

    blk = _iota((nb, tq), 0)
    own = pl.multiple_of(i * tq, tq)
    causal = _iota((tq, tq), 0) <= _iota((tq, tq), 1)
    qs = [q_ref[:, col(h)] for h in heads]
    gates = [jnp.where(blk < i, _dot_nt(kmean_scr[:, col(h)], qs[h]), NEG_INF) for h in heads]
    ss = [_dot_nt(k_ref[pl.ds(own, tq), col(h)], qs[h]) for h in heads]
    for h in heads:
        gate = gates[h]
        rank = jnp.zeros((nb, tq), F32)
        for j in range(nb):
            gj = gate[j:j + 1, :]
            ahead = (gj > gate) | ((gj == gate) & (blk > j))
            rank = rank + jnp.where(ahead, 1.0, 0.0)
        bias_scr[h] = jnp.where((rank < float(MOBA_TOPK)) & (gate > NEG_INF * 0.5), 0.0, NEG_INF)
    ss = [jnp.where(causal, s, NEG_INF) for s in ss]
    ms = [jnp.max(s, axis=0, keepdims=True) for s in ss]
    ps = [jnp.exp2(s - m) for s, m in zip(ss, ms)]
    ls = [jnp.sum(p, axis=0, keepdims=True) for p in ps]
    accs = [_dot(vt_ref[col(h), pl.ds(own, tq)], ps[h].astype(BF16)) for h in heads]

    def body(kj, carry):
        off = pl.multiple_of(kj * tq, tq)
        bs = [bias_scr[h, pl.ds(kj, 1), :] for h in heads]
        ss = [_dot_nt(k_ref[pl.ds(off, tq), col(h)], qs[h]) for h in heads]
        m_new = [jnp.maximum(carry[h][0], jnp.max(ss[h], axis=0, keepdims=True) + bs[h]) for h in heads]
        alpha = [jnp.exp2(carry[h][0] - m_new[h]) for h in heads]
        ps = [jnp.exp2(ss[h] - (m_new[h] - bs[h])) for h in heads]
        ls = [alpha[h] * carry[h][1] + jnp.sum(ps[h], axis=0, keepdims=True) for h in heads]
        pvs = [_dot(vt_ref[col(h), pl.ds(off, tq)], ps[h].astype(BF16)) for h in heads]
        return tuple((m_new[h], ls[h], alpha[h] * carry[h][2] + pvs[h]) for h in heads)

    fin = lax.fori_loop(0, i, body, tuple((ms[h], ls[h], accs[h]) for h in heads))
    for h in heads:
        _, l, acc = fin[h]
        o_ref[:, col(h)] = jnp.transpose(acc / l).astype(o_ref.dtype)


def moba_attention(c_qk, vt, batch):
    tq = MOBA_BLOCK
    nq = SEQ // tq
    return pl.pallas_call(
        _moba_kernel,
        grid=(batch, nq),
        in_specs=[pl.BlockSpec((tq, C_QKV), lambda b, i: (b * nq + i, 0)),
                  pl.BlockSpec((SEQ, C_QKV), lambda b, i: (b, 1)),
                  pl.BlockSpec((C_QKV, SEQ), lambda b, i: (0, b))],
        out_specs=pl.BlockSpec((tq, C_QKV), lambda b, i: (b * nq + i, 0)),
        out_shape=jax.ShapeDtypeStruct((batch * SEQ, C_QKV), BF16),
        scratch_shapes=[pltpu.VMEM((MOBA_NB, C_QKV), BF16), pltpu.VMEM((MOBA_HEADS, MOBA_NB, tq), F32)],
        compiler_params=_params("parallel", "arbitrary"),
        name="moba_attention",
    )(c_qk, c_qk, vt)


def _merge_kernel(oa_ref, ob_ref, oc_ref, ga_ref, gb_ref, gc_ref, wa_ref, wb_ref, wc_ref, out_ref):
    y = ga_ref[...] * _dot(oa_ref[...], wa_ref[...].astype(BF16))
    y = y + gb_ref[...] * _dot(ob_ref[...], wb_ref[...].astype(BF16))
    y = y + gc_ref[...] * _dot(oc_ref[...], wc_ref[...].astype(BF16))
    out_ref[...] = y.astype(out_ref.dtype)


def gated_merge(o_a, o_b, o_c, gates_m, w_a, w_b, w_c, layer, tm=1024, tn=512):
    m = o_a.shape[0]
    nb = D_MODEL // tn
    act = lambda width: pl.BlockSpec((tm, width), lambda j, i: (i, 0))
    gate = lambda g: pl.BlockSpec((tm, tn), lambda j, i, g=g: (i, g * nb + j))
    wgt = lambda width: pl.BlockSpec((None, width, tn), lambda j, i: (layer, 0, j))
    return pl.pallas_call(
        _merge_kernel,
        grid=(nb, m // tm),
        in_specs=[act(o_a.shape[1]), act(o_b.shape[1]), act(o_c.shape[1]),
                  gate(0), gate(1), gate(2),
                  wgt(w_a.shape[1]), wgt(w_b.shape[1]), wgt(w_c.shape[1])],
        out_specs=pl.BlockSpec((tm, tn), lambda j, i: (i, j)),
        out_shape=jax.ShapeDtypeStruct((m, D_MODEL), BF16),
        compiler_params=_params("parallel", "parallel"),
        name="gated_merge",
    )(o_a, o_b, o_c, gates_m, gates_m, gates_m, w_a, w_b, w_c)


def _out_proj_kernel(y_ref, w_ref, x_ref, g_ref, xo_ref, h_ref):
    x_new = x_ref[...] + _dot(y_ref[...], w_ref[...])
    xo_ref[...] = x_new
    h_ref[...] = _rmsnorm_rows(x_new, g_ref[...]).astype(h_ref.dtype)


def out_proj_residual_norm(y, w_o, layer, x2, g, tm=512):
    m, d = x2.shape
    row = pl.BlockSpec((tm, d), lambda i: (i, 0))
    return pl.pallas_call(
        _out_proj_kernel,
        grid=(m // tm,),
        in_specs=[row, pl.BlockSpec((None, d, d), lambda i: (layer, 0, 0)), row,
                  pl.BlockSpec((1, d), lambda i: (0, 0))],
        out_specs=[row, row],
        out_shape=[jax.ShapeDtypeStruct((m, d), F32), jax.ShapeDtypeStruct((m, d), BF16)],
        compiler_params=_params("parallel"),
        name="out_proj",
    )(y, w_o, x2, g.reshape(1, d))


def _mlp_kernel(h_ref, w1_ref, w2_ref, x_ref, g_ref, *outs):
    acc_ref, hn_ref = outs[0], outs[-1]
    f = pl.program_id(1)

    @pl.when(f == 0)
    def _():
        acc_ref[...] = jnp.zeros_like(acc_ref)

    u = jnp.square(jnp.maximum(_dot(h_ref[...], w1_ref[...].astype(BF16)), 0.0))
    acc_ref[...] += _dot(u.astype(BF16), w2_ref[...].astype(BF16))

    @pl.when(f == pl.num_programs(1) - 1)
    def _():
        x_new = x_ref[...] + acc_ref[...]
        if len(outs) == 2:
            acc_ref[...] = x_new
        hn_ref[...] = _rmsnorm_rows(x_new, g_ref[...]).astype(hn_ref.dtype)


def mlp_residual_norm(h2, w1, w2, layer, x2, g_next, next_dtype, emit_x, tm=1024, tf=512):
    m, d = x2.shape
    row = pl.BlockSpec((tm, d), lambda i, f: (i, 0))
    row_once = pl.BlockSpec((tm, d), lambda i, f: (i, 0), pipeline_mode=pl.Buffered(1))
    out_shape = [jax.ShapeDtypeStruct((m, d), F32)] * emit_x + [jax.ShapeDtypeStruct((m, d), next_dtype)]
    res = pl.pallas_call(
        _mlp_kernel,
        grid=(m // tm, D_FF // tf),
        in_specs=[row, pl.BlockSpec((None, d, tf), lambda i, f: (layer, 0, f)),
                  pl.BlockSpec((None, tf, d), lambda i, f: (layer, f, 0)),
                  row_once, pl.BlockSpec((1, d), lambda i, f: (0, 0))],
        out_specs=[row_once] * len(out_shape),
        out_shape=out_shape,
        compiler_params=_params("parallel", "arbitrary"),
        name="mlp",
    )(h2, w1, w2, x2, g_next.reshape(1, d))
    return (res[0], res[1]) if emit_x else (None, res[0])


def _rope_tables():
    inv = ROPE_THETA ** (-jnp.arange(0, ROPE_DIM, 2, dtype=F32) / ROPE_DIM)
    ang = jnp.arange(SEQ, dtype=F32)[:, None] * inv[None, :]
    cos, sin = jnp.cos(ang), jnp.sin(ang)
    zeros = jnp.zeros((SEQ, HEAD_DIM - ROPE_DIM), F32)
    zero_h = jnp.zeros((SEQ, ROPE_HALF), F32)
    c = jnp.concatenate([cos, cos, jnp.ones_like(zeros)], axis=1)
    s_up = jnp.concatenate([-sin, zero_h, zeros], axis=1)
    s_dn = jnp.concatenate([zero_h, sin, zeros], axis=1)
    return c, s_up, s_dn


def _overlap_table_t():
    cs = np.arange(NSA_M_PAD)[None, :] * NSA_CMP_STRIDE
    bs = np.arange(NSA_NB)[:, None] * NSA_SEL_BLOCK
    ov = np.clip(np.minimum(cs + NSA_CMP_LEN, bs + NSA_SEL_BLOCK) - np.maximum(cs, bs), 0, None) / NSA_CMP_LEN
    ov[:, NSA_M_PAD - 1] = 0.0
    return jnp.asarray(ov, dtype=BF16)


W_IN_NAMES = ("a_q", "a_kc", "a_vc", "a_ks", "a_vs", "a_kw", "a_vw", "a_g",
              "b_q", "b_k", "b_v", "c_q", "c_k", "c_v", "m_a", "m_b", "m_c")
W_IN_START = dict(zip(W_IN_NAMES, np.cumsum((0,) + IN_SPLIT_SIZES[:-1]).tolist()))
W_IN_SIZE = dict(zip(W_IN_NAMES, IN_SPLIT_SIZES))


def _scales(*widths_and_values):
    return jnp.concatenate([jnp.full((1, w), v, F32) for w, v in widths_and_values], axis=1)


def _layer(x2, h, batch, layer, tabs, overlap_t, w_in_t, pe_k, w1_k, w2_k, pe_v, w1_v, w2_v,
           w_br_a, w_br_b, w_br_c, w_o, mlp_g, w_mlp_in, w_mlp_out, g_next, last):
    def proj(first, n, *args):
        return project(h, w_in_t, layer, W_IN_START[first], n, *args)

    def proj_t(name):
        return project_transposed(h, w_in_t, layer, W_IN_START[name], W_IN_SIZE[name], BF16)

    q_raw, q_rot = proj("a_q", A_Q, "both", BF16, tabs, _scales((A_Q, Q_SCALE)))
    cmp_src, k_sel, k_win, vt_sel, vt_win, gates_a = project_nsa_side(h, w_in_t, layer, tabs)
    b_qk = proj("b_q", 2 * B_QKV, "rope", F32, tabs, _scales((B_QKV, Q_SCALE), (B_QKV, 1.0)))
    b_v = proj("b_v", B_QKV, "plain", F32)
    c_qk = proj("c_q", 2 * C_QKV, "rope", BF16, tabs, _scales((C_QKV, Q_SCALE), (C_QKV, 1.0)))
    gates_m = proj("m_a", 3 * D_MODEL, "sigmoid", BF16)
    vt_moba = proj_t("c_v")

    flat = NSA_CMP_LEN * HEAD_DIM
    cmp_kv = nsa_compress(cmp_src, pe_k.reshape(1, flat), w1_k.reshape(flat, NSA_CMP_HIDDEN).astype(BF16),
                          w2_k.astype(BF16), pe_v.reshape(1, flat),
                          w1_v.reshape(flat, NSA_CMP_HIDDEN).astype(BF16), w2_v.astype(BF16), batch)
    o_a = nsa_attention(q_raw, q_rot, cmp_kv, k_sel, k_win, vt_sel, vt_win, gates_a, overlap_t, batch)
    o_b = dilated_attention(b_qk, b_v, batch)
    o_c = moba_attention(c_qk, vt_moba, batch)

    merged = gated_merge(o_a, o_b, o_c, gates_m, w_br_a, w_br_b, w_br_c, layer)
    x2, h2 = out_proj_residual_norm(merged, w_o, layer, x2, mlp_g)
    return mlp_residual_norm(h2, w_mlp_in, w_mlp_out, layer, x2, g_next, F32 if last else BF16, emit_x=not last)


def kernel(x, attn_norm_g, w_in, cmp_pe_k, cmp_w1_k, cmp_w2_k, cmp_pe_v, cmp_w1_v, cmp_w2_v,
           w_br_a, w_br_b, w_br_c, w_o, mlp_norm_g, w_mlp_in, w_mlp_out, final_norm_g):
    batch, seq, d = x.shape
    assert seq == SEQ and d == D_MODEL
    depth = w_in.shape[0]
    tabs = _rope_tables()
    overlap_t = _overlap_table_t()
    w_in_t = jnp.transpose(w_in, (0, 2, 1))
    w_o = w_o.astype(BF16)
    x2 = x.reshape(batch * seq, d)
    h = rmsnorm(x2, attn_norm_g[0], BF16)
    for l in range(depth):
        last = l == depth - 1
        g_next = final_norm_g if last else attn_norm_g[l + 1]
        x2, h = _layer(x2, h, batch, l, tabs, overlap_t, w_in_t,
                       cmp_pe_k[l], cmp_w1_k[l], cmp_w2_k[l], cmp_pe_v[l], cmp_w1_v[l], cmp_w2_v[l],
                       w_br_a, w_br_b, w_br_c, w_o, mlp_norm_g[l], w_mlp_in, w_mlp_out, g_next, last)
    return h.reshape(batch, seq, d)
```

```python
import functools
import math

import numpy as np
import jax
import jax.numpy as jnp
from jax import lax
from jax.experimental import pallas as pl
from jax.experimental.pallas import tpu as pltpu

D_MODEL = 2048
SEQ = 2048
HEAD_DIM = 128
ROPE_THETA = 500000.0
ROPE_DIM = HEAD_DIM // 4
ROPE_HALF = ROPE_DIM // 2
NORM_EPS = 1e-6
NEG_INF = -1e30
Q_SCALE = HEAD_DIM ** -0.5 * math.log2(math.e)

NSA_HEADS = 8
NSA_KV_HEADS = 2
NSA_REP = NSA_HEADS // NSA_KV_HEADS
NSA_CMP_LEN = 32
NSA_CMP_STRIDE = 16
NSA_CMP_HIDDEN = 256
NSA_SEL_BLOCK = 64
NSA_SEL_TOPN = 16
NSA_WINDOW = 512
NSA_FORCE_BONUS = 1e4
NSA_NB = SEQ // NSA_SEL_BLOCK
NSA_M_PAD = SEQ // NSA_CMP_STRIDE

DIL_GROUPS = ((128, 1), (512, 4), (2048, 16))
DIL_HEADS_PER_GROUP = 4
DIL_HEADS = DIL_HEADS_PER_GROUP * len(DIL_GROUPS)
DIL_TILE = 128
DIL_UNROLL = 16

MOBA_HEADS = 8
MOBA_BLOCK = 256
MOBA_TOPK = 3
MOBA_NB = SEQ // MOBA_BLOCK

D_FF = 4 * D_MODEL
A_Q = NSA_HEADS * HEAD_DIM
A_KV = NSA_KV_HEADS * HEAD_DIM
A_G = 3 * NSA_HEADS
B_QKV = DIL_HEADS * HEAD_DIM
C_QKV = MOBA_HEADS * HEAD_DIM
IN_SPLIT_SIZES = (A_Q, A_KV, A_KV, A_KV, A_KV, A_KV, A_KV, A_G,
                  B_QKV, B_QKV, B_QKV, C_QKV, C_QKV, C_QKV,
                  D_MODEL, D_MODEL, D_MODEL)

LANES = 128
F32_SUBLANES = 8
PROJ_TN_MAX = 1024
PROJ_VMEM_BUDGET = 56 * 1024 * 1024
VMEM_LIMIT = 60 * 1024 * 1024

BF16 = jnp.bfloat16
F32 = jnp.float32


def _params(*sem):
    return pltpu.CompilerParams(dimension_semantics=sem, vmem_limit_bytes=VMEM_LIMIT)


def _dot(a, b):
    return jnp.dot(a, b, preferred_element_type=F32)


def _dot_nt(a, b):
    return lax.dot_general(a, b, (((1,), (1,)), ((), ())), preferred_element_type=F32)


def _iota(shape, axis):
    return lax.broadcasted_iota(jnp.int32, shape, axis)


def _rmsnorm_rows(x, g):
    y = x * lax.rsqrt(jnp.mean(x * x, axis=-1, keepdims=True) + NORM_EPS)
    return y * g


def _rmsnorm_kernel(x_ref, g_ref, h_ref):
    h_ref[...] = _rmsnorm_rows(x_ref[...], g_ref[...]).astype(h_ref.dtype)


def rmsnorm(x2, g, out_dtype, tm=512):
    m, d = x2.shape
    return pl.pallas_call(
        _rmsnorm_kernel,
        grid=(m // tm,),
        in_specs=[pl.BlockSpec((tm, d), lambda i: (i, 0)), pl.BlockSpec((1, d), lambda i: (0, 0))],
        out_specs=pl.BlockSpec((tm, d), lambda i: (i, 0)),
        out_shape=jax.ShapeDtypeStruct((m, d), out_dtype),
        compiler_params=_params("parallel"),
        name="rmsnorm",
    )(x2, g.reshape(1, d))


def _rope_lanes(acc, c, s_up, s_dn):
    tn = acc.shape[1]
    reps = tn // HEAD_DIM
    if reps > 1:
        c = jnp.concatenate([c] * reps, axis=1)
        s_up = jnp.concatenate([s_up] * reps, axis=1)
        s_dn = jnp.concatenate([s_dn] * reps, axis=1)
    up = pltpu.roll(acc, tn - ROPE_HALF, axis=1)
    dn = pltpu.roll(acc, ROPE_HALF, axis=1)
    return acc * c + up * s_up + dn * s_dn


def _sigmoid(x):
    return 0.5 * jnp.tanh(0.5 * x) + 0.5


def _proj_kernel(*refs, mode):
    if mode in ("rope", "both"):
        h_ref, w_ref, c_ref, su_ref, sd_ref, cs_ref = refs[:6]
        outs = refs[6:]
    else:
        h_ref, w_ref = refs[:2]
        outs = refs[2:]
    acc = _dot_nt(h_ref[...], w_ref[0].astype(BF16))
    if mode == "plain":
        outs[0][...] = acc.astype(outs[0].dtype)
    elif mode == "sigmoid":
        outs[0][...] = _sigmoid(acc).astype(outs[0].dtype)
    else:
        col_scale = cs_ref[...]
        roped = _rope_lanes(acc, c_ref[...], su_ref[...], sd_ref[...]) * col_scale
        if mode == "both":
            outs[0][...] = (acc * col_scale).astype(outs[0].dtype)
            outs[1][...] = roped.astype(outs[1].dtype)
        else:
            outs[0][...] = roped.astype(outs[0].dtype)


def _proj_tiles(k, n, w_bytes, out_bytes, n_out, rope):
    def fits(tm, tn):
        blocks = (tm * k * 2 + k * tn * w_bytes + n_out * tm * tn * out_bytes
                  + (3 * tm * HEAD_DIM * 4 if rope else 0))
        temporaries = 2 * tm * tn * 4
        return 2 * blocks + temporaries <= PROJ_VMEM_BUDGET

    widths = [math.gcd(w, n) for w in (PROJ_TN_MAX, PROJ_TN_MAX // 2)]
    for tm in (SEQ, SEQ // 2):
        for tn in widths:
            if fits(tm, tn):
                return tm, tn
    return SEQ // 2, widths[-1]


def _weight_rows_spec(tn, k, layer, row0, step_axis):
    assert row0 % F32_SUBLANES == 0
    return pl.BlockSpec(
        (pl.Element(1), pl.Element(tn), pl.Element(k)),
        lambda *idx: (layer, pl.multiple_of(row0 + idx[step_axis] * tn, F32_SUBLANES), 0))


def project(h, wt, layer, col0, n, mode, out_dtype, rope_tabs=None, col_scale=None):
    m, k = h.shape
    n_out = 2 if mode == "both" else 1
    rope = mode in ("rope", "both")
    tm, tn = _proj_tiles(k, n, wt.dtype.itemsize, jnp.dtype(out_dtype).itemsize, n_out, rope)
    assert m % tm == 0 and n % tn == 0 and SEQ % tm == 0
    in_specs = [pl.BlockSpec((tm, k), lambda i, j: (i, 0)), _weight_rows_spec(tn, k, layer, col0, 1)]
    args = [h, wt]
    if rope:
        pos_blocks = SEQ // tm
        for t in rope_tabs:
            in_specs.append(pl.BlockSpec((tm, HEAD_DIM), lambda i, j: (i % pos_blocks, 0)))
            args.append(t)
        in_specs.append(pl.BlockSpec((1, tn), lambda i, j: (0, j)))
        args.append(col_scale)
    out_spec = pl.BlockSpec((tm, tn), lambda i, j: (i, j))
    out_shape = jax.ShapeDtypeStruct((m, n), out_dtype)
    res = pl.pallas_call(
        functools.partial(_proj_kernel, mode=mode),
        grid=(m // tm, n // tn),
        in_specs=in_specs,
        out_specs=[out_spec] * n_out,
        out_shape=[out_shape] * n_out,
        compiler_params=_params("parallel", "parallel"),
        name="proj_" + mode,
    )(*args)
    return res if n_out == 2 else res[0]


def _proj_t_kernel(wt_ref, h_ref, out_ref):
    out_ref[...] = _dot_nt(wt_ref[0].astype(BF16), h_ref[...]).astype(out_ref.dtype)


def project_transposed(h, wt, layer, col0, n, out_dtype, tm=1024):
    m, k = h.shape
    tn = math.gcd(PROJ_TN_MAX, n)
    return pl.pallas_call(
        _proj_t_kernel,
        grid=(n // tn, m // tm),
        in_specs=[_weight_rows_spec(tn, k, layer, col0, 0), pl.BlockSpec((tm, k), lambda j, i: (i, 0))],
        out_specs=pl.BlockSpec((tn, tm), lambda j, i: (j, i)),
        out_shape=jax.ShapeDtypeStruct((n, m), out_dtype),
        compiler_params=_params("parallel", "parallel"),
        name="proj_transposed",
    )(wt, h)


NSA_SIDE_BLOCK = A_KV
NSA_SIDE_NAMES = ("a_kc", "a_vc", "a_ks", "a_vs", "a_kw", "a_vw", "a_g")


def _nsa_side_kernel(h_ref, w_ref, c_ref, su_ref, sd_ref, cmp_ref, ksel_ref, kwin_ref, vts_ref, vtw_ref, ga_ref):
    j = pl.program_id(1)
    token_major = lambda: _dot_nt(h_ref[...], w_ref[0].astype(BF16))
    feature_major = lambda: _dot_nt(w_ref[0].astype(BF16), h_ref[...])
    roped = lambda: _rope_lanes(token_major(), c_ref[...], su_ref[...], sd_ref[...])

    @pl.when(j < 2)
    def _():
        cmp_ref[...] = token_major()

    @pl.when(j == 2)
    def _():
        ksel_ref[...] = roped().astype(ksel_ref.dtype)

    @pl.when(j == 3)
    def _():
        vts_ref[...] = feature_major().astype(vts_ref.dtype)

    @pl.when(j == 4)
    def _():
        kwin_ref[...] = roped().astype(kwin_ref.dtype)

    @pl.when(j == 5)
    def _():
        vtw_ref[...] = feature_major().astype(vtw_ref.dtype)

    @pl.when(j == 6)
    def _():
        ga_ref[...] = _sigmoid(token_major())


def project_nsa_side(h, wt, layer, rope_tabs, tm=SEQ):
    m, k = h.shape
    blk = NSA_SIDE_BLOCK
    row0 = W_IN_START[NSA_SIDE_NAMES[0]]
    assert all(W_IN_START[n] == row0 + i * blk for i, n in enumerate(NSA_SIDE_NAMES))
    rows = lambda cols: pl.BlockSpec((tm, blk), cols)
    fixed = rows(lambda i, j: (i, 0))
    feat = pl.BlockSpec((blk, tm), lambda i, j: (0, i))
    tab = pl.BlockSpec((tm, HEAD_DIM), lambda i, j: (0, 0))
    return pl.pallas_call(
        _nsa_side_kernel,
        grid=(m // tm, len(NSA_SIDE_NAMES)),
        in_specs=[pl.BlockSpec((tm, k), lambda i, j: (i, 0)), _weight_rows_spec(blk, k, layer, row0, 1), tab, tab, tab],
        out_specs=[rows(lambda i, j: (i, jnp.minimum(j, 1))), fixed, fixed, feat, feat, fixed],
        out_shape=[jax.ShapeDtypeStruct((m, 2 * blk), F32), jax.ShapeDtypeStruct((m, blk), BF16),
                   jax.ShapeDtypeStruct((m, blk), BF16), jax.ShapeDtypeStruct((blk, m), BF16),
                   jax.ShapeDtypeStruct((blk, m), BF16), jax.ShapeDtypeStruct((m, blk), F32)],
        compiler_params=_params("parallel", "arbitrary"),
        name="proj_nsa_side",
    )(h, wt, *rope_tabs)


def _compress_kernel(k0_ref, k1_ref, v0_ref, v1_ref, pek_ref, w1k_ref, w2k_ref, pev_ref, w1v_ref, w2v_ref,
                     out_ref):
    half = NSA_CMP_STRIDE * HEAD_DIM
    for idx, src_ref in enumerate((k0_ref, k1_ref, v0_ref, v1_ref)):
        is_k = idx < NSA_KV_HEADS
        pe_ref, w1_ref, w2_ref = (pek_ref, w1k_ref, w2k_ref) if is_k else (pev_ref, w1v_ref, w2v_ref)
        x = jnp.concatenate(
            [src_ref[pl.ds(l, NSA_M_PAD, stride=NSA_CMP_STRIDE), :]
             for l in range(NSA_CMP_STRIDE)], axis=1)
        pe = pe_ref[...]
        first = _dot((x + pe[:, :half]).astype(BF16), w1_ref[:half, :])
        second = _dot((x + pe[:, half:]).astype(BF16), w1_ref[half:, :])
        hid = jax.nn.gelu(first + pltpu.roll(second, NSA_M_PAD - 1, axis=0))
        out = _dot(hid.astype(BF16), w2_ref[...])
        out_ref[idx] = (out if is_k else jnp.transpose(out)).astype(out_ref.dtype)


def nsa_compress(pf, pe_k, w1_k, w2_k, pe_v, w1_v, w2_v, batch):
    flat = NSA_CMP_LEN * HEAD_DIM
    const = lambda shape: pl.BlockSpec(shape, lambda b: (0,) * len(shape))
    return pl.pallas_call(
        _compress_kernel,
        grid=(batch,),
        in_specs=[pl.BlockSpec((SEQ, HEAD_DIM), lambda b, c=c: (b, c)) for c in range(4)] + [
                  const((1, flat)), const((flat, NSA_CMP_HIDDEN)), const((NSA_CMP_HIDDEN, HEAD_DIM)),
                  const((1, flat)), const((flat, NSA_CMP_HIDDEN)), const((NSA_CMP_HIDDEN, HEAD_DIM))],
        out_specs=pl.BlockSpec((None, 4, NSA_M_PAD, HEAD_DIM), lambda b: (b, 0, 0, 0)),
        out_shape=jax.ShapeDtypeStruct((batch, 4, NSA_M_PAD, HEAD_DIM), BF16),
        compiler_params=_params("parallel"),
        name="nsa_compress",
    )(pf, pf, pf, pf, pe_k, w1_k, w2_k, pe_v, w1_v, w2_v)


NSA_TQ = 256
NSA_KC = 256
NSA_BLK_PER_CHUNK = NSA_KC // NSA_SEL_BLOCK


def _nsa_kernel(qraw_ref, qrot_ref, cmp_ref, ksel_ref, vselt_ref, kwin_ref, vwint_ref,
                gate_ref, ovt_ref, o_ref, bias_scr):
    i = pl.program_id(1)
    tq, kc_w, rep, d = NSA_TQ, NSA_KC, NSA_REP, HEAD_DIM
    groups = range(NSA_KV_HEADS)
    width = rep * tq
    bpc = NSA_BLK_PER_CHUNK
    t0 = i * tq
    head_cols = lambda g, r: slice((g * rep + r) * d, (g * rep + r + 1) * d)
    stack = lambda ref, g: jnp.concatenate([ref[:, head_cols(g, r)] for r in range(rep)], axis=0)
    q_raw = [stack(qraw_ref, g) for g in groups]
    q_rot = [stack(qrot_ref, g) for g in groups]
    k_of = lambda ref, g, kj: ref[pl.ds(pl.multiple_of(kj * kc_w, kc_w), kc_w), g * d:(g + 1) * d]
    vt_of = lambda ref, g, kj: ref[g * d:(g + 1) * d, pl.ds(pl.multiple_of(kj * kc_w, kc_w), kc_w)]
    col_max = lambda s: jnp.max(s, axis=0, keepdims=True)
    col_sum = lambda p: jnp.sum(p, axis=0, keepdims=True)

    q_pos = t0 + (_iota((NSA_M_PAD, width), 1) & (tq - 1))
    vis = (_iota((NSA_M_PAD, width), 0) * NSA_CMP_STRIDE + (NSA_CMP_LEN - 1)) <= q_pos
    sc = [jnp.where(vis, _dot_nt(cmp_ref[g], q_raw[g]), NEG_INF) for g in groups]
    ec = [jnp.where(vis, jnp.exp2(s - col_max(s)), 0.0) for s in sc]
    pc = [(e / jnp.maximum(col_sum(e), 1e-30)).astype(BF16) for e in ec]
    o_cmp = [_dot(cmp_ref[NSA_KV_HEADS + g], pc[g]) for g in groups]
    imp_heads = [_dot(ovt_ref[...], p) for p in pc]

    blk = _iota((NSA_NB, tq), 0)
    q_blk = (t0 + _iota((NSA_NB, tq), 1)) // NSA_SEL_BLOCK
    forced = (blk == 0) | (blk == q_blk) | (blk == q_blk - 1)
    for g in groups:
        imp = imp_heads[g][:, :tq]
        for r in range(1, rep):
            imp = imp + imp_heads[g][:, r * tq:(r + 1) * tq]
        score = jnp.where(blk <= q_blk, imp + jnp.where(forced, NSA_FORCE_BONUS, 0.0), NEG_INF)
        rank = jnp.zeros((NSA_NB, tq), F32)
        for j in range(NSA_NB):
            sj = score[j:j + 1, :]
            ahead = (sj > score) | ((sj == score) & (blk > j))
            rank = rank + jnp.where(ahead, 1.0, 0.0)
        bias = jnp.where((rank < float(NSA_SEL_TOPN)) & (score > NEG_INF * 0.5), 0.0, NEG_INF)
        bias = jnp.concatenate([bias] * rep, axis=1)
        for c in range(SEQ // kc_w):
            bias_scr[g, c] = bias[c * bpc:(c + 1) * bpc, :]

    def sel_step(kj, ss, carry):
        bs = [bias_scr[g, kj] for g in groups]
        s3 = [s.reshape(bpc, NSA_SEL_BLOCK, width) for s in ss]
        m_new = [jnp.maximum(carry[g][0], col_max(jnp.max(s3[g], axis=1) + bs[g])) for g in groups]
        alpha = [jnp.exp2(carry[g][0] - m_new[g]) for g in groups]
        ps = [jnp.exp2(s3[g] - (m_new[g] - bs[g])[:, None, :]).reshape(kc_w, width) for g in groups]
        ls = [alpha[g] * carry[g][1] + col_sum(ps[g]) for g in groups]
        pvs = [_dot(vt_of(vselt_ref, g, kj), ps[g].astype(BF16)) for g in groups]
        return tuple((m_new[g], ls[g], alpha[g] * carry[g][2] + pvs[g]) for g in groups)

    init = tuple((jnp.full((1, width), NEG_INF, F32), jnp.zeros((1, width), F32), jnp.zeros((d, width), F32))
                 for g in groups)
    carry = lax.fori_loop(
        0, i, lambda kj, c: sel_step(kj, [_dot_nt(k_of(ksel_ref, g, kj), q_rot[g]) for g in groups], c), init)

    key_row = _iota((kc_w, width), 0)
    q_col = _iota((kc_w, width), 1) & (tq - 1)
    causal = key_row <= q_col
    diag = [jnp.where(causal, _dot_nt(k_of(ksel_ref, g, i), q_rot[g]), NEG_INF) for g in groups]
    sel = sel_step(i, diag, carry)

    far = i - 2
    near = i - 1
    far_ok = jnp.where(far >= 0, 0.0, NEG_INF)
    near_ok = jnp.where(near >= 0, 0.0, NEG_INF)
    far_c = jnp.maximum(far, 0)
    near_c = jnp.maximum(near, 0)
    s_own = [jnp.where(causal, _dot_nt(k_of(kwin_ref, g, i), q_rot[g]), NEG_INF) for g in groups]
    s_near = [_dot_nt(k_of(kwin_ref, g, near_c), q_rot[g]) + near_ok for g in groups]
    s_far = [jnp.where(key_row > q_col, _dot_nt(k_of(kwin_ref, g, far_c), q_rot[g]), NEG_INF) + far_ok
             for g in groups]
    m_w = [jnp.maximum(jnp.maximum(col_max(s_own[g]), col_max(s_near[g])), col_max(s_far[g])) for g in groups]
    p_own = [jnp.exp2(s_own[g] - m_w[g]) for g in groups]
    p_near = [jnp.exp2(s_near[g] - m_w[g]) for g in groups]
    p_far = [jnp.exp2(s_far[g] - m_w[g]) for g in groups]
    l_win = [col_sum(p_own[g]) + col_sum(p_near[g]) + col_sum(p_far[g]) for g in groups]
    acc_win = [_dot(vt_of(vwint_ref, g, i), p_own[g].astype(BF16))
               + _dot(vt_of(vwint_ref, g, near_c), p_near[g].astype(BF16))
               + _dot(vt_of(vwint_ref, g, far_c), p_far[g].astype(BF16)) for g in groups]

    gates = jnp.transpose(gate_ref[...])
    for g in groups:
        o_sel = sel[g][2] / sel[g][1]
        o_win = acc_win[g] / l_win[g]
        for r in range(rep):
            lanes = slice(r * tq, (r + 1) * tq)
            row = 3 * (g * rep + r)
            o = (gates[row:row + 1, :] * o_cmp[g][:, lanes] + gates[row + 1:row + 2, :] * o_sel[:, lanes]
                 + gates[row + 2:row + 3, :] * o_win[:, lanes])
            o_ref[:, head_cols(g, r)] = jnp.transpose(o).astype(o_ref.dtype)


def nsa_attention(q_raw, q_rot, cmp_kv, k_sel, k_win, vt_sel, vt_win, gates_a, overlap_t, batch):
    tq = NSA_TQ
    nq = SEQ // tq
    g_n = NSA_KV_HEADS
    row = lambda b, i: (b * nq + i, 0)
    return pl.pallas_call(
        _nsa_kernel,
        grid=(batch, nq),
        in_specs=[pl.BlockSpec((tq, A_Q), row),
                  pl.BlockSpec((tq, A_Q), row),
                  pl.BlockSpec((None, 2 * g_n, NSA_M_PAD, HEAD_DIM), lambda b, i: (b, 0, 0, 0)),
                  pl.BlockSpec((SEQ, A_KV), lambda b, i: (b, 0)),
                  pl.BlockSpec((A_KV, SEQ), lambda b, i: (0, b)),
                  pl.BlockSpec((SEQ, A_KV), lambda b, i: (b, 0)),
                  pl.BlockSpec((A_KV, SEQ), lambda b, i: (0, b)),
                  pl.BlockSpec((tq, LANES), row),
                  pl.BlockSpec((NSA_NB, NSA_M_PAD), lambda b, i: (0, 0))],
        out_specs=pl.BlockSpec((tq, A_Q), row),
        out_shape=jax.ShapeDtypeStruct((batch * SEQ, A_Q), BF16),
        scratch_shapes=[pltpu.VMEM((g_n, SEQ // NSA_KC, NSA_BLK_PER_CHUNK, NSA_REP * tq), F32)],
        compiler_params=_params("parallel", "parallel"),
        name="nsa_attention",
    )(q_raw, q_rot, cmp_kv, k_sel, vt_sel, k_win, vt_win, gates_a, overlap_t)


def _dilated_kernel(q0, k0, v0, q1, k1, v1, q2, k2, v2, o_ref,
                    o0_scr, o1_scr, o2_scr, l0_scr, l1_scr, l2_scr):
    tile = DIL_TILE
    groups = ((q0, k0, v0, o0_scr, l0_scr), (q1, k1, v1, o1_scr, l1_scr), (q2, k2, v2, o2_scr, l2_scr))
    for (window, dil), (q_ref, k_ref, v_ref, og_scr, lg_scr) in zip(DIL_GROUPS, groups):
        assert window // dil == tile
        per_class = SEQ // dil
        tiles_per_class = per_class // tile
        nk = tile if tiles_per_class == 1 else 2 * tile
        a_minus_a = _iota((tile, nk), 0) - _iota((tile, nk), 1)

        def step(u, carry, q_ref=q_ref, k_ref=k_ref, v_ref=v_ref, og_scr=og_scr, lg_scr=lg_scr,
                 dil=dil, tiles_per_class=tiles_per_class, nk=nk, a_minus_a=a_minus_a):
            ts = [u * DIL_UNROLL + a for a in range(DIL_UNROLL)]
            cls = [t // tiles_per_class for t in ts]
            n0 = [(t % tiles_per_class) * tile for t in ts]
            kbase = [jnp.maximum(n - (nk - tile), 0) for n in n0]
            q_rows = [pl.ds(c + dil * n, tile, stride=dil) for c, n in zip(cls, n0)]
            k_rows = [pl.ds(c + dil * kb, nk, stride=dil) for c, kb in zip(cls, kbase)]
            qs = [q_ref[r, :].astype(BF16) for r in q_rows]
            ks = [k_ref[r, :].astype(BF16) for r in k_rows]
            vs = [v_ref[r, :].astype(BF16) for r in k_rows]
            ss = [_dot_nt(q, k) for q, k in zip(qs, ks)]
            masks = []
            for n, kb in zip(n0, kbase):
                dist = a_minus_a + (n - kb)
                masks.append((dist >= 0) & (dist <= tile))
            ss = [jnp.where(mk, s, NEG_INF) for mk, s in zip(masks, ss)]
            ms = [jnp.max(s, axis=-1, keepdims=True) for s in ss]
            es = [jnp.where(mk, jnp.exp2(s - m), 0.0) for mk, s, m in zip(masks, ss, ms)]
            dens = [jnp.maximum(jnp.sum(e, axis=-1, keepdims=True), 1e-30) for e in es]
            os_ = [_dot((e / den).astype(BF16), v) for e, den, v in zip(es, dens, vs)]
            for r, o, m, den in zip(q_rows, os_, ms, dens):
                og_scr[r, :] = o
                lg_scr[r, :] = jnp.broadcast_to(m + jnp.log2(den), (tile, HEAD_DIM))
            return carry

        lax.fori_loop(0, SEQ // tile // DIL_UNROLL, step, 0)

    rows = 256

    def merge_body(c, carry):
        sl = pl.ds(pl.multiple_of(c * rows, rows), rows)
        la, lb, lc = l0_scr[sl, :], l1_scr[sl, :], l2_scr[sl, :]
        mx = jnp.maximum(jnp.maximum(la, lb), lc)
        ea, eb, ec = jnp.exp2(la - mx), jnp.exp2(lb - mx), jnp.exp2(lc - mx)
        tot = ea + eb + ec
        out = (ea / tot) * o0_scr[sl, :] + (eb / tot) * o1_scr[sl, :] + (ec / tot) * o2_scr[sl, :]
        o_ref[sl, :] = out.astype(o_ref.dtype)
        return carry

    lax.fori_loop(0, SEQ // rows, merge_body, 0)


def dilated_attention(b_qk, b_v, batch):
    hp = DIL_HEADS_PER_GROUP
    in_specs, args = [], []
    for g in range(len(DIL_GROUPS)):
        for arr, c0 in ((b_qk, 0), (b_qk, DIL_HEADS), (b_v, 0)):
            in_specs.append(pl.BlockSpec((SEQ, HEAD_DIM), lambda b, j, c0=c0, g=g: (b, c0 + g * hp + j)))
            args.append(arr)
    return pl.pallas_call(
        _dilated_kernel,
        grid=(batch, hp),
        in_specs=in_specs,
        out_specs=pl.BlockSpec((SEQ, HEAD_DIM), lambda b, j: (b, j)),
        out_shape=jax.ShapeDtypeStruct((batch * SEQ, hp * HEAD_DIM), BF16),
        scratch_shapes=[pltpu.VMEM((SEQ, HEAD_DIM), F32)] * 6,
        compiler_params=_params("parallel", "parallel"),
        name="dilated_attention",
    )(*args)


def _moba_kernel(q_ref, k_ref, vt_ref, o_ref, kmean_scr, bias_scr):
    i = pl.program_id(1)
    tq, nb, d = MOBA_BLOCK, MOBA_NB, HEAD_DIM
    heads = range(MOBA_HEADS)
    col = lambda h: slice(h * d, (h + 1) * d)

    @pl.when(i == 0)
    def _():
        avg = jnp.where(_iota((nb, SEQ), 1) // MOBA_BLOCK == _iota((nb, SEQ), 0), 1.0 / MOBA_BLOCK, 0.0)
        kmean_scr[...] = _dot(avg.astype(BF16), k_ref[...]).astype(kmean_scr.dtype)

    blk = _iota((nb, tq), 0)
    own = pl.multiple_of(i * tq, tq)
    causal = _iota((tq, tq), 0) <= _iota((tq, tq), 1)
    qs = [q_ref[:, col(h)] for h in heads]
    gates = [jnp.where(blk < i, _dot_nt(kmean_scr[:, col(h)], qs[h]), NEG_INF) for h in heads]
    ss = [_dot_nt(k_ref[pl.ds(own, tq), col(h)], qs[h]) for h in heads]
    for h in heads:
        gate = gates[h]
        rank = jnp.zeros((nb, tq), F32)
        for j in range(nb):
            gj = gate[j:j + 1, :]
            ahead = (gj > gate) | ((gj == gate) & (blk > j))
            rank = rank + jnp.where(ahead, 1.0, 0.0)
        bias_scr[h] = jnp.where((rank < float(MOBA_TOPK)) & (gate > NEG_INF * 0.5), 0.0, NEG_INF)
    ss = [jnp.where(causal, s, NEG_INF) for s in ss]
    ms = [jnp.max(s, axis=0, keepdims=True) for s in ss]
    ps = [jnp.exp2(s - m) for s, m in zip(ss, ms)]
    ls = [jnp.sum(p, axis=0, keepdims=True) for p in ps]
    accs = [_dot(vt_ref[col(h), pl.ds(own, tq)], ps[h].astype(BF16)) for h in heads]

    def body(kj, carry):
        off = pl.multiple_of(kj * tq, tq)
        bs = [bias_scr[h, pl.ds(kj, 1), :] for h in heads]
        ss = [_dot_nt(k_ref[pl.ds(off, tq), col(h)], qs[h]) for h in heads]
        m_new = [jnp.maximum(carry[h][0], jnp.max(ss[h], axis=0, keepdims=True) + bs[h]) for h in heads]
        alpha = [jnp.exp2(carry[h][0] - m_new[h]) for h in heads]
        ps = [jnp.exp2(ss[h] - (m_new[h] - bs[h])) for h in heads]
        ls = [alpha[h] * carry[h][1] + jnp.sum(ps[h], axis=0, keepdims=True) for h in heads]
        pvs = [_dot(vt_ref[col(h), pl.ds(off, tq)], ps[h].astype(BF16)) for h in heads]
        return tuple((m_new[h], ls[h], alpha[h] * carry[h][2] + pvs[h]) for h in heads)

    fin = lax.fori_loop(0, i, body, tuple((ms[h], ls[h], accs[h]) for h in heads))
    for h in heads:
        _, l, acc = fin[h]
        o_ref[:, col(h)] = jnp.transpose(acc / l).astype(o_ref.dtype)


def moba_attention(c_qk, vt, batch):
    tq = MOBA_BLOCK
    nq = SEQ // tq
    return pl.pallas_call(
        _moba_kernel,
        grid=(batch, nq),
        in_specs=[pl.BlockSpec((tq, C_QKV), lambda b, i: (b * nq + i, 0)),
                  pl.BlockSpec((SEQ, C_QKV), lambda b, i: (b, 1)),
                  pl.BlockSpec((C_QKV, SEQ), lambda b, i: (0, b))],
        out_specs=pl.BlockSpec((tq, C_QKV), lambda b, i: (b * nq + i, 0)),
        out_shape=jax.ShapeDtypeStruct((batch * SEQ, C_QKV), BF16),
        scratch_shapes=[pltpu.VMEM((MOBA_NB, C_QKV), BF16), pltpu.VMEM((MOBA_HEADS, MOBA_NB, tq), F32)],
        compiler_params=_params("parallel", "arbitrary"),
        name="moba_attention",
    )(c_qk, c_qk, vt)


def _merge_kernel(oa_ref, ob_ref, oc_ref, ga_ref, gb_ref, gc_ref, wa_ref, wb_ref, wc_ref, out_ref):
    y = ga_ref[...] * _dot(oa_ref[...], wa_ref[...].astype(BF16))
    y = y + gb_ref[...] * _dot(ob_ref[...], wb_ref[...].astype(BF16))
    y = y + gc_ref[...] * _dot(oc_ref[...], wc_ref[...].astype(BF16))
    out_ref[...] = y.astype(out_ref.dtype)


def gated_merge(o_a, o_b, o_c, gates_m, w_a, w_b, w_c, layer, tm=1024, tn=512):
    m = o_a.shape[0]
    nb = D_MODEL // tn
    act = lambda width: pl.BlockSpec((tm, width), lambda j, i: (i, 0))
    gate = lambda g: pl.BlockSpec((tm, tn), lambda j, i, g=g: (i, g * nb + j))
    wgt = lambda width: pl.BlockSpec((None, width, tn), lambda j, i: (layer, 0, j))
    return pl.pallas_call(
        _merge_kernel,
        grid=(nb, m // tm),
        in_specs=[act(o_a.shape[1]), act(o_b.shape[1]), act(o_c.shape[1]),
                  gate(0), gate(1), gate(2),
                  wgt(w_a.shape[1]), wgt(w_b.shape[1]), wgt(w_c.shape[1])],
        out_specs=pl.BlockSpec((tm, tn), lambda j, i: (i, j)),
        out_shape=jax.ShapeDtypeStruct((m, D_MODEL), BF16),
        compiler_params=_params("parallel", "parallel"),
        name="gated_merge",
    )(o_a, o_b, o_c, gates_m, gates_m, gates_m, w_a, w_b, w_c)


def _out_proj_kernel(y_ref, w_ref, x_ref, g_ref, xo_ref, h_ref):
    x_new = x_ref[...] + _dot(y_ref[...], w_ref[...])
    xo_ref[...] = x_new
    h_ref[...] = _rmsnorm_rows(x_new, g_ref[...]).astype(h_ref.dtype)


def out_proj_residual_norm(y, w_o, layer, x2, g, tm=512):
    m, d = x2.shape
    row = pl.BlockSpec((tm, d), lambda i: (i, 0))
    return pl.pallas_call(
        _out_proj_kernel,
        grid=(m // tm,),
        in_specs=[row, pl.BlockSpec((None, d, d), lambda i: (layer, 0, 0)), row,
                  pl.BlockSpec((1, d), lambda i: (0, 0))],
        out_specs=[row, row],
        out_shape=[jax.ShapeDtypeStruct((m, d), F32), jax.ShapeDtypeStruct((m, d), BF16)],
        compiler_params=_params("parallel"),
        name="out_proj",
    )(y, w_o, x2, g.reshape(1, d))


def _mlp_kernel(h_ref, w1_ref, w2_ref, x_ref, g_ref, *outs):
    acc_ref, hn_ref = outs[0], outs[-1]
    f = pl.program_id(1)

    @pl.when(f == 0)
    def _():
        acc_ref[...] = jnp.zeros_like(acc_ref)

    u = jnp.square(jnp.maximum(_dot(h_ref[...], w1_ref[...].astype(BF16)), 0.0))
    acc_ref[...] += _dot(u.astype(BF16), w2_ref[...].astype(BF16))

    @pl.when(f == pl.num_programs(1) - 1)
    def _():
        x_new = x_ref[...] + acc_ref[...]
        if len(outs) == 2:
            acc_ref[...] = x_new
        hn_ref[...] = _rmsnorm_rows(x_new, g_ref[...]).astype(hn_ref.dtype)


def mlp_residual_norm(h2, w1, w2, layer, x2, g_next, next_dtype, emit_x, tm=1024, tf=512):
    m, d = x2.shape
    row = pl.BlockSpec((tm, d), lambda i, f: (i, 0))
    row_once = pl.BlockSpec((tm, d), lambda i, f: (i, 0), pipeline_mode=pl.Buffered(1))
    out_shape = [jax.ShapeDtypeStruct((m, d), F32)] * emit_x + [jax.ShapeDtypeStruct((m, d), next_dtype)]
    res = pl.pallas_call(
        _mlp_kernel,
        grid=(m // tm, D_FF // tf),
        in_specs=[row, pl.BlockSpec((None, d, tf), lambda i, f: (layer, 0, f)),
                  pl.BlockSpec((None, tf, d), lambda i, f: (layer, f, 0)),
                  row_once, pl.BlockSpec((1, d), lambda i, f: (0, 0))],
        out_specs=[row_once] * len(out_shape),
        out_shape=out_shape,
        compiler_params=_params("parallel", "arbitrary"),
        name="mlp",
    )(h2, w1, w2, x2, g_next.reshape(1, d))
    return (res[0], res[1]) if emit_x else (None, res[0])


def _rope_tables():
    inv = ROPE_THETA ** (-jnp.arange(0, ROPE_DIM, 2, dtype=F32) / ROPE_DIM)
    ang = jnp.arange(SEQ, dtype=F32)[:, None] * inv[None, :]
    cos, sin = jnp.cos(ang), jnp.sin(ang)
    zeros = jnp.zeros((SEQ, HEAD_DIM - ROPE_DIM), F32)
    zero_h = jnp.zeros((SEQ, ROPE_HALF), F32)
    c = jnp.concatenate([cos, cos, jnp.ones_like(zeros)], axis=1)
    s_up = jnp.concatenate([-sin, zero_h, zeros], axis=1)
    s_dn = jnp.concatenate([zero_h, sin, zeros], axis=1)
    return c, s_up, s_dn


def _overlap_table_t():
    cs = np.arange(NSA_M_PAD)[None, :] * NSA_CMP_STRIDE
    bs = np.arange(NSA_NB)[:, None] * NSA_SEL_BLOCK
    ov = np.clip(np.minimum(cs + NSA_CMP_LEN, bs + NSA_SEL_BLOCK) - np.maximum(cs, bs), 0, None) / NSA_CMP_LEN
    ov[:, NSA_M_PAD - 1] = 0.0
    return jnp.asarray(ov, dtype=BF16)


W_IN_NAMES = ("a_q", "a_kc", "a_vc", "a_ks", "a_vs", "a_kw", "a_vw", "a_g",
              "b_q", "b_k", "b_v", "c_q", "c_k", "c_v", "m_a", "m_b", "m_c")
W_IN_START = dict(zip(W_IN_NAMES, np.cumsum((0,) + IN_SPLIT_SIZES[:-1]).tolist()))
W_IN_SIZE = dict(zip(W_IN_NAMES, IN_SPLIT_SIZES))


def _scales(*widths_and_values):
    return jnp.concatenate([jnp.full((1, w), v, F32) for w, v in widths_and_values], axis=1)


def _layer(x2, h, batch, layer, tabs, overlap_t, w_in_t, pe_k, w1_k, w2_k, pe_v, w1_v, w2_v,
           w_br_a, w_br_b, w_br_c, w_o, mlp_g, w_mlp_in, w_mlp_out, g_next, last):
    def proj(first, n, *args):
        return project(h, w_in_t, layer, W_IN_START[first], n, *args)

    def proj_t(name):
        return project_transposed(h, w_in_t, layer, W_IN_START[name], W_IN_SIZE[name], BF16)

    q_raw, q_rot = proj("a_q", A_Q, "both", BF16, tabs, _scales((A_Q, Q_SCALE)))
    cmp_src, k_sel, k_win, vt_sel, vt_win, gates_a = project_nsa_side(h, w_in_t, layer, tabs)
    b_qk = proj("b_q", 2 * B_QKV, "rope", F32, tabs, _scales((B_QKV, Q_SCALE), (B_QKV, 1.0)))
    b_v = proj("b_v", B_QKV, "plain", F32)
    c_qk = proj("c_q", 2 * C_QKV, "rope", BF16, tabs, _scales((C_QKV, Q_SCALE), (C_QKV, 1.0)))
    gates_m = proj("m_a", 3 * D_MODEL, "sigmoid", BF16)
    vt_moba = proj_t("c_v")

    flat = NSA_CMP_LEN * HEAD_DIM
    cmp_kv = nsa_compress(cmp_src, pe_k.reshape(1, flat), w1_k.reshape(flat, NSA_CMP_HIDDEN).astype(BF16),
                          w2_k.astype(BF16), pe_v.reshape(1, flat),
                          w1_v.reshape(flat, NSA_CMP_HIDDEN).astype(BF16), w2_v.astype(BF16), batch)
    o_a = nsa_attention(q_raw, q_rot, cmp_kv, k_sel, k_win, vt_sel, vt_win, gates_a, overlap_t, batch)
    o_b = dilated_attention(b_qk, b_v, batch)
    o_c = moba_attention(c_qk, vt_moba, batch)

    merged = gated_merge(o_a, o_b, o_c, gates_m, w_br_a, w_br_b, w_br_c, layer)
    x2, h2 = out_proj_residual_norm(merged, w_o, layer, x2, mlp_g)
    return mlp_residual_norm(h2, w_mlp_in, w_mlp_out, layer, x2, g_next, F32 if last else BF16, emit_x=not last)


def kernel(x, attn_norm_g, w_in, cmp_pe_k, cmp_w1_k, cmp_w2_k, cmp_pe_v, cmp_w1_v, cmp_w2_v,
           w_br_a, w_br_b, w_br_c, w_o, mlp_norm_g, w_mlp_in, w_mlp_out, final_norm_g):
    batch, seq, d = x.shape
    assert seq == SEQ and d == D_MODEL
    depth = w_in.shape[0]
    tabs = _rope_tables()
    overlap_t = _overlap_table_t()
    w_in_t = jnp.transpose(w_in, (0, 2, 1))
    w_o = w_o.astype(BF16)
    x2 = x.reshape(batch * seq, d)
    h = rmsnorm(x2, attn_norm_g[0], BF16)
    for l in range(depth):
        last = l == depth - 1
        g_next = final_norm_g if last else attn_norm_g[l + 1]
        x2, h = _layer(x2, h, batch, l, tabs, overlap_t, w_in_t,
                       cmp_pe_k[l], cmp_w1_k[l], cmp_w2_k[l], cmp_pe_v[l], cmp_w1_v[l], cmp_w2_v[l],
                       w_br_a, w_br_b, w_br_c, w_o, mlp_norm_g[l], w_mlp_in, w_mlp_out, g_next, last)
    return h.reshape(batch, seq, d)
```

```python
import functools
import math

import numpy as np
import jax
import jax.numpy as jnp
from jax import lax
from jax.experimental import pallas as pl
from jax.experimental.pallas import tpu as pltpu

D_MODEL = 2048
SEQ = 2048
HEAD_DIM = 128
ROPE_THETA = 500000.0
ROPE_DIM = HEAD_DIM // 4
ROPE_HALF = ROPE_DIM // 2
NORM_EPS = 1e-6
NEG_INF = -1e30
Q_SCALE = HEAD_DIM ** -0.5 * math.log2(math.e)

NSA_HEADS = 8
NSA_KV_HEADS = 2
NSA_REP = NSA_HEADS // NSA_KV_HEADS
NSA_CMP_LEN = 32
NSA_CMP_STRIDE = 16
NSA_CMP_HIDDEN = 256
NSA_SEL_BLOCK = 64
NSA_SEL_TOPN = 16
NSA_WINDOW = 512
NSA_FORCE_BONUS = 1e4
NSA_NB = SEQ // NSA_SEL_BLOCK
NSA_M_PAD = SEQ // NSA_CMP_STRIDE

DIL_GROUPS = ((128, 1), (512, 4), (2048, 16))
DIL_HEADS_PER_GROUP = 4
DIL_HEADS = DIL_HEADS_PER_GROUP * len(DIL_GROUPS)
DIL_TILE = 128
DIL_UNROLL = 16

MOBA_HEADS = 8
MOBA_BLOCK = 256
MOBA_TOPK = 3
MOBA_NB = SEQ // MOBA_BLOCK

D_FF = 4 * D_MODEL
A_Q = NSA_HEADS * HEAD_DIM
A_KV = NSA_KV_HEADS * HEAD_DIM
A_G = 3 * NSA_HEADS
B_QKV = DIL_HEADS * HEAD_DIM
C_QKV = MOBA_HEADS * HEAD_DIM
IN_SPLIT_SIZES = (A_Q, A_KV, A_KV, A_KV, A_KV, A_KV, A_KV, A_G,
                  B_QKV, B_QKV, B_QKV, C_QKV, C_QKV, C_QKV,
                  D_MODEL, D_MODEL, D_MODEL)

LANES = 128
F32_SUBLANES = 8
PROJ_TN_MAX = 1024
PROJ_VMEM_BUDGET = 56 * 1024 * 1024
VMEM_LIMIT = 60 * 1024 * 1024

BF16 = jnp.bfloat16
F32 = jnp.float32


def _params(*sem):
    return pltpu.CompilerParams(dimension_semantics=sem, vmem_limit_bytes=VMEM_LIMIT)


def _dot(a, b):
    return jnp.dot(a, b, preferred_element_type=F32)


def _dot_nt(a, b):
    return lax.dot_general(a, b, (((1,), (1,)), ((), ())), preferred_element_type=F32)


def _iota(shape, axis):
    return lax.broadcasted_iota(jnp.int32, shape, axis)


def _rmsnorm_rows(x, g):
    y = x * lax.rsqrt(jnp.mean(x * x, axis=-1, keepdims=True) + NORM_EPS)
    return y * g


def _rmsnorm_kernel(x_ref, g_ref, h_ref):
    h_ref[...] = _rmsnorm_rows(x_ref[...], g_ref[...]).astype(h_ref.dtype)


def rmsnorm(x2, g, out_dtype, tm=512):
    m, d = x2.shape
    return pl.pallas_call(
        _rmsnorm_kernel,
        grid=(m // tm,),
        in_specs=[pl.BlockSpec((tm, d), lambda i: (i, 0)), pl.BlockSpec((1, d), lambda i: (0, 0))],
        out_specs=pl.BlockSpec((tm, d), lambda i: (i, 0)),
        out_shape=jax.ShapeDtypeStruct((m, d), out_dtype),
        compiler_params=_params("parallel"),
        name="rmsnorm",
    )(x2, g.reshape(1, d))


def _rope_lanes(acc, c, s_up, s_dn):
    tn = acc.shape[1]
    reps = tn // HEAD_DIM
    if reps > 1:
        c = jnp.concatenate([c] * reps, axis=1)
        s_up = jnp.concatenate([s_up] * reps, axis=1)
        s_dn = jnp.concatenate([s_dn] * reps, axis=1)
    up = pltpu.roll(acc, tn - ROPE_HALF, axis=1)
    dn = pltpu.roll(acc, ROPE_HALF, axis=1)
    return acc * c + up * s_up + dn * s_dn


def _sigmoid(x):
    return 0.5 * jnp.tanh(0.5 * x) + 0.5


def _proj_kernel(*refs, mode):
    if mode in ("rope", "both"):
        h_ref, w_ref, c_ref, su_ref, sd_ref, cs_ref = refs[:6]
        outs = refs[6:]
    else:
        h_ref, w_ref = refs[:2]
        outs = refs[2:]
    acc = _dot_nt(h_ref[...], w_ref[0].astype(BF16))
    if mode == "plain":
        outs[0][...] = acc.astype(outs[0].dtype)
    elif mode == "sigmoid":
        outs[0][...] = _sigmoid(acc).astype(outs[0].dtype)
    else:
        col_scale = cs_ref[...]
        roped = _rope_lanes(acc, c_ref[...], su_ref[...], sd_ref[...]) * col_scale
        if mode == "both":
            outs[0][...] = (acc * col_scale).astype(outs[0].dtype)
            outs[1][...] = roped.astype(outs[1].dtype)
        else:
            outs[0][...] = roped.astype(outs[0].dtype)


def _proj_tiles(k, n, w_bytes, out_bytes, n_out, rope):
    def fits(tm, tn):
        blocks = (tm * k * 2 + k * tn * w_bytes + n_out * tm * tn * out_bytes
                  + (3 * tm * HEAD_DIM * 4 if rope else 0))
        temporaries = 2 * tm * tn * 4
        return 2 * blocks + temporaries <= PROJ_VMEM_BUDGET

    widths = [math.gcd(w, n) for w in (PROJ_TN_MAX, PROJ_TN_MAX // 2)]
    for tm in (SEQ, SEQ // 2):
        for tn in widths:
            if fits(tm, tn):
                return tm, tn
    return SEQ // 2, widths[-1]


def _weight_rows_spec(tn, k, layer, row0, step_axis):
    assert row0 % F32_SUBLANES == 0
    return pl.BlockSpec(
        (pl.Element(1), pl.Element(tn), pl.Element(k)),
        lambda *idx: (layer, pl.multiple_of(row0 + idx[step_axis] * tn, F32_SUBLANES), 0))


def project(h, wt, layer, col0, n, mode, out_dtype, rope_tabs=None, col_scale=None):
    m, k = h.shape
    n_out = 2 if mode == "both" else 1
    rope = mode in ("rope", "both")
    tm, tn = _proj_tiles(k, n, wt.dtype.itemsize, jnp.dtype(out_dtype).itemsize, n_out, rope)
    assert m % tm == 0 and n % tn == 0 and SEQ % tm == 0
    in_specs = [pl.BlockSpec((tm, k), lambda i, j: (i, 0)), _weight_rows_spec(tn, k, layer, col0, 1)]
    args = [h, wt]
    if rope:
        pos_blocks = SEQ // tm
        for t in rope_tabs:
            in_specs.append(pl.BlockSpec((tm, HEAD_DIM), lambda i, j: (i % pos_blocks, 0)))
            args.append(t)
        in_specs.append(pl.BlockSpec((1, tn), lambda i, j: (0, j)))
        args.append(col_scale)
    out_spec = pl.BlockSpec((tm, tn), lambda i, j: (i, j))
    out_shape = jax.ShapeDtypeStruct((m, n), out_dtype)
    res = pl.pallas_call(
        functools.partial(_proj_kernel, mode=mode),
        grid=(m // tm, n // tn),
        in_specs=in_specs,
        out_specs=[out_spec] * n_out,
        out_shape=[out_shape] * n_out,
        compiler_params=_params("parallel", "parallel"),
        name="proj_" + mode,
    )(*args)
    return res if n_out == 2 else res[0]


def _proj_t_kernel(wt_ref, h_ref, out_ref):
    out_ref[...] = _dot_nt(wt_ref[0].astype(BF16), h_ref[...]).astype(out_ref.dtype)


def project_transposed(h, wt, layer, col0, n, out_dtype, tm=1024):
    m, k = h.shape
    tn = math.gcd(PROJ_TN_MAX, n)
    return pl.pallas_call(
        _proj_t_kernel,
        grid=(n // tn, m // tm),
        in_specs=[_weight_rows_spec(tn, k, layer, col0, 0), pl.BlockSpec((tm, k), lambda j, i: (i, 0))],
        out_specs=pl.BlockSpec((tn, tm), lambda j, i: (j, i)),
        out_shape=jax.ShapeDtypeStruct((n, m), out_dtype),
        compiler_params=_params("parallel", "parallel"),
        name="proj_transposed",
    )(wt, h)


NSA_SIDE_BLOCK = A_KV
NSA_SIDE_NAMES = ("a_kc", "a_vc", "a_ks", "a_vs", "a_kw", "a_vw", "a_g")


def _nsa_side_kernel(h_ref, w_ref, c_ref, su_ref, sd_ref, cmp_ref, ksel_ref, kwin_ref, vts_ref, vtw_ref, ga_ref):
    j = pl.program_id(1)
    token_major = lambda: _dot_nt(h_ref[...], w_ref[0].astype(BF16))
    feature_major = lambda: _dot_nt(w_ref[0].astype(BF16), h_ref[...])
    roped = lambda: _rope_lanes(token_major(), c_ref[...], su_ref[...], sd_ref[...])

    @pl.when(j < 2)
    def _():
        cmp_ref[...] = token_major()

    @pl.when(j == 2)
    def _():
        ksel_ref[...] = roped().astype(ksel_ref.dtype)

    @pl.when(j == 3)
    def _():
        vts_ref[...] = feature_major().astype(vts_ref.dtype)

    @pl.when(j == 4)
    def _():
        kwin_ref[...] = roped().astype(kwin_ref.dtype)

    @pl.when(j == 5)
    def _():
        vtw_ref[...] = feature_major().astype(vtw_ref.dtype)

    @pl.when(j == 6)
    def _():
        ga_ref[...] = _sigmoid(token_major())


def project_nsa_side(h, wt, layer, rope_tabs, tm=SEQ):
    m, k = h.shape
    blk = NSA_SIDE_BLOCK
    row0 = W_IN_START[NSA_SIDE_NAMES[0]]
    assert all(W_IN_START[n] == row0 + i * blk for i, n in enumerate(NSA_SIDE_NAMES))
    rows = lambda cols: pl.BlockSpec((tm, blk), cols)
    fixed = rows(lambda i, j: (i, 0))
    feat = pl.BlockSpec((blk, tm), lambda i, j: (0, i))
    tab = pl.BlockSpec((tm, HEAD_DIM), lambda i, j: (0, 0))
    return pl.pallas_call(
        _nsa_side_kernel,
        grid=(m // tm, len(NSA_SIDE_NAMES)),
        in_specs=[pl.BlockSpec((tm, k), lambda i, j: (i, 0)), _weight_rows_spec(blk, k, layer, row0, 1), tab, tab, tab],
        out_specs=[rows(lambda i, j: (i, jnp.minimum(j, 1))), fixed, fixed, feat, feat, fixed],
        out_shape=[jax.ShapeDtypeStruct((m, 2 * blk), F32), jax.ShapeDtypeStruct((m, blk), BF16),
                   jax.ShapeDtypeStruct((m, blk), BF16), jax.ShapeDtypeStruct((blk, m), BF16),
                   jax.ShapeDtypeStruct((blk, m), BF16), jax.ShapeDtypeStruct((m, blk), F32)],
        compiler_params=_params("parallel", "arbitrary"),
        name="proj_nsa_side",
    )(h, wt, *rope_tabs)


def _compress_kernel(k0_ref, k1_ref, v0_ref, v1_ref, pek_ref, w1k_ref, w2k_ref, pev_ref, w1v_ref, w2v_ref,
                     out_ref):
    half = NSA_CMP_STRIDE * HEAD_DIM
    for idx, src_ref in enumerate((k0_ref, k1_ref, v0_ref, v1_ref)):
        is_k = idx < NSA_KV_HEADS
        pe_ref, w1_ref, w2_ref = (pek_ref, w1k_ref, w2k_ref) if is_k else (pev_ref, w1v_ref, w2v_ref)
        x = jnp.concatenate(
            [src_ref[pl.ds(l, NSA_M_PAD, stride=NSA_CMP_STRIDE), :]
             for l in range(NSA_CMP_STRIDE)], axis=1)
        pe = pe_ref[...]
        first = _dot((x + pe[:, :half]).astype(BF16), w1_ref[:half, :])
        second = _dot((x + pe[:, half:]).astype(BF16), w1_ref[half:, :])
        hid = jax.nn.gelu(first + pltpu.roll(second, NSA_M_PAD - 1, axis=0))
        out = _dot(hid.astype(BF16), w2_ref[...])
        out_ref[idx] = (out if is_k else jnp.transpose(out)).astype(out_ref.dtype)


def nsa_compress(pf, pe_k, w1_k, w2_k, pe_v, w1_v, w2_v, batch):
    flat = NSA_CMP_LEN * HEAD_DIM
    const = lambda shape: pl.BlockSpec(shape, lambda b: (0,) * len(shape))
    return pl.pallas_call(
        _compress_kernel,
        grid=(batch,),
        in_specs=[pl.BlockSpec((SEQ, HEAD_DIM), lambda b, c=c: (b, c)) for c in range(4)] + [
                  const((1, flat)), const((flat, NSA_CMP_HIDDEN)), const((NSA_CMP_HIDDEN, HEAD_DIM)),
                  const((1, flat)), const((flat, NSA_CMP_HIDDEN)), const((NSA_CMP_HIDDEN, HEAD_DIM))],
        out_specs=pl.BlockSpec((None, 4, NSA_M_PAD, HEAD_DIM), lambda b: (b, 0, 0, 0)),
        out_shape=jax.ShapeDtypeStruct((batch, 4, NSA_M_PAD, HEAD_DIM), BF16),
        compiler_params=_params("parallel"),
        name="nsa_compress",
    )(pf, pf, pf, pf, pe_k, w1_k, w2_k, pe_v, w1_v, w2_v)


NSA_TQ = 256
NSA_KC = 256
NSA_BLK_PER_CHUNK = NSA_KC // NSA_SEL_BLOCK


def _nsa_kernel(qraw_ref, qrot_ref, cmp_ref, ksel_ref, vselt_ref, kwin_ref, vwint_ref,
                gate_ref, ovt_ref, h_ref, wm_ref, o_ref, gm_ref, bias_scr):
    i = pl.program_id(1)

    gate_pieces = 8
    piece_rows = h_ref.shape[0] // gate_pieces

    def merge_gate_rows(p):
        rows = slice(p * piece_rows, (p + 1) * piece_rows)
        acc = _dot_nt(h_ref[rows, :], wm_ref[0].astype(BF16))
        gm_ref[rows, :] = _sigmoid(acc).astype(gm_ref.dtype)

    merge_gate_rows(0)
    tq, kc_w, rep, d = NSA_TQ, NSA_KC, NSA_REP, HEAD_DIM
    groups = range(NSA_KV_HEADS)
    width = rep * tq
    bpc = NSA_BLK_PER_CHUNK
    t0 = i * tq
    head_cols = lambda g, r: slice((g * rep + r) * d, (g * rep + r + 1) * d)
    stack = lambda ref, g: jnp.concatenate([ref[:, head_cols(g, r)] for r in range(rep)], axis=0)
    q_raw = [stack(qraw_ref, g) for g in groups]
    q_rot = [stack(qrot_ref, g) for g in groups]
    k_of = lambda ref, g, kj: ref[pl.ds(pl.multiple_of(kj * kc_w, kc_w), kc_w), g * d:(g + 1) * d]
    vt_of = lambda ref, g, kj: ref[g * d:(g + 1) * d, pl.ds(pl.multiple_of(kj * kc_w, kc_w), kc_w)]
    col_max = lambda s: jnp.max(s, axis=0, keepdims=True)
    col_sum = lambda p: jnp.sum(p, axis=0, keepdims=True)

    q_pos = t0 + (_iota((NSA_M_PAD, width), 1) & (tq - 1))
    vis = (_iota((NSA_M_PAD, width), 0) * NSA_CMP_STRIDE + (NSA_CMP_LEN - 1)) <= q_pos
    sc = [jnp.where(vis, _dot_nt(cmp_ref[g], q_raw[g]), NEG_INF) for g in groups]
    merge_gate_rows(1)
    ec = [jnp.where(vis, jnp.exp2(s - col_max(s)), 0.0) for s in sc]
    pc = [(e / jnp.maximum(col_sum(e), 1e-30)).astype(BF16) for e in ec]
    merge_gate_rows(2)
    o_cmp = [_dot(cmp_ref[NSA_KV_HEADS + g], pc[g]) for g in groups]
    imp_heads = [_dot(ovt_ref[...], p) for p in pc]
    merge_gate_rows(3)

    blk = _iota((NSA_NB, tq), 0)
    q_blk = (t0 + _iota((NSA_NB, tq), 1)) // NSA_SEL_BLOCK
    forced = (blk == 0) | (blk == q_blk) | (blk == q_blk - 1)
    for g in groups:
        imp = imp_heads[g][:, :tq]
        for r in range(1, rep):
            imp = imp + imp_heads[g][:, r * tq:(r + 1) * tq]
        score = jnp.where(blk <= q_blk, imp + jnp.where(forced, NSA_FORCE_BONUS, 0.0), NEG_INF)
        rank = jnp.zeros((NSA_NB, tq), F32)
        for j in range(NSA_NB):
            sj = score[j:j + 1, :]
            ahead = (sj > score) | ((sj == score) & (blk > j))
            rank = rank + jnp.where(ahead, 1.0, 0.0)
        bias = jnp.where((rank < float(NSA_SEL_TOPN)) & (score > NEG_INF * 0.5), 0.0, NEG_INF)
        bias = jnp.concatenate([bias] * rep, axis=1)
        for c in range(SEQ // kc_w):
            bias_scr[g, c] = bias[c * bpc:(c + 1) * bpc, :]

    def sel_step(kj, ss, carry):
        bs = [bias_scr[g, kj] for g in groups]
        s3 = [s.reshape(bpc, NSA_SEL_BLOCK, width) for s in ss]
        m_new = [jnp.maximum(carry[g][0], col_max(jnp.max(s3[g], axis=1) + bs[g])) for g in groups]
        alpha = [jnp.exp2(carry[g][0] - m_new[g]) for g in groups]
        ps = [jnp.exp2(s3[g] - (m_new[g] - bs[g])[:, None, :]).reshape(kc_w, width) for g in groups]
        ls = [alpha[g] * carry[g][1] + col_sum(ps[g]) for g in groups]
        pvs = [_dot(vt_of(vselt_ref, g, kj), ps[g].astype(BF16)) for g in groups]
        return tuple((m_new[g], ls[g], alpha[g] * carry[g][2] + pvs[g]) for g in groups)

    init = tuple((jnp.full((1, width), NEG_INF, F32), jnp.zeros((1, width), F32), jnp.zeros((d, width), F32))
                 for g in groups)
    carry = lax.fori_loop(
        0, i, lambda kj, c: sel_step(kj, [_dot_nt(k_of(ksel_ref, g, kj), q_rot[g]) for g in groups], c), init)

    key_row = _iota((kc_w, width), 0)
    q_col = _iota((kc_w, width), 1) & (tq - 1)
    causal = key_row <= q_col
    diag = [jnp.where(causal, _dot_nt(k_of(ksel_ref, g, i), q_rot[g]), NEG_INF) for g in groups]
    sel = sel_step(i, diag, carry)

    merge_gate_rows(4)

    far = i - 2
    near = i - 1
    far_ok = jnp.where(far >= 0, 0.0, NEG_INF)
    near_ok = jnp.where(near >= 0, 0.0, NEG_INF)
    far_c = jnp.maximum(far, 0)
    near_c = jnp.maximum(near, 0)
    s_own = [jnp.where(causal, _dot_nt(k_of(kwin_ref, g, i), q_rot[g]), NEG_INF) for g in groups]
    s_near = [_dot_nt(k_of(kwin_ref, g, near_c), q_rot[g]) + near_ok for g in groups]
    s_far = [jnp.where(key_row > q_col, _dot_nt(k_of(kwin_ref, g, far_c), q_rot[g]), NEG_INF) + far_ok
             for g in groups]
    merge_gate_rows(5)
    m_w = [jnp.maximum(jnp.maximum(col_max(s_own[g]), col_max(s_near[g])), col_max(s_far[g])) for g in groups]
    p_own = [jnp.exp2(s_own[g] - m_w[g]) for g in groups]
    merge_gate_rows(6)
    p_near = [jnp.exp2(s_near[g] - m_w[g]) for g in groups]
    p_far = [jnp.exp2(s_far[g] - m_w[g]) for g in groups]
    merge_gate_rows(7)
    l_win =[col_sum(p_own[g]) + col_sum(p_near[g]) + col_sum(p_far[g]) for g in groups]
    acc_win = [_dot(vt_of(vwint_ref, g, i), p_own[g].astype(BF16))
               + _dot(vt_of(vwint_ref, g, near_c), p_near[g].astype(BF16))
               + _dot(vt_of(vwint_ref, g, far_c), p_far[g].astype(BF16)) for g in groups]

    gates = jnp.transpose(gate_ref[...])
    for g in groups:
        o_sel = sel[g][2] / sel[g][1]
        o_win = acc_win[g] / l_win[g]
        for r in range(rep):
            lanes = slice(r * tq, (r + 1) * tq)
            row = 3 * (g * rep + r)
            o = (gates[row:row + 1, :] * o_cmp[g][:, lanes] + gates[row + 1:row + 2, :] * o_sel[:, lanes]
                 + gates[row + 2:row + 3, :] * o_win[:, lanes])
            o_ref[:, head_cols(g, r)] = jnp.transpose(o).astype(o_ref.dtype)


NSA_GATE_TN = 512


def nsa_attention(q_raw, q_rot, cmp_kv, k_sel, k_win, vt_sel, vt_win, gates_a, overlap_t, batch, h, wt, layer, col0):
    tq = NSA_TQ
    nq = SEQ // tq
    g_n = NSA_KV_HEADS
    k = h.shape[1]
    row = lambda b, i: (b * nq + i, 0)
    return pl.pallas_call(
        _nsa_kernel,
        grid=(batch, nq),
        in_specs=[pl.BlockSpec((tq, A_Q), row),
                  pl.BlockSpec((tq, A_Q), row),
                  pl.BlockSpec((None, 2 * g_n, NSA_M_PAD, HEAD_DIM), lambda b, i: (b, 0, 0, 0)),
                  pl.BlockSpec((SEQ, A_KV), lambda b, i: (b, 0)),
                  pl.BlockSpec((A_KV, SEQ), lambda b, i: (0, b)),
                  pl.BlockSpec((SEQ, A_KV), lambda b, i: (b, 0)),
                  pl.BlockSpec((A_KV, SEQ), lambda b, i: (0, b)),
                  pl.BlockSpec((tq, LANES), row),
                  pl.BlockSpec((NSA_NB, NSA_M_PAD), lambda b, i: (0, 0)),
                  pl.BlockSpec((SEQ, k), lambda b, i: (b, 0)),
                  _weight_rows_spec(NSA_GATE_TN, k, layer, col0, 1)],
        out_specs=[pl.BlockSpec((tq, A_Q), row), pl.BlockSpec((SEQ, NSA_GATE_TN), lambda b, i: (b, i))],
        out_shape=[jax.ShapeDtypeStruct((batch * SEQ, A_Q), BF16),
                   jax.ShapeDtypeStruct((batch * SEQ, nq * NSA_GATE_TN), BF16)],
        scratch_shapes=[pltpu.VMEM((g_n, SEQ // NSA_KC, NSA_BLK_PER_CHUNK, NSA_REP * tq), F32)],
        compiler_params=_params("parallel", "parallel"),
        name="nsa_attention",
    )(q_raw, q_rot, cmp_kv, k_sel, vt_sel, k_win, vt_win, gates_a, overlap_t, h, wt)


def _dilated_kernel(q0, k0, v0, q1, k1, v1, q2, k2, v2, o_ref,
                    o0_scr, o1_scr, o2_scr, l0_scr, l1_scr, l2_scr):
    tile = DIL_TILE
    groups = ((q0, k0, v0, o0_scr, l0_scr), (q1, k1, v1, o1_scr, l1_scr), (q2, k2, v2, o2_scr, l2_scr))
    for (window, dil), (q_ref, k_ref, v_ref, og_scr, lg_scr) in zip(DIL_GROUPS, groups):
        assert window // dil == tile
        per_class = SEQ // dil
        tiles_per_class = per_class // tile
        nk = tile if tiles_per_class == 1 else 2 * tile
        a_minus_a = _iota((tile, nk), 0) - _iota((tile, nk), 1)

        def step(u, carry, q_ref=q_ref, k_ref=k_ref, v_ref=v_ref, og_scr=og_scr, lg_scr=lg_scr,
                 dil=dil, tiles_per_class=tiles_per_class, nk=nk, a_minus_a=a_minus_a):
            ts = [u * DIL_UNROLL + a for a in range(DIL_UNROLL)]
            cls = [t // tiles_per_class for t in ts]
            n0 = [(t % tiles_per_class) * tile for t in ts]
            kbase = [jnp.maximum(n - (nk - tile), 0) for n in n0]
            q_rows = [pl.ds(c + dil * n, tile, stride=dil) for c, n in zip(cls, n0)]
            k_rows = [pl.ds(c + dil * kb, nk, stride=dil) for c, kb in zip(cls, kbase)]
            qs = [q_ref[r, :].astype(BF16) for r in q_rows]
            ks = [k_ref[r, :].astype(BF16) for r in k_rows]
            vs = [v_ref[r, :].astype(BF16) for r in k_rows]
            ss = [_dot_nt(q, k) for q, k in zip(qs, ks)]
            masks = []
            for n, kb in zip(n0, kbase):
                dist = a_minus_a + (n - kb)
                masks.append((dist >= 0) & (dist <= tile))
            ss = [jnp.where(mk, s, NEG_INF) for mk, s in zip(masks, ss)]
            ms = [jnp.max(s, axis=-1, keepdims=True) for s in ss]
            es = [jnp.where(mk, jnp.exp2(s - m), 0.0) for mk, s, m in zip(masks, ss, ms)]
            dens = [jnp.maximum(jnp.sum(e, axis=-1, keepdims=True), 1e-30) for e in es]
            os_ = [_dot((e / den).astype(BF16), v) for e, den, v in zip(es, dens, vs)]
            for r, o, m, den in zip(q_rows, os_, ms, dens):
                og_scr[r, :] = o
                lg_scr[r, :] = jnp.broadcast_to(m + jnp.log2(den), (tile, HEAD_DIM))
            return carry

        lax.fori_loop(0, SEQ // tile // DIL_UNROLL, step, 0)

    rows = 256

    def merge_body(c, carry):
        sl = pl.ds(pl.multiple_of(c * rows, rows), rows)
        la, lb, lc = l0_scr[sl, :], l1_scr[sl, :], l2_scr[sl, :]
        mx = jnp.maximum(jnp.maximum(la, lb), lc)
        ea, eb, ec = jnp.exp2(la - mx), jnp.exp2(lb - mx), jnp.exp2(lc - mx)
        tot = ea + eb + ec
        out = (ea / tot) * o0_scr[sl, :] + (eb / tot) * o1_scr[sl, :] + (ec / tot) * o2_scr[sl, :]
        o_ref[sl, :] = out.astype(o_ref.dtype)
        return carry

    lax.fori_loop(0, SEQ // rows, merge_body, 0)


def dilated_attention(b_qk, b_v, batch):
    hp = DIL_HEADS_PER_GROUP
    in_specs, args = [], []
    for g in range(len(DIL_GROUPS)):
        for arr, c0 in ((b_qk, 0), (b_qk, DIL_HEADS), (b_v, 0)):
            in_specs.append(pl.BlockSpec((SEQ, HEAD_DIM), lambda b, j, c0=c0, g=g: (b, c0 + g * hp + j)))
            args.append(arr)
    return pl.pallas_call(
        _dilated_kernel,
        grid=(batch, hp),
        in_specs=in_specs,
        out_specs=pl.BlockSpec((SEQ, HEAD_DIM), lambda b, j: (b, j)),
        out_shape=jax.ShapeDtypeStruct((batch * SEQ, hp * HEAD_DIM), BF16),
        scratch_shapes=[pltpu.VMEM((SEQ, HEAD_DIM), F32)] * 6,
        compiler_params=_params("parallel", "parallel"),
        name="dilated_attention",
    )(*args)


def _moba_kernel(q_ref, k_ref, vt_ref, o_ref, kmean_scr, bias_scr):
    i = pl.program_id(1)
    tq, nb, d = MOBA_BLOCK, MOBA_NB, HEAD_DIM
    heads = range(MOBA_HEADS)
    col = lambda h: slice(h * d, (h + 1) * d)

    @pl.when(i == 0)
    def _():
        avg = jnp.where(_iota((nb, SEQ), 1) // MOBA_BLOCK == _iota((nb, SEQ), 0), 1.0 / MOBA_BLOCK, 0.0)
        kmean_scr[...] = _dot(avg.astype(BF16), k_ref[...]).astype(kmean_scr.dtype)

    blk = _iota((nb, tq), 0)
    own = pl.multiple_of(i * tq, tq)
    causal = _iota((tq, tq), 0) <= _iota((tq, tq), 1)
    qs = [q_ref[:, col(h)] for h in heads]
    gates = [jnp.where(blk < i, _dot_nt(kmean_scr[:, col(h)], qs[h]), NEG_INF) for h in heads]
    ss = [_dot_nt(k_ref[pl.ds(own, tq), col(h)], qs[h]) for h in heads]
    for h in heads:
        gate = gates[h]
        rank = jnp.zeros((nb, tq), F32)
        for j in range(nb):
            gj = gate[j:j + 1, :]
            ahead = (gj > gate) | ((gj == gate) & (blk > j))
            rank = rank + jnp.where(ahead, 1.0, 0.0)
        bias_scr[h] = jnp.where((rank < float(MOBA_TOPK)) & (gate > NEG_INF * 0.5), 0.0, NEG_INF)
    ss = [jnp.where(causal, s, NEG_INF) for s in ss]
    ms = [jnp.max(s, axis=0, keepdims=True) for s in ss]
    ps = [jnp.exp2(s - m) for s, m in zip(ss, ms)]
    ls = [jnp.sum(p, axis=0, keepdims=True) for p in ps]
    accs = [_dot(vt_ref[col(h), pl.ds(own, tq)], ps[h].astype(BF16)) for h in heads]

    def body(kj, carry):
        off = pl.multiple_of(kj * tq, tq)
        bs = [bias_scr[h, pl.ds(kj, 1), :] for h in heads]
        ss = [_dot_nt(k_ref[pl.ds(off, tq), col(h)], qs[h]) for h in heads]
        m_new = [jnp.maximum(carry[h][0], jnp.max(ss[h], axis=0, keepdims=True) + bs[h]) for h in heads]
        alpha = [jnp.exp2(carry[h][0] - m_new[h]) for h in heads]
        ps = [jnp.exp2(ss[h] - (m_new[h] - bs[h])) for h in heads]
        ls = [alpha[h] * carry[h][1] + jnp.sum(ps[h], axis=0, keepdims=True) for h in heads]
        pvs = [_dot(vt_ref[col(h), pl.ds(off, tq)], ps[h].astype(BF16)) for h in heads]
        return tuple((m_new[h], ls[h], alpha[h] * carry[h][2] + pvs[h]) for h in heads)

    fin = lax.fori_loop(0, i, body, tuple((ms[h], ls[h], accs[h]) for h in heads))
    for h in heads:
        _, l, acc = fin[h]
        o_ref[:, col(h)] = jnp.transpose(acc / l).astype(o_ref.dtype)


def moba_attention(c_qk, vt, batch):
    tq = MOBA_BLOCK
    nq = SEQ // tq
    return pl.pallas_call(
        _moba_kernel,
        grid=(batch, nq),
        in_specs=[pl.BlockSpec((tq, C_QKV), lambda b, i: (b * nq + i, 0)),
                  pl.BlockSpec((SEQ, C_QKV), lambda b, i: (b, 1)),
                  pl.BlockSpec((C_QKV, SEQ), lambda b, i: (0, b))],
        out_specs=pl.BlockSpec((tq, C_QKV), lambda b, i: (b * nq + i, 0)),
        out_shape=jax.ShapeDtypeStruct((batch * SEQ, C_QKV), BF16),
        scratch_shapes=[pltpu.VMEM((MOBA_NB, C_QKV), BF16), pltpu.VMEM((MOBA_HEADS, MOBA_NB, tq), F32)],
        compiler_params=_params("parallel", "arbitrary"),
        name="moba_attention",
    )(c_qk, c_qk, vt)


def _merge_kernel(oa_ref, ob_ref, oc_ref, ga_ref, gb_ref, gc_ref, wa_ref, wb_ref, wc_ref, out_ref):
    y = ga_ref[...] * _dot(oa_ref[...], wa_ref[...].astype(BF16))
    y = y + gb_ref[...] * _dot(ob_ref[...], wb_ref[...].astype(BF16))
    y = y + gc_ref[...] * _dot(oc_ref[...], wc_ref[...].astype(BF16))
    out_ref[...] = y.astype(out_ref.dtype)


def gated_merge(o_a, o_b, o_c, gates_ab, gates_c, w_a, w_b, w_c, layer, tm=1024, tn=512):
    m = o_a.shape[0]
    nb = D_MODEL // tn
    act = lambda width: pl.BlockSpec((tm, width), lambda j, i: (i, 0))
    gate = lambda g: pl.BlockSpec((tm, tn), lambda j, i, g=g: (i, g * nb + j))
    wgt = lambda width: pl.BlockSpec((None, width, tn), lambda j, i: (layer, 0, j))
    return pl.pallas_call(
        _merge_kernel,
        grid=(nb, m // tm),
        in_specs=[act(o_a.shape[1]), act(o_b.shape[1]), act(o_c.shape[1]),
                  gate(0), gate(1), gate(0),
                  wgt(w_a.shape[1]), wgt(w_b.shape[1]), wgt(w_c.shape[1])],
        out_specs=pl.BlockSpec((tm, tn), lambda j, i: (i, j)),
        out_shape=jax.ShapeDtypeStruct((m, D_MODEL), BF16),
        compiler_params=_params("parallel", "parallel"),
        name="gated_merge",
    )(o_a, o_b, o_c, gates_ab, gates_ab, gates_c, w_a, w_b, w_c)


def _out_proj_kernel(y_ref, w_ref, x_ref, g_ref, xo_ref, h_ref):
    x_new = x_ref[...] + _dot(y_ref[...], w_ref[...])
    xo_ref[...] = x_new
    h_ref[...] = _rmsnorm_rows(x_new, g_ref[...]).astype(h_ref.dtype)


def out_proj_residual_norm(y, w_o, layer, x2, g, tm=512):
    m, d = x2.shape
    row = pl.BlockSpec((tm, d), lambda i: (i, 0))
    return pl.pallas_call(
        _out_proj_kernel,
        grid=(m // tm,),
        in_specs=[row, pl.BlockSpec((None, d, d), lambda i: (layer, 0, 0)), row,
                  pl.BlockSpec((1, d), lambda i: (0, 0))],
        out_specs=[row, row],
        out_shape=[jax.ShapeDtypeStruct((m, d), F32), jax.ShapeDtypeStruct((m, d), BF16)],
        compiler_params=_params("parallel"),
        name="out_proj",
    )(y, w_o, x2, g.reshape(1, d))


def _mlp_kernel(h_ref, w1_ref, w2_ref, x_ref, g_ref, *outs):
    acc_ref, hn_ref = outs[0], outs[-1]
    f = pl.program_id(1)

    @pl.when(f == 0)
    def _():
        acc_ref[...] = jnp.zeros_like(acc_ref)

    u = jnp.square(jnp.maximum(_dot(h_ref[...], w1_ref[...].astype(BF16)), 0.0))
    acc_ref[...] += _dot(u.astype(BF16), w2_ref[...].astype(BF16))

    @pl.when(f == pl.num_programs(1) - 1)
    def _():
        x_new = x_ref[...] + acc_ref[...]
        if len(outs) == 2:
            acc_ref[...] = x_new
        hn_ref[...] = _rmsnorm_rows(x_new, g_ref[...]).astype(hn_ref.dtype)


def mlp_residual_norm(h2, w1, w2, layer, x2, g_next, next_dtype, emit_x, tm=1024, tf=512):
    m, d = x2.shape
    row = pl.BlockSpec((tm, d), lambda i, f: (i, 0))
    row_once = pl.BlockSpec((tm, d), lambda i, f: (i, 0), pipeline_mode=pl.Buffered(1))
    out_shape = [jax.ShapeDtypeStruct((m, d), F32)] * emit_x + [jax.ShapeDtypeStruct((m, d), next_dtype)]
    res = pl.pallas_call(
        _mlp_kernel,
        grid=(m // tm, D_FF // tf),
        in_specs=[row, pl.BlockSpec((None, d, tf), lambda i, f: (layer, 0, f)),
                  pl.BlockSpec((None, tf, d), lambda i, f: (layer, f, 0)),
                  row_once, pl.BlockSpec((1, d), lambda i, f: (0, 0))],
        out_specs=[row_once] * len(out_shape),
        out_shape=out_shape,
        compiler_params=_params("parallel", "arbitrary"),
        name="mlp",
    )(h2, w1, w2, x2, g_next.reshape(1, d))
    return (res[0], res[1]) if emit_x else (None, res[0])


def _rope_tables():
    inv = ROPE_THETA ** (-jnp.arange(0, ROPE_DIM, 2, dtype=F32) / ROPE_DIM)
    ang = jnp.arange(SEQ, dtype=F32)[:, None] * inv[None, :]
    cos, sin = jnp.cos(ang), jnp.sin(ang)
    zeros = jnp.zeros((SEQ, HEAD_DIM - ROPE_DIM), F32)
    zero_h = jnp.zeros((SEQ, ROPE_HALF), F32)
    c = jnp.concatenate([cos, cos, jnp.ones_like(zeros)], axis=1)
    s_up = jnp.concatenate([-sin, zero_h, zeros], axis=1)
    s_dn = jnp.concatenate([zero_h, sin, zeros], axis=1)
    return c, s_up, s_dn


def _overlap_table_t():
    cs = np.arange(NSA_M_PAD)[None, :] * NSA_CMP_STRIDE
    bs = np.arange(NSA_NB)[:, None] * NSA_SEL_BLOCK
    ov = np.clip(np.minimum(cs + NSA_CMP_LEN, bs + NSA_SEL_BLOCK) - np.maximum(cs, bs), 0, None) / NSA_CMP_LEN
    ov[:, NSA_M_PAD - 1] = 0.0
    return jnp.asarray(ov, dtype=BF16)


W_IN_NAMES = ("a_q", "a_kc", "a_vc", "a_ks", "a_vs", "a_kw", "a_vw", "a_g",
              "b_q", "b_k", "b_v", "c_q", "c_k", "c_v", "m_a", "m_b", "m_c")
W_IN_START = dict(zip(W_IN_NAMES, np.cumsum((0,) + IN_SPLIT_SIZES[:-1]).tolist()))
W_IN_SIZE = dict(zip(W_IN_NAMES, IN_SPLIT_SIZES))


def _scales(*widths_and_values):
    return jnp.concatenate([jnp.full((1, w), v, F32) for w, v in widths_and_values], axis=1)


def _layer(x2, h, batch, layer, tabs, overlap_t, w_in_t, pe_k, w1_k, w2_k, pe_v, w1_v, w2_v,
           w_br_a, w_br_b, w_br_c, w_o, mlp_g, w_mlp_in, w_mlp_out, g_next, last):
    def proj(first, n, *args):
        return project(h, w_in_t, layer, W_IN_START[first], n, *args)

    def proj_t(name):
        return project_transposed(h, w_in_t, layer, W_IN_START[name], W_IN_SIZE[name], BF16)

    q_raw, q_rot = proj("a_q", A_Q, "both", BF16, tabs, _scales((A_Q, Q_SCALE)))
    cmp_src, k_sel, k_win, vt_sel, vt_win, gates_a = project_nsa_side(h, w_in_t, layer, tabs)
    b_qk = proj("b_q", 2 * B_QKV, "rope", F32, tabs, _scales((B_QKV, Q_SCALE), (B_QKV, 1.0)))
    b_v = proj("b_v", B_QKV, "plain", F32)
    c_qk = proj("c_q", 2 * C_QKV, "rope", BF16, tabs, _scales((C_QKV, Q_SCALE), (C_QKV, 1.0)))
    gates_c = proj("m_c", D_MODEL, "sigmoid", BF16)
    vt_moba = proj_t("c_v")

    flat = NSA_CMP_LEN * HEAD_DIM
    cmp_kv = nsa_compress(cmp_src, pe_k.reshape(1, flat), w1_k.reshape(flat, NSA_CMP_HIDDEN).astype(BF16),
                          w2_k.astype(BF16), pe_v.reshape(1, flat),
                          w1_v.reshape(flat, NSA_CMP_HIDDEN).astype(BF16), w2_v.astype(BF16), batch)
    o_a, gates_ab = nsa_attention(q_raw, q_rot, cmp_kv, k_sel, k_win, vt_sel, vt_win, gates_a, overlap_t, batch,
                                  h, w_in_t, layer, W_IN_START["m_a"])
    assert gates_ab.shape[1] == 2 * D_MODEL
    o_b = dilated_attention(b_qk, b_v, batch)
    o_c = moba_attention(c_qk, vt_moba, batch)

    merged = gated_merge(o_a, o_b, o_c, gates_ab, gates_c, w_br_a, w_br_b, w_br_c, layer)
    x2, h2 = out_proj_residual_norm(merged, w_o, layer, x2, mlp_g)
    return mlp_residual_norm(h2, w_mlp_in, w_mlp_out, layer, x2, g_next, F32 if last else BF16, emit_x=not last)


def kernel(x, attn_norm_g, w_in, cmp_pe_k, cmp_w1_k, cmp_w2_k, cmp_pe_v, cmp_w1_v, cmp_w2_v,
           w_br_a, w_br_b, w_br_c, w_o, mlp_norm_g, w_mlp_in, w_mlp_out, final_norm_g):
    batch, seq, d = x.shape
    assert seq == SEQ and d == D_MODEL
    depth = w_in.shape[0]
    tabs = _rope_tables()
    overlap_t = _overlap_table_t()
    w_in_t = jnp.transpose(w_in, (0, 2, 1))
    w_o = w_o.astype(BF16)
    x2 = x.reshape(batch * seq, d)
    h = rmsnorm(x2, attn_norm_g[0], BF16)
    for l in range(depth):
        last = l == depth - 1
        g_next = final_norm_g if last else attn_norm_g[l + 1]
        x2, h = _layer(x2, h, batch, l, tabs, overlap_t, w_in_t,
                       cmp_pe_k[l], cmp_w1_k[l], cmp_w2_k[l], cmp_pe_v[l], cmp_w1_v[l], cmp_w2_v[l],
                       w_br_a, w_br_b, w_br_c, w_o, mlp_norm_g[l], w_mlp_in, w_mlp_out, g_next, last)
    return h.reshape(batch, seq, d)
```

```python
import functools
import math

import numpy as np
import jax
import jax.numpy as jnp
from jax import lax
from jax.experimental import pallas as pl
from jax.experimental.pallas import tpu as pltpu

D_MODEL = 2048
SEQ = 2048
HEAD_DIM = 128
ROPE_THETA = 500000.0
ROPE_DIM = HEAD_DIM // 4
ROPE_HALF = ROPE_DIM // 2
NORM_EPS = 1e-6
NEG_INF = -1e30
Q_SCALE = HEAD_DIM ** -0.5 * math.log2(math.e)

NSA_HEADS = 8
NSA_KV_HEADS = 2
NSA_REP = NSA_HEADS // NSA_KV_HEADS
NSA_CMP_LEN = 32
NSA_CMP_STRIDE = 16
NSA_CMP_HIDDEN = 256
NSA_SEL_BLOCK = 64
NSA_SEL_TOPN = 16
NSA_WINDOW = 512
NSA_FORCE_BONUS = 1e4
NSA_NB = SEQ // NSA_SEL_BLOCK
NSA_M_PAD = SEQ // NSA_CMP_STRIDE

DIL_GROUPS = ((128, 1), (512, 4), (2048, 16))
DIL_HEADS_PER_GROUP = 4
DIL_HEADS = DIL_HEADS_PER_GROUP * len(DIL_GROUPS)
DIL_TILE = 128
DIL_UNROLL = 16

MOBA_HEADS = 8
MOBA_BLOCK = 256
MOBA_TOPK = 3
MOBA_NB = SEQ // MOBA_BLOCK

D_FF = 4 * D_MODEL
A_Q = NSA_HEADS * HEAD_DIM
A_KV = NSA_KV_HEADS * HEAD_DIM
A_G = 3 * NSA_HEADS
B_QKV = DIL_HEADS * HEAD_DIM
C_QKV = MOBA_HEADS * HEAD_DIM
IN_SPLIT_SIZES = (A_Q, A_KV, A_KV, A_KV, A_KV, A_KV, A_KV, A_G,
                  B_QKV, B_QKV, B_QKV, C_QKV, C_QKV, C_QKV,
                  D_MODEL, D_MODEL, D_MODEL)

LANES = 128
F32_SUBLANES = 8
PROJ_TN_MAX = 1024
PROJ_VMEM_BUDGET = 56 * 1024 * 1024
VMEM_LIMIT = 60 * 1024 * 1024

BF16 = jnp.bfloat16
F32 = jnp.float32


def _params(*sem):
    return pltpu.CompilerParams(dimension_semantics=sem, vmem_limit_bytes=VMEM_LIMIT)


def _dot(a, b):
    return jnp.dot(a, b, preferred_element_type=F32)


def _dot_nt(a, b):
    return lax.dot_general(a, b, (((1,), (1,)), ((), ())), preferred_element_type=F32)


def _iota(shape, axis):
    return lax.broadcasted_iota(jnp.int32, shape, axis)


def _rmsnorm_rows(x, g):
    y = x * lax.rsqrt(jnp.mean(x * x, axis=-1, keepdims=True) + NORM_EPS)
    return y * g


def _rmsnorm_kernel(x_ref, g_ref, h_ref):
    h_ref[...] = _rmsnorm_rows(x_ref[...], g_ref[...]).astype(h_ref.dtype)


def rmsnorm(x2, g, out_dtype, tm=512):
    m, d = x2.shape
    return pl.pallas_call(
        _rmsnorm_kernel,
        grid=(m // tm,),
        in_specs=[pl.BlockSpec((tm, d), lambda i: (i, 0)), pl.BlockSpec((1, d), lambda i: (0, 0))],
        out_specs=pl.BlockSpec((tm, d), lambda i: (i, 0)),
        out_shape=jax.ShapeDtypeStruct((m, d), out_dtype),
        compiler_params=_params("parallel"),
        name="rmsnorm",
    )(x2, g.reshape(1, d))


def _rope_lanes(acc, c, s_up, s_dn):
    tn = acc.shape[1]
    reps = tn // HEAD_DIM
    if reps > 1:
        c = jnp.concatenate([c] * reps, axis=1)
        s_up = jnp.concatenate([s_up] * reps, axis=1)
        s_dn = jnp.concatenate([s_dn] * reps, axis=1)
    up = pltpu.roll(acc, tn - ROPE_HALF, axis=1)
    dn = pltpu.roll(acc, ROPE_HALF, axis=1)
    return acc * c + up * s_up + dn * s_dn


def _sigmoid(x):
    return 0.5 * jnp.tanh(0.5 * x) + 0.5


def _proj_kernel(*refs, mode):
    if mode in ("rope", "both"):
        h_ref, w_ref, c_ref, su_ref, sd_ref, cs_ref = refs[:6]
        outs = refs[6:]
    else:
        h_ref, w_ref = refs[:2]
        outs = refs[2:]
    acc = _dot_nt(h_ref[...], w_ref[0].astype(BF16))
    if mode == "plain":
        outs[0][...] = acc.astype(outs[0].dtype)
    elif mode == "sigmoid":
        outs[0][...] = _sigmoid(acc).astype(outs[0].dtype)
    else:
        col_scale = cs_ref[...]
        roped = _rope_lanes(acc, c_ref[...], su_ref[...], sd_ref[...]) * col_scale
        if mode == "both":
            outs[0][...] = (acc * col_scale).astype(outs[0].dtype)
            outs[1][...] = roped.astype(outs[1].dtype)
        else:
            outs[0][...] = roped.astype(outs[0].dtype)


def _proj_tiles(k, n, w_bytes, out_bytes, n_out, rope):
    def fits(tm, tn):
        blocks = (tm * k * 2 + k * tn * w_bytes + n_out * tm * tn * out_bytes
                  + (3 * tm * HEAD_DIM * 4 if rope else 0))
        temporaries = 2 * tm * tn * 4
        return 2 * blocks + temporaries <= PROJ_VMEM_BUDGET

    widths = [math.gcd(w, n) for w in (PROJ_TN_MAX, PROJ_TN_MAX // 2)]
    for tm in (SEQ, SEQ // 2):
        for tn in widths:
            if fits(tm, tn):
                return tm, tn
    return SEQ // 2, widths[-1]


def _weight_rows_spec(tn, k, layer, row0, step_axis):
    assert row0 % F32_SUBLANES == 0
    return pl.BlockSpec(
        (pl.Element(1), pl.Element(tn), pl.Element(k)),
        lambda *idx: (layer, pl.multiple_of(row0 + idx[step_axis] * tn, F32_SUBLANES), 0))


def project(h, wt, layer, col0, n, mode, out_dtype, rope_tabs=None, col_scale=None):
    m, k = h.shape
    n_out = 2 if mode == "both" else 1
    rope = mode in ("rope", "both")
    tm, tn = _proj_tiles(k, n, wt.dtype.itemsize, jnp.dtype(out_dtype).itemsize, n_out, rope)
    assert m % tm == 0 and n % tn == 0 and SEQ % tm == 0
    in_specs = [pl.BlockSpec((tm, k), lambda i, j: (i, 0)), _weight_rows_spec(tn, k, layer, col0, 1)]
    args = [h, wt]
    if rope:
        pos_blocks = SEQ // tm
        for t in rope_tabs:
            in_specs.append(pl.BlockSpec((tm, HEAD_DIM), lambda i, j: (i % pos_blocks, 0)))
            args.append(t)
        in_specs.append(pl.BlockSpec((1, tn), lambda i, j: (0, j)))
        args.append(col_scale)
    out_spec = pl.BlockSpec((tm, tn), lambda i, j: (i, j))
    out_shape = jax.ShapeDtypeStruct((m, n), out_dtype)
    res = pl.pallas_call(
        functools.partial(_proj_kernel, mode=mode),
        grid=(m // tm, n // tn),
        in_specs=in_specs,
        out_specs=[out_spec] * n_out,
        out_shape=[out_shape] * n_out,
        compiler_params=_params("parallel", "parallel"),
        name="proj_" + mode,
    )(*args)
    return res if n_out == 2 else res[0]


def _proj_t_kernel(wt_ref, h_ref, out_ref):
    out_ref[...] = _dot_nt(wt_ref[0].astype(BF16), h_ref[...]).astype(out_ref.dtype)


def project_transposed(h, wt, layer, col0, n, out_dtype, tm=1024):
    m, k = h.shape
    tn = math.gcd(PROJ_TN_MAX, n)
    return pl.pallas_call(
        _proj_t_kernel,
        grid=(n // tn, m // tm),
        in_specs=[_weight_rows_spec(tn, k, layer, col0, 0), pl.BlockSpec((tm, k), lambda j, i: (i, 0))],
        out_specs=pl.BlockSpec((tn, tm), lambda j, i: (j, i)),
        out_shape=jax.ShapeDtypeStruct((n, m), out_dtype),
        compiler_params=_params("parallel", "parallel"),
        name="proj_transposed",
    )(wt, h)


NSA_SIDE_BLOCK = A_KV
NSA_SIDE_NAMES = ("a_kc", "a_vc", "a_ks", "a_vs", "a_kw", "a_vw", "a_g")


def _nsa_side_kernel(h_ref, w_ref, c_ref, su_ref, sd_ref, cmp_ref, ksel_ref, kwin_ref, vts_ref, vtw_ref, ga_ref):
    j = pl.program_id(1)
    token_major = lambda: _dot_nt(h_ref[...], w_ref[0].astype(BF16))
    feature_major = lambda: _dot_nt(w_ref[0].astype(BF16), h_ref[...])
    roped = lambda: _rope_lanes(token_major(), c_ref[...], su_ref[...], sd_ref[...])

    @pl.when(j < 2)
    def _():
        cmp_ref[...] = token_major()

    @pl.when(j == 2)
    def _():
        ksel_ref[...] = roped().astype(ksel_ref.dtype)

    @pl.when(j == 3)
    def _():
        vts_ref[...] = feature_major().astype(vts_ref.dtype)

    @pl.when(j == 4)
    def _():
        kwin_ref[...] = roped().astype(kwin_ref.dtype)

    @pl.when(j == 5)
    def _():
        vtw_ref[...] = feature_major().astype(vtw_ref.dtype)

    @pl.when(j == 6)
    def _():
        ga_ref[...] = _sigmoid(token_major())


def project_nsa_side(h, wt, layer, rope_tabs, tm=SEQ):
    m, k = h.shape
    blk = NSA_SIDE_BLOCK
    row0 = W_IN_START[NSA_SIDE_NAMES[0]]
    assert all(W_IN_START[n] == row0 + i * blk for i, n in enumerate(NSA_SIDE_NAMES))
    rows = lambda cols: pl.BlockSpec((tm, blk), cols)
    fixed = rows(lambda i, j: (i, 0))
    feat = pl.BlockSpec((blk, tm), lambda i, j: (0, i))
    tab = pl.BlockSpec((tm, HEAD_DIM), lambda i, j: (0, 0))
    return pl.pallas_call(
        _nsa_side_kernel,
        grid=(m // tm, len(NSA_SIDE_NAMES)),
        in_specs=[pl.BlockSpec((tm, k), lambda i, j: (i, 0)), _weight_rows_spec(blk, k, layer, row0, 1), tab, tab, tab],
        out_specs=[rows(lambda i, j: (i, jnp.minimum(j, 1))), fixed, fixed, feat, feat, fixed],
        out_shape=[jax.ShapeDtypeStruct((m, 2 * blk), F32), jax.ShapeDtypeStruct((m, blk), BF16),
                   jax.ShapeDtypeStruct((m, blk), BF16), jax.ShapeDtypeStruct((blk, m), BF16),
                   jax.ShapeDtypeStruct((blk, m), BF16), jax.ShapeDtypeStruct((m, blk), F32)],
        compiler_params=_params("parallel", "arbitrary"),
        name="proj_nsa_side",
    )(h, wt, *rope_tabs)


def _compress_kernel(k0_ref, k1_ref, v0_ref, v1_ref, pek_ref, w1k_ref, w2k_ref, pev_ref, w1v_ref, w2v_ref,
                     out_ref):
    half = NSA_CMP_STRIDE * HEAD_DIM
    for idx, src_ref in enumerate((k0_ref, k1_ref, v0_ref, v1_ref)):
        is_k = idx < NSA_KV_HEADS
        pe_ref, w1_ref, w2_ref = (pek_ref, w1k_ref, w2k_ref) if is_k else (pev_ref, w1v_ref, w2v_ref)
        x = jnp.concatenate(
            [src_ref[pl.ds(l, NSA_M_PAD, stride=NSA_CMP_STRIDE), :]
             for l in range(NSA_CMP_STRIDE)], axis=1)
        pe = pe_ref[...]
        first = _dot((x + pe[:, :half]).astype(BF16), w1_ref[:half, :])
        second = _dot((x + pe[:, half:]).astype(BF16), w1_ref[half:, :])
        hid = jax.nn.gelu(first + pltpu.roll(second, NSA_M_PAD - 1, axis=0))
        out = _dot(hid.astype(BF16), w2_ref[...])
        out_ref[idx] = (out if is_k else jnp.transpose(out)).astype(out_ref.dtype)


def nsa_compress(pf, pe_k, w1_k, w2_k, pe_v, w1_v, w2_v, batch):
    flat = NSA_CMP_LEN * HEAD_DIM
    const = lambda shape: pl.BlockSpec(shape, lambda b: (0,) * len(shape))
    return pl.pallas_call(
        _compress_kernel,
        grid=(batch,),
        in_specs=[pl.BlockSpec((SEQ, HEAD_DIM), lambda b, c=c: (b, c)) for c in range(4)] + [
                  const((1, flat)), const((flat, NSA_CMP_HIDDEN)), const((NSA_CMP_HIDDEN, HEAD_DIM)),
                  const((1, flat)), const((flat, NSA_CMP_HIDDEN)), const((NSA_CMP_HIDDEN, HEAD_DIM))],
        out_specs=pl.BlockSpec((None, 4, NSA_M_PAD, HEAD_DIM), lambda b: (b, 0, 0, 0)),
        out_shape=jax.ShapeDtypeStruct((batch, 4, NSA_M_PAD, HEAD_DIM), BF16),
        compiler_params=_params("parallel"),
        name="nsa_compress",
    )(pf, pf, pf, pf, pe_k, w1_k, w2_k, pe_v, w1_v, w2_v)


NSA_TQ = 256
NSA_KC = 256
NSA_BLK_PER_CHUNK = NSA_KC // NSA_SEL_BLOCK


def _nsa_kernel(qraw_ref, qrot_ref, cmp_ref, ksel_ref, vselt_ref, kwin_ref, vwint_ref,
                gate_ref, ovt_ref, o_ref, bias_scr):
    i = pl.program_id(1)
    tq, kc_w, rep, d = NSA_TQ, NSA_KC, NSA_REP, HEAD_DIM
    groups = range(NSA_KV_HEADS)
    width = rep * tq
    bpc = NSA_BLK_PER_CHUNK
    t0 = i * tq
    head_cols = lambda g, r: slice((g * rep + r) * d, (g * rep + r + 1) * d)
    stack = lambda ref, g: jnp.concatenate([ref[:, head_cols(g, r)] for r in range(rep)], axis=0)
    q_raw = [stack(qraw_ref, g) for g in groups]
    q_rot = [stack(qrot_ref, g) for g in groups]
    k_of = lambda ref, g, kj: ref[pl.ds(pl.multiple_of(kj * kc_w, kc_w), kc_w), g * d:(g + 1) * d]
    vt_of = lambda ref, g, kj: ref[g * d:(g + 1) * d, pl.ds(pl.multiple_of(kj * kc_w, kc_w), kc_w)]
    col_max = lambda s: jnp.max(s, axis=0, keepdims=True)
    col_sum = lambda p: jnp.sum(p, axis=0, keepdims=True)

    q_pos = t0 + (_iota((NSA_M_PAD, width), 1) & (tq - 1))
    vis = (_iota((NSA_M_PAD, width), 0) * NSA_CMP_STRIDE + (NSA_CMP_LEN - 1)) <= q_pos
    sc = [jnp.where(vis, _dot_nt(cmp_ref[g], q_raw[g]), NEG_INF) for g in groups]
    ec = [jnp.where(vis, jnp.exp2(s - col_max(s)), 0.0) for s in sc]
    pc = [(e / jnp.maximum(col_sum(e), 1e-30)).astype(BF16) for e in ec]
    o_cmp = [_dot(cmp_ref[NSA_KV_HEADS + g], pc[g]) for g in groups]
    imp_heads = [_dot(ovt_ref[...], p) for p in pc]

    blk = _iota((NSA_NB, tq), 0)
    q_blk = (t0 + _iota((NSA_NB, tq), 1)) // NSA_SEL_BLOCK
    forced = (blk == 0) | (blk == q_blk) | (blk == q_blk - 1)
    for g in groups:
        imp = imp_heads[g][:, :tq]
        for r in range(1, rep):
            imp = imp + imp_heads[g][:, r * tq:(r + 1) * tq]
        score = jnp.where(blk <= q_blk, imp + jnp.where(forced, NSA_FORCE_BONUS, 0.0), NEG_INF)
        rank = jnp.zeros((NSA_NB, tq), F32)
        for j in range(NSA_NB):
            sj = score[j:j + 1, :]
            ahead = (sj > score) | ((sj == score) & (blk > j))
            rank = rank + jnp.where(ahead, 1.0, 0.0)
        bias = jnp.where((rank < float(NSA_SEL_TOPN)) & (score > NEG_INF * 0.5), 0.0, NEG_INF)
        bias = jnp.concatenate([bias] * rep, axis=1)
        for c in range(SEQ // kc_w):
            bias_scr[g, c] = bias[c * bpc:(c + 1) * bpc, :]

    def sel_step(kj, ss, carry):
        bs = [bias_scr[g, kj] for g in groups]
        s3 = [s.reshape(bpc, NSA_SEL_BLOCK, width) for s in ss]
        m_new = [jnp.maximum(carry[g][0], col_max(jnp.max(s3[g], axis=1) + bs[g])) for g in groups]
        alpha = [jnp.exp2(carry[g][0] - m_new[g]) for g in groups]
        ps = [jnp.exp2(s3[g] - (m_new[g] - bs[g])[:, None, :]).reshape(kc_w, width) for g in groups]
        ls = [alpha[g] * carry[g][1] + col_sum(ps[g]) for g in groups]
        pvs = [_dot(vt_of(vselt_ref, g, kj), ps[g].astype(BF16)) for g in groups]
        return tuple((m_new[g], ls[g], alpha[g] * carry[g][2] + pvs[g]) for g in groups)

    init = tuple((jnp.full((1, width), NEG_INF, F32), jnp.zeros((1, width), F32), jnp.zeros((d, width), F32))
                 for g in groups)
    carry = lax.fori_loop(
        0, i, lambda kj, c: sel_step(kj, [_dot_nt(k_of(ksel_ref, g, kj), q_rot[g]) for g in groups], c), init)

    key_row = _iota((kc_w, width), 0)
    q_col = _iota((kc_w, width), 1) & (tq - 1)
    causal = key_row <= q_col
    diag = [jnp.where(causal, _dot_nt(k_of(ksel_ref, g, i), q_rot[g]), NEG_INF) for g in groups]
    sel = sel_step(i, diag, carry)

    far = i - 2
    near = i - 1
    far_ok = jnp.where(far >= 0, 0.0, NEG_INF)
    near_ok = jnp.where(near >= 0, 0.0, NEG_INF)
    far_c = jnp.maximum(far, 0)
    near_c = jnp.maximum(near, 0)
    s_own = [jnp.where(causal, _dot_nt(k_of(kwin_ref, g, i), q_rot[g]), NEG_INF) for g in groups]
    s_near = [_dot_nt(k_of(kwin_ref, g, near_c), q_rot[g]) + near_ok for g in groups]
    s_far = [jnp.where(key_row > q_col, _dot_nt(k_of(kwin_ref, g, far_c), q_rot[g]), NEG_INF) + far_ok
             for g in groups]
    m_w = [jnp.maximum(jnp.maximum(col_max(s_own[g]), col_max(s_near[g])), col_max(s_far[g])) for g in groups]
    p_own = [jnp.exp2(s_own[g] - m_w[g]) for g in groups]
    p_near = [jnp.exp2(s_near[g] - m_w[g]) for g in groups]
    p_far = [jnp.exp2(s_far[g] - m_w[g]) for g in groups]
    l_win = [col_sum(p_own[g]) + col_sum(p_near[g]) + col_sum(p_far[g]) for g in groups]
    acc_win = [_dot(vt_of(vwint_ref, g, i), p_own[g].astype(BF16))
               + _dot(vt_of(vwint_ref, g, near_c), p_near[g].astype(BF16))
               + _dot(vt_of(vwint_ref, g, far_c), p_far[g].astype(BF16)) for g in groups]

    gates = jnp.transpose(gate_ref[...])
    for g in groups:
        o_sel = sel[g][2] / sel[g][1]
        o_win = acc_win[g] / l_win[g]
        for r in range(rep):
            lanes = slice(r * tq, (r + 1) * tq)
            row = 3 * (g * rep + r)
            o = (gates[row:row + 1, :] * o_cmp[g][:, lanes] + gates[row + 1:row + 2, :] * o_sel[:, lanes]
                 + gates[row + 2:row + 3, :] * o_win[:, lanes])
            o_ref[:, head_cols(g, r)] = jnp.transpose(o).astype(o_ref.dtype)


def nsa_attention(q_raw, q_rot, cmp_kv, k_sel, k_win, vt_sel, vt_win, gates_a, overlap_t, batch):
    tq = NSA_TQ
    nq = SEQ // tq
    g_n = NSA_KV_HEADS
    row = lambda b, i: (b * nq + i, 0)
    return pl.pallas_call(
        _nsa_kernel,
        grid=(batch, nq),
        in_specs=[pl.BlockSpec((tq, A_Q), row),
                  pl.BlockSpec((tq, A_Q), row),
                  pl.BlockSpec((None, 2 * g_n, NSA_M_PAD, HEAD_DIM), lambda b, i: (b, 0, 0, 0)),
                  pl.BlockSpec((SEQ, A_KV), lambda b, i: (b, 0)),
                  pl.BlockSpec((A_KV, SEQ), lambda b, i: (0, b)),
                  pl.BlockSpec((SEQ, A_KV), lambda b, i: (b, 0)),
                  pl.BlockSpec((A_KV, SEQ), lambda b, i: (0, b)),
                  pl.BlockSpec((tq, LANES), row),
                  pl.BlockSpec((NSA_NB, NSA_M_PAD), lambda b, i: (0, 0))],
        out_specs=pl.BlockSpec((tq, A_Q), row),
        out_shape=jax.ShapeDtypeStruct((batch * SEQ, A_Q), BF16),
        scratch_shapes=[pltpu.VMEM((g_n, SEQ // NSA_KC, NSA_BLK_PER_CHUNK, NSA_REP * tq), F32)],
        compiler_params=_params("parallel", "parallel"),
        name="nsa_attention",
    )(q_raw, q_rot, cmp_kv, k_sel, vt_sel, k_win, vt_win, gates_a, overlap_t)


def _dilated_kernel(q0, k0, v0, q1, k1, v1, q2, k2, v2, o_ref,
                    o0_scr, o1_scr, o2_scr, l0_scr, l1_scr, l2_scr):
    tile = DIL_TILE
    groups = ((q0, k0, v0, o0_scr, l0_scr), (q1, k1, v1, o1_scr, l1_scr), (q2, k2, v2, o2_scr, l2_scr))
    for (window, dil), (q_ref, k_ref, v_ref, og_scr, lg_scr) in zip(DIL_GROUPS, groups):
        assert window // dil == tile
        per_class = SEQ // dil
        tiles_per_class = per_class // tile
        nk = tile if tiles_per_class == 1 else 2 * tile
        a_minus_a = _iota((tile, nk), 0) - _iota((tile, nk), 1)

        def step(u, carry, q_ref=q_ref, k_ref=k_ref, v_ref=v_ref, og_scr=og_scr, lg_scr=lg_scr,
                 dil=dil, tiles_per_class=tiles_per_class, nk=nk, a_minus_a=a_minus_a):
            ts = [u * DIL_UNROLL + a for a in range(DIL_UNROLL)]
            cls = [t // tiles_per_class for t in ts]
            n0 = [(t % tiles_per_class) * tile for t in ts]
            kbase = [jnp.maximum(n - (nk - tile), 0) for n in n0]
            q_rows = [pl.ds(c + dil * n, tile, stride=dil) for c, n in zip(cls, n0)]
            k_rows = [pl.ds(c + dil * kb, nk, stride=dil) for c, kb in zip(cls, kbase)]
            qs = [q_ref[r, :].astype(BF16) for r in q_rows]
            ks = [k_ref[r, :].astype(BF16) for r in k_rows]
            vs = [v_ref[r, :].astype(BF16) for r in k_rows]
            ss = [_dot_nt(q, k) for q, k in zip(qs, ks)]
            masks = []
            for n, kb in zip(n0, kbase):
                dist = a_minus_a + (n - kb)
                masks.append((dist >= 0) & (dist <= tile))
            ss = [jnp.where(mk, s, NEG_INF) for mk, s in zip(masks, ss)]
            ms = [jnp.max(s, axis=-1, keepdims=True) for s in ss]
            es = [jnp.where(mk, jnp.exp2(s - m), 0.0) for mk, s, m in zip(masks, ss, ms)]
            dens = [jnp.maximum(jnp.sum(e, axis=-1, keepdims=True), 1e-30) for e in es]
            os_ = [_dot((e / den).astype(BF16), v) for e, den, v in zip(es, dens, vs)]
            for r, o, m, den in zip(q_rows, os_, ms, dens):
                og_scr[r, :] = o
                lg_scr[r, :] = jnp.broadcast_to(m + jnp.log2(den), (tile, HEAD_DIM))
            return carry

        lax.fori_loop(0, SEQ // tile // DIL_UNROLL, step, 0)

    rows = 256

    def merge_body(c, carry):
        sl = pl.ds(pl.multiple_of(c * rows, rows), rows)
        la, lb, lc = l0_scr[sl, :], l1_scr[sl, :], l2_scr[sl, :]
        mx = jnp.maximum(jnp.maximum(la, lb), lc)
        ea, eb, ec = jnp.exp2(la - mx), jnp.exp2(lb - mx), jnp.exp2(lc - mx)
        tot = ea + eb + ec
        out = (ea / tot) * o0_scr[sl, :] + (eb / tot) * o1_scr[sl, :] + (ec / tot) * o2_scr[sl, :]
        o_ref[sl, :] = out.astype(o_ref.dtype)
        return carry

    lax.fori_loop(0, SEQ // rows, merge_body, 0)


def dilated_attention(b_qk, b_v, batch):
    hp = DIL_HEADS_PER_GROUP
    in_specs, args = [], []
    for g in range(len(DIL_GROUPS)):
        for arr, c0 in ((b_qk, 0), (b_qk, DIL_HEADS), (b_v, 0)):
            in_specs.append(pl.BlockSpec((SEQ, HEAD_DIM), lambda b, j, c0=c0, g=g: (b, c0 + g * hp + j)))
            args.append(arr)
    return pl.pallas_call(
        _dilated_kernel,
        grid=(batch, hp),
        in_specs=in_specs,
        out_specs=pl.BlockSpec((SEQ, HEAD_DIM), lambda b, j: (b, j)),
        out_shape=jax.ShapeDtypeStruct((batch * SEQ, hp * HEAD_DIM), BF16),
        scratch_shapes=[pltpu.VMEM((SEQ, HEAD_DIM), F32)] * 6,
        compiler_params=_params("parallel", "parallel"),
        name="dilated_attention",
    )(*args)


def _moba_kernel(q_ref, k_ref, vt_ref, o_ref, kmean_scr, bias_scr):
    i = pl.program_id(1)
    tq, nb, d = MOBA_BLOCK, MOBA_NB, HEAD_DIM
    heads = range(MOBA_HEADS)
    col = lambda h: slice(h * d, (h + 1) * d)

    @pl.when(i == 0)
    def _():
        avg = jnp.where(_iota((nb, SEQ), 1) // MOBA_BLOCK == _iota((nb, SEQ), 0), 1.0 / MOBA_BLOCK, 0.0)
        kmean_scr[...] = _dot(avg.astype(BF16), k_ref[...]).astype(kmean_scr.dtype)

    blk = _iota((nb, tq), 0)
    own = pl.multiple_of(i * tq, tq)
    causal = _iota((tq, tq), 0) <= _iota((tq, tq), 1)
    qs = [q_ref[:, col(h)] for h in heads]
    gates = [jnp.where(blk < i, _dot_nt(kmean_scr[:, col(h)], qs[h]), NEG_INF) for h in heads]
    ss = [_dot_nt(k_ref[pl.ds(own, tq), col(h)], qs[h]) for h in heads]
    for h in heads:
        gate = gates[h]
        rank = jnp.zeros((nb, tq), F32)
        for j in range(nb):
            gj = gate[j:j + 1, :]
            ahead = (gj > gate) | ((gj == gate) & (blk > j))
            rank = rank + jnp.where(ahead, 1.0, 0.0)
        bias_scr[h] = jnp.where((rank < float(MOBA_TOPK)) & (gate > NEG_INF * 0.5), 0.0, NEG_INF)
    ss = [jnp.where(causal, s, NEG_INF) for s in ss]
    ms = [jnp.max(s, axis=0, keepdims=True) for s in ss]
    ps = [jnp.exp2(s - m) for s, m in zip(ss, ms)]
    ls = [jnp.sum(p, axis=0, keepdims=True) for p in ps]
    accs = [_dot(vt_ref[col(h), pl.ds(own, tq)], ps[h].astype(BF16)) for h in heads]

    def body(kj, carry):
        off = pl.multiple_of(kj * tq, tq)
        bs = [bias_scr[h, pl.ds(kj, 1), :] for h in heads]
        ss = [_dot_nt(k_ref[pl.ds(off, tq), col(h)], qs[h]) for h in heads]
        m_new = [jnp.maximum(carry[h][0], jnp.max(ss[h], axis=0, keepdims=True) + bs[h]) for h in heads]
        alpha = [jnp.exp2(carry[h][0] - m_new[h]) for h in heads]
        ps = [jnp.exp2(ss[h] - (m_new[h] - bs[h])) for h in heads]
        ls = [alpha[h] * carry[h][1] + jnp.sum(ps[h], axis=0, keepdims=True) for h in heads]
        pvs = [_dot(vt_ref[col(h), pl.ds(off, tq)], ps[h].astype(BF16)) for h in heads]
        return tuple((m_new[h], ls[h], alpha[h] * carry[h][2] + pvs[h]) for h in heads)

    fin = lax.fori_loop(0, i, body, tuple((ms[h], ls[h], accs[h]) for h in heads))
    for h in heads:
        _, l, acc = fin[h]
        o_ref[:, col(h)] = jnp.transpose(acc / l).astype(o_ref.dtype)


def moba_attention(c_qk, vt, batch):
    tq = MOBA_BLOCK
    nq = SEQ // tq
    return pl.pallas_call(
        _moba_kernel,
        grid=(batch, nq),
        in_specs=[pl.BlockSpec((tq, C_QKV), lambda b, i: (b * nq + i, 0)),
                  pl.BlockSpec((SEQ, C_QKV), lambda b, i: (b, 1)),
                  pl.BlockSpec((C_QKV, SEQ), lambda b, i: (0, b))],
        out_specs=pl.BlockSpec((tq, C_QKV), lambda b, i: (b * nq + i, 0)),
        out_shape=jax.ShapeDtypeStruct((batch * SEQ, C_QKV), BF16),
        scratch_shapes=[pltpu.VMEM((MOBA_NB, C_QKV), BF16), pltpu.VMEM((MOBA_HEADS, MOBA_NB, tq), F32)],
        compiler_params=_params("parallel", "arbitrary"),
        name="moba_attention",
    )(c_qk, c_qk, vt)


def _merge_kernel(oa_ref, ob_ref, oc_ref, ga_ref, gb_ref, gc_ref, wa_ref, wb_ref, wc_ref, out_ref):
    y = ga_ref[...] * _dot(oa_ref[...], wa_ref[...].astype(BF16))
    y = y + gb_ref[...] * _dot(ob_ref[...], wb_ref[...].astype(BF16))
    y = y + gc_ref[...] * _dot(oc_ref[...], wc_ref[...].astype(BF16))
    out_ref[...] = y.astype(out_ref.dtype)


def gated_merge(o_a, o_b, o_c, gates_m, w_a, w_b, w_c, layer, tm=1024, tn=512):
    m = o_a.shape[0]
    nb = D_MODEL // tn
    act = lambda width: pl.BlockSpec((tm, width), lambda j, i: (i, 0))
    gate = lambda g: pl.BlockSpec((tm, tn), lambda j, i, g=g: (i, g * nb + j))
    wgt = lambda width: pl.BlockSpec((None, width, tn), lambda j, i: (layer, 0, j))
    return pl.pallas_call(
        _merge_kernel,
        grid=(nb, m // tm),
        in_specs=[act(o_a.shape[1]), act(o_b.shape[1]), act(o_c.shape[1]),
                  gate(0), gate(1), gate(2),
                  wgt(w_a.shape[1]), wgt(w_b.shape[1]), wgt(w_c.shape[1])],
        out_specs=pl.BlockSpec((tm, tn), lambda j, i: (i, j)),
        out_shape=jax.ShapeDtypeStruct((m, D_MODEL), BF16),
        compiler_params=_params("parallel", "parallel"),
        name="gated_merge",
    )(o_a, o_b, o_c, gates_m, gates_m, gates_m, w_a, w_b, w_c)


def _out_proj_kernel(y_ref, w_ref, x_ref, g_ref, xo_ref, h_ref):
    x_new = x_ref[...] + _dot(y_ref[...], w_ref[...])
    xo_ref[...] = x_new
    h_ref[...] = _rmsnorm_rows(x_new, g_ref[...]).astype(h_ref.dtype)


def out_proj_residual_norm(y, w_o, layer, x2, g, tm=512):
    m, d = x2.shape
    row = pl.BlockSpec((tm, d), lambda i: (i, 0))
    return pl.pallas_call(
        _out_proj_kernel,
        grid=(m // tm,),
        in_specs=[row, pl.BlockSpec((None, d, d), lambda i: (layer, 0, 0)), row,
                  pl.BlockSpec((1, d), lambda i: (0, 0))],
        out_specs=[row, row],
        out_shape=[jax.ShapeDtypeStruct((m, d), F32), jax.ShapeDtypeStruct((m, d), BF16)],
        compiler_params=_params("parallel"),
        name="out_proj",
    )(y, w_o, x2, g.reshape(1, d))


def _mlp_kernel(h_ref, w1_ref, w2_ref, x_ref, g_ref, *outs):
    acc_ref, hn_ref = outs[0], outs[-1]
    f = pl.program_id(1)

    @pl.when(f == 0)
    def _():
        acc_ref[...] = x_ref[...]

    u = jnp.square(jnp.maximum(_dot(h_ref[...], w1_ref[...].astype(BF16)), 0.0))
    acc_ref[...] += _dot(u.astype(BF16), w2_ref[...].astype(BF16))

    @pl.when(f == pl.num_programs(1) - 1)
    def _():
        hn_ref[...] = _rmsnorm_rows(acc_ref[...], g_ref[...]).astype(hn_ref.dtype)


def mlp_residual_norm(h2, w1, w2, layer, x2, g_next, next_dtype, emit_x, tm=1024, tf=512):
    m, d = x2.shape
    row = pl.BlockSpec((tm, d), lambda i, f: (i, 0))
    row_once = pl.BlockSpec((tm, d), lambda i, f: (i, 0), pipeline_mode=pl.Buffered(1))
    out_shape = [jax.ShapeDtypeStruct((m, d), F32)] * emit_x + [jax.ShapeDtypeStruct((m, d), next_dtype)]
    res = pl.pallas_call(
        _mlp_kernel,
        grid=(m // tm, D_FF // tf),
        in_specs=[row, pl.BlockSpec((None, d, tf), lambda i, f: (layer, 0, f)),
                  pl.BlockSpec((None, tf, d), lambda i, f: (layer, f, 0)),
                  row_once, pl.BlockSpec((1, d), lambda i, f: (0, 0))],
        out_specs=[row_once] * len(out_shape),
        out_shape=out_shape,
        compiler_params=_params("parallel", "arbitrary"),
        name="mlp",
    )(h2, w1, w2, x2, g_next.reshape(1, d))
    return (res[0], res[1]) if emit_x else (None, res[0])


def _rope_tables():
    inv = ROPE_THETA ** (-jnp.arange(0, ROPE_DIM, 2, dtype=F32) / ROPE_DIM)
    ang = jnp.arange(SEQ, dtype=F32)[:, None] * inv[None, :]
    cos, sin = jnp.cos(ang), jnp.sin(ang)
    zeros = jnp.zeros((SEQ, HEAD_DIM - ROPE_DIM), F32)
    zero_h = jnp.zeros((SEQ, ROPE_HALF), F32)
    c = jnp.concatenate([cos, cos, jnp.ones_like(zeros)], axis=1)
    s_up = jnp.concatenate([-sin, zero_h, zeros], axis=1)
    s_dn = jnp.concatenate([zero_h, sin, zeros], axis=1)
    return c, s_up, s_dn


def _overlap_table_t():
    cs = np.arange(NSA_M_PAD)[None, :] * NSA_CMP_STRIDE
    bs = np.arange(NSA_NB)[:, None] * NSA_SEL_BLOCK
    ov = np.clip(np.minimum(cs + NSA_CMP_LEN, bs + NSA_SEL_BLOCK) - np.maximum(cs, bs), 0, None) / NSA_CMP_LEN
    ov[:, NSA_M_PAD - 1] = 0.0
    return jnp.asarray(ov, dtype=BF16)


W_IN_NAMES = ("a_q", "a_kc", "a_vc", "a_ks", "a_vs", "a_kw", "a_vw", "a_g",
              "b_q", "b_k", "b_v", "c_q", "c_k", "c_v", "m_a", "m_b", "m_c")
W_IN_START = dict(zip(W_IN_NAMES, np.cumsum((0,) + IN_SPLIT_SIZES[:-1]).tolist()))
W_IN_SIZE = dict(zip(W_IN_NAMES, IN_SPLIT_SIZES))


def _scales(*widths_and_values):
    return jnp.concatenate([jnp.full((1, w), v, F32) for w, v in widths_and_values], axis=1)


def _layer(x2, h, batch, layer, tabs, overlap_t, w_in_t, pe_k, w1_k, w2_k, pe_v, w1_v, w2_v,
           w_br_a, w_br_b, w_br_c, w_o, mlp_g, w_mlp_in, w_mlp_out, g_next, last):
    def proj(first, n, *args):
        return project(h, w_in_t, layer, W_IN_START[first], n, *args)

    def proj_t(name):
        return project_transposed(h, w_in_t, layer, W_IN_START[name], W_IN_SIZE[name], BF16)

    q_raw, q_rot = proj("a_q", A_Q, "both", BF16, tabs, _scales((A_Q, Q_SCALE)))
    cmp_src, k_sel, k_win, vt_sel, vt_win, gates_a = project_nsa_side(h, w_in_t, layer, tabs)
    b_qk = proj("b_q", 2 * B_QKV, "rope", F32, tabs, _scales((B_QKV, Q_SCALE), (B_QKV, 1.0)))
    b_v = proj("b_v", B_QKV, "plain", F32)
    c_qk = proj("c_q", 2 * C_QKV, "rope", BF16, tabs, _scales((C_QKV, Q_SCALE), (C_QKV, 1.0)))
    gates_m = proj("m_a", 3 * D_MODEL, "sigmoid", BF16)
    vt_moba = proj_t("c_v")

    flat = NSA_CMP_LEN * HEAD_DIM
    cmp_kv = nsa_compress(cmp_src, pe_k.reshape(1, flat), w1_k.reshape(flat, NSA_CMP_HIDDEN).astype(BF16),
                          w2_k.astype(BF16), pe_v.reshape(1, flat),
                          w1_v.reshape(flat, NSA_CMP_HIDDEN).astype(BF16), w2_v.astype(BF16), batch)
    o_a = nsa_attention(q_raw, q_rot, cmp_kv, k_sel, k_win, vt_sel, vt_win, gates_a, overlap_t, batch)
    o_b = dilated_attention(b_qk, b_v, batch)
    o_c = moba_attention(c_qk, vt_moba, batch)

    merged = gated_merge(o_a, o_b, o_c, gates_m, w_br_a, w_br_b, w_br_c, layer)
    x2, h2 = out_proj_residual_norm(merged, w_o, layer, x2, mlp_g)
    return mlp_residual_norm(h2, w_mlp_in, w_mlp_out, layer, x2, g_next, F32 if last else BF16, emit_x=not last)


def kernel(x, attn_norm_g, w_in, cmp_pe_k, cmp_w1_k, cmp_w2_k, cmp_pe_v, cmp_w1_v, cmp_w2_v,
           w_br_a, w_br_b, w_br_c, w_o, mlp_norm_g, w_mlp_in, w_mlp_out, final_norm_g):
    batch, seq, d = x.shape
    assert seq == SEQ and d == D_MODEL
    depth = w_in.shape[0]
    tabs = _rope_tables()
    overlap_t = _overlap_table_t()
    w_in_t = jnp.transpose(w_in, (0, 2, 1))
    w_o = w_o.astype(BF16)
    x2 = x.reshape(batch * seq, d)
    h = rmsnorm(x2, attn_norm_g[0], BF16)
    for l in range(depth):
        last = l == depth - 1
        g_next = final_norm_g if last else attn_norm_g[l + 1]
        x2, h = _layer(x2, h, batch, l, tabs, overlap_t, w_in_t,
                       cmp_pe_k[l], cmp_w1_k[l], cmp_w2_k[l], cmp_pe_v[l], cmp_w1_v[l], cmp_w2_v[l],
                       w_br_a, w_br_b, w_br_c, w_o, mlp_norm_g[l], w_mlp_in, w_mlp_out, g_next, last)
    return h.reshape(batch, seq, d)
```
